```python
import jax, jax.numpy as jnp
from jax import lax
import numpy as np

D_MODEL = 2048
BATCH = 8
SEQ = 2048
DEPTH = 1

MEM_LEN = 256
EPS = 1e-6
GN_EPS = 1e-5
ROPE_THETA = 10000.0
HEAD_DIM = 64
ATT_Q_HEADS = 16
ATT_KV_HEADS = 2
ATT_WIDTH = ATT_Q_HEADS * HEAD_DIM
WINDOW = 128
BLOCK = 128
RET_HEADS = 4
RET_HEAD_DIM = 256
RET_WIDTH = RET_HEADS * RET_HEAD_DIM
RET_CHUNK = 128
MIX_WIDTH = ATT_WIDTH + RET_WIDTH
ATT_KV_WIDTH = ATT_KV_HEADS * HEAD_DIM
IN_COLS = ATT_WIDTH + 2 * ATT_KV_WIDTH + 4 * RET_WIDTH
XATT_HEADS = 4
XATT_HEAD_DIM = 128
XATT_WIDTH = XATT_HEADS * XATT_HEAD_DIM
PEER_HEADS = 8
PEER_N_KEYS = 128
PEER_N_EXPERTS = PEER_N_KEYS * PEER_N_KEYS
PEER_QUERY_DIM = 256
PEER_HALF_DIM = PEER_QUERY_DIM // 2
PEER_TOPK = 16
PEER_TOKEN_BLOCK = 128

kernel_name = "hymba_swa_retnet_peer_hybrid"


def rmsnorm(x, g):
    xf = x.astype(jnp.float32)
    y = xf * lax.rsqrt(jnp.mean(xf * xf, axis=-1, keepdims=True) + EPS)
    return (y * g.astype(jnp.float32)).astype(x.dtype)


def rope(x, positions):
    dh = x.shape[-1]
    inv_freq = ROPE_THETA ** (-jnp.arange(0, dh, 2, dtype=jnp.float32) / dh)
    ang = positions.astype(jnp.float32)[..., None] * inv_freq
    cos = jnp.cos(ang)[:, :, None, :]
    sin = jnp.sin(ang)[:, :, None, :]
    xf = x.astype(jnp.float32)
    x1, x2 = xf[..., : dh // 2], xf[..., dh // 2:]
    out = jnp.concatenate([x1 * cos - x2 * sin, x2 * cos + x1 * sin], axis=-1)
    return out.astype(x.dtype)


def sliding_window_attention(q, k, v, sinks):
    B, S = q.shape[0], q.shape[1]
    nb = S // BLOCK
    G = ATT_Q_HEADS // ATT_KV_HEADS
    qb = q.reshape(B, nb, BLOCK, ATT_KV_HEADS, G, HEAD_DIM)
    kb = k.reshape(B, nb, BLOCK, ATT_KV_HEADS, HEAD_DIM)
    vb = v.reshape(B, nb, BLOCK, ATT_KV_HEADS, HEAD_DIM)
    pad = ((0, 0), (1, 0), (0, 0), (0, 0), (0, 0))
    kk = jnp.concatenate([jnp.pad(kb, pad)[:, :-1], kb], axis=2)
    vv = jnp.concatenate([jnp.pad(vb, pad)[:, :-1], vb], axis=2)
    scores = jnp.einsum('bnqhgd,bnkhd->bnhgqk', qb, kk).astype(jnp.float32) * (HEAD_DIM ** -0.5)
    qi = jnp.arange(BLOCK)[:, None] + BLOCK
    ki = jnp.arange(2 * BLOCK)[None, :]
    band = (qi - ki >= 0) & (qi - ki < WINDOW)
    blk = jnp.arange(nb)[:, None, None]
    valid = band[None] & ((blk > 0) | (ki[None] >= BLOCK))
    scores = jnp.where(valid[None, :, None, None], scores, -jnp.inf)
    sink = sinks.astype(jnp.float32).reshape(ATT_KV_HEADS, G)[None, None, :, :, None, None]
    sink = jnp.broadcast_to(sink, scores.shape[:-1] + (1,))
    probs = jax.nn.softmax(jnp.concatenate([scores, sink], axis=-1), axis=-1)[..., :-1]
    out = jnp.einsum('bnhgqk,bnkhd->bnqhgd', probs.astype(v.dtype), vv)
    return out.reshape(B, S, ATT_Q_HEADS, HEAD_DIM)


def retention(q, k, v):
    B, S = q.shape[0], q.shape[1]
    C = RET_CHUNK
    nc = S // C
    qc = q.astype(jnp.float32).reshape(B, nc, C, RET_HEADS, RET_HEAD_DIM)
    kc = k.astype(jnp.float32).reshape(B, nc, C, RET_HEADS, RET_HEAD_DIM) * (RET_HEAD_DIM ** -0.5)
    vc = v.astype(jnp.float32).reshape(B, nc, C, RET_HEADS, RET_HEAD_DIM)
    log_gamma = jnp.log1p(-jnp.exp2(-5.0 - jnp.arange(RET_HEADS, dtype=jnp.float32)))
    idx = jnp.arange(C, dtype=jnp.float32)
    diff = idx[:, None] - idx[None, :]
    decay = jnp.where(diff[None] >= 0, jnp.exp(jnp.maximum(diff, 0.0)[None] * log_gamma[:, None, None]), 0.0)
    inner = jnp.einsum('bnihd,bnjhd->bnhij', qc, kc) * decay[None, None]
    inner_out = jnp.einsum('bnhij,bnjhe->bnihe', inner, vc)
    zeta = jnp.exp((C - 1 - idx)[None, :] * log_gamma[:, None])
    xi = jnp.exp((idx + 1)[None, :] * log_gamma[:, None])
    kv = jnp.einsum('bnjhd,hj,bnjhe->bnhde', kc, zeta, vc)
    chunk_decay = jnp.exp(C * log_gamma)[None, :, None, None]

    def step(state, kv_n):
        return state * chunk_decay + kv_n, state

    init = jnp.zeros((B, RET_HEADS, RET_HEAD_DIM, RET_HEAD_DIM), jnp.float32)
    _, states = lax.scan(step, init, jnp.moveaxis(kv, 1, 0))
    states = jnp.moveaxis(states, 0, 1)
    cross = jnp.einsum('bnihd,hi,bnhde->bnihe', qc, xi, states)
    return (inner_out + cross).reshape(B, S, RET_HEADS, RET_HEAD_DIM)


def hybrid_mixer(xn, positions, w_in, sinks, att_gain, ret_gain, w_out):
    B, S, _ = xn.shape
    dt = xn.dtype
    h = xn @ w_in
    splits = [ATT_WIDTH, ATT_WIDTH + ATT_KV_WIDTH, ATT_WIDTH + 2 * ATT_KV_WIDTH]
    base = ATT_WIDTH + 2 * ATT_KV_WIDTH
    splits += [base + RET_WIDTH, base + 2 * RET_WIDTH, base + 3 * RET_WIDTH]
    qa, ka, va, qr, kr, vr, gr = jnp.split(h, splits, axis=-1)
    qa = rope(qa.reshape(B, S, ATT_Q_HEADS, HEAD_DIM), positions)
    ka = rope(ka.reshape(B, S, ATT_KV_HEADS, HEAD_DIM), positions)
    va = va.reshape(B, S, ATT_KV_HEADS, HEAD_DIM)
    oa = sliding_window_attention(qa, ka, va, sinks).astype(jnp.float32)
    oa = oa * lax.rsqrt(jnp.mean(oa * oa, axis=-1, keepdims=True) + EPS)
    oa = oa.reshape(B, S, ATT_WIDTH) * att_gain.astype(jnp.float32)
    qr = rope(qr.reshape(B, S, RET_HEADS, RET_HEAD_DIM), positions)
    kr = rope(kr.reshape(B, S, RET_HEADS, RET_HEAD_DIM), positions)
    vr = vr.reshape(B, S, RET_HEADS, RET_HEAD_DIM)
    orr = retention(qr, kr, vr)
    mu = jnp.mean(orr, axis=-1, keepdims=True)
    var = jnp.mean(jnp.square(orr - mu), axis=-1, keepdims=True)
    orr = ((orr - mu) * lax.rsqrt(var + GN_EPS)).reshape(B, S, RET_WIDTH) * ret_gain.astype(jnp.float32)
    orr = orr * jax.nn.silu(gr.astype(jnp.float32))
    y = jnp.concatenate([oa, orr], axis=-1).astype(dt)
    return y @ w_out


def memory_cross_attention(xn, memn, w_xq, w_xk, w_xv, w_xo):
    B, S, _ = xn.shape
    M = memn.shape[1]
    q = (xn @ w_xq).reshape(B, S, XATT_HEADS, XATT_HEAD_DIM)
    k = (memn @ w_xk).reshape(B, M, XATT_HEADS, XATT_HEAD_DIM)
    v = (memn @ w_xv).reshape(B, M, XATT_HEADS, XATT_HEAD_DIM)
    s = jnp.einsum('bshd,bmhd->bhsm', q, k).astype(jnp.float32) * (XATT_HEAD_DIM ** -0.5)
    p = jax.nn.softmax(s, axis=-1).astype(v.dtype)
    o = jnp.einsum('bhsm,bmhd->bshd', p, v).reshape(B, S, XATT_WIDTH)
    return o @ w_xo


def peer_ffn(xn, w_pq, sub_keys, expert_u, expert_v):
    B, S, D = xn.shape
    T = B * S
    xt = xn.reshape(T, D)
    q = (xt @ w_pq).reshape(T, PEER_HEADS, 2, PEER_HALF_DIM)
    s = jnp.einsum('thpd,hpkd->thpk', q, sub_keys).astype(jnp.float32)
    s1, i1 = lax.top_k(s[:, :, 0], PEER_TOPK)
    s2, i2 = lax.top_k(s[:, :, 1], PEER_TOPK)
    cand_s = (s1[..., :, None] + s2[..., None, :]).reshape(T, PEER_HEADS, PEER_TOPK * PEER_TOPK)
    cand_i = (i1[..., :, None] * PEER_N_KEYS + i2[..., None, :]).reshape(T, PEER_HEADS, PEER_TOPK * PEER_TOPK)
    top_s, top_pos = lax.top_k(cand_s, PEER_TOPK)
    experts = jnp.take_along_axis(cand_i, top_pos, axis=-1)
    gates = jax.nn.softmax(top_s, axis=-1).astype(xn.dtype)
    nblk = T // PEER_TOKEN_BLOCK
    hk = PEER_HEADS * PEER_TOPK

    def block(args):
        xb, eb, gb = args
        u = jnp.take(expert_u, eb, axis=0)
        act = jax.nn.gelu(jnp.einsum('tkd,td->tk', u, xb), approximate=False)
        v = jnp.take(expert_v, eb, axis=0)
        return jnp.einsum('tk,tkd->td', gb * act, v)

    y = lax.map(block, (xt.reshape(nblk, PEER_TOKEN_BLOCK, D),
                        experts.reshape(nblk, PEER_TOKEN_BLOCK, hk),
                        gates.reshape(nblk, PEER_TOKEN_BLOCK, hk)))
    return y.reshape(B, S, D)


def setup_inputs(seed: int = 0) -> dict:
    key = jax.random.key(seed)
    ks = jax.random.split(key, 24)
    nrm = jax.random.normal
    f32 = jnp.float32

    def gain(k, n):
        return 1.0 + 0.02 * nrm(k, (DEPTH, n), f32)

    x = nrm(ks[0], (BATCH, SEQ, D_MODEL), f32)
    mem = nrm(ks[1], (BATCH, MEM_LEN, D_MODEL), f32)
    offsets = jax.random.randint(ks[2], (BATCH, 1), 0, 4096, dtype=jnp.int32)
    positions = offsets + jnp.arange(SEQ, dtype=jnp.int32)[None, :]
    return {
        "x": x,
        "mem": mem,
        "positions": positions,
        "mix_norm_gain": gain(ks[3], D_MODEL),
        "w_in": nrm(ks[4], (DEPTH, D_MODEL, IN_COLS), f32) * D_MODEL ** -0.5,
        "att_sinks": nrm(ks[5], (DEPTH, ATT_Q_HEADS), f32),
        "att_out_gain": gain(ks[6], ATT_WIDTH),
        "ret_out_gain": gain(ks[7], RET_WIDTH),
        "w_out": nrm(ks[8], (DEPTH, MIX_WIDTH, D_MODEL), f32) * MIX_WIDTH ** -0.5,
        "cross_norm_gain": gain(ks[9], D_MODEL),
        "mem_norm_gain": gain(ks[10], D_MODEL),
        "w_xq": nrm(ks[11], (DEPTH, D_MODEL, XATT_WIDTH), f32) * D_MODEL ** -0.5,
        "w_xk": nrm(ks[12], (DEPTH, D_MODEL, XATT_WIDTH), f32) * D_MODEL ** -0.5,
        "w_xv": nrm(ks[13], (DEPTH, D_MODEL, XATT_WIDTH), f32) * D_MODEL ** -0.5,
        "w_xo": nrm(ks[14], (DEPTH, XATT_WIDTH, D_MODEL), f32) * XATT_WIDTH ** -0.5,
        "ffn_norm_gain": gain(ks[15], D_MODEL),
        "w_peer_q": nrm(ks[16], (DEPTH, D_MODEL, PEER_HEADS * PEER_QUERY_DIM), f32) * D_MODEL ** -0.5,
        "peer_sub_keys": nrm(ks[17], (DEPTH, PEER_HEADS, 2, PEER_N_KEYS, PEER_HALF_DIM), f32) * PEER_HALF_DIM ** -0.5,
        "peer_u": nrm(ks[18], (DEPTH, PEER_N_EXPERTS, D_MODEL), f32) * D_MODEL ** -0.5,
        "peer_v": nrm(ks[19], (DEPTH, PEER_N_EXPERTS, D_MODEL), f32) * (PEER_HEADS * PEER_TOPK) ** -0.5,
        "final_norm_gain": 1.0 + 0.02 * nrm(ks[20], (D_MODEL,), f32),
    }


def reference(x, mem, positions, mix_norm_gain, w_in, att_sinks, att_out_gain, ret_out_gain, w_out,
              cross_norm_gain, mem_norm_gain, w_xq, w_xk, w_xv, w_xo, ffn_norm_gain,
              w_peer_q, peer_sub_keys, peer_u, peer_v, final_norm_gain):
    for l in range(DEPTH):
        x = x + hybrid_mixer(rmsnorm(x, mix_norm_gain[l]), positions, w_in[l], att_sinks[l],
                             att_out_gain[l], ret_out_gain[l], w_out[l])
        x = x + memory_cross_attention(rmsnorm(x, cross_norm_gain[l]), rmsnorm(mem, mem_norm_gain[l]),
                                       w_xq[l], w_xk[l], w_xv[l], w_xo[l])
        x = x + peer_ffn(rmsnorm(x, ffn_norm_gain[l]), w_peer_q[l], peer_sub_keys[l], peer_u[l], peer_v[l])
    return rmsnorm(x, final_norm_gain)
```

```python
import functools
import math

import jax
import jax.numpy as jnp
from jax import lax
from jax.experimental import pallas as pl
from jax.experimental.pallas import tpu as pltpu

F32 = jnp.float32
BF16 = jnp.bfloat16

EPS = 1e-6
GN_EPS = 1e-5
ROPE_THETA = 10000.0
D_MODEL = 2048
HEAD_DIM = 64
ATT_Q_HEADS = 16
ATT_KV_HEADS = 2
ATT_WIDTH = ATT_Q_HEADS * HEAD_DIM
ATT_KV_WIDTH = ATT_KV_HEADS * HEAD_DIM
BLOCK = 128
RET_HEADS = 4
RET_HEAD_DIM = 256
RET_WIDTH = RET_HEADS * RET_HEAD_DIM
RET_CHUNK = 128
IN_COLS = ATT_WIDTH + 2 * ATT_KV_WIDTH + 4 * RET_WIDTH
XATT_HEADS = 4
XATT_HEAD_DIM = 128
XATT_WIDTH = XATT_HEADS * XATT_HEAD_DIM
PEER_HEADS = 8
PEER_N_KEYS = 128
PEER_HALF_DIM = 128
PEER_TOPK = 16
PEER_SLOTS = PEER_HEADS * PEER_TOPK

LANES = 128
SUBLANES = 8
VMEM_LIMIT = 56 * 1024 * 1024

_KA0 = ATT_WIDTH
_VA0 = _KA0 + ATT_KV_WIDTH
_QR0 = _VA0 + ATT_KV_WIDTH
_KR0 = _QR0 + RET_WIDTH
_VR0 = _KR0 + RET_WIDTH
_GR0 = _VR0 + RET_WIDTH


def _params(*sem):
    return pltpu.CompilerParams(dimension_semantics=sem, vmem_limit_bytes=VMEM_LIMIT)


def _rms(xf, gain):
    ms = jnp.mean(xf * xf, axis=-1, keepdims=True)
    return xf * lax.rsqrt(ms + EPS) * gain


def _dot(a, b):
    return jnp.dot(a, b, preferred_element_type=F32)


def _dot_nt(a, b):
    return lax.dot_general(a, b, (((1,), (1,)), ((), ())), preferred_element_type=F32)


def _norm_matmul_kernel(x_ref, g_ref, w_ref, o_ref, xn_ref):
    @pl.when(pl.program_id(1) == 0)
    def _():
        xn_ref[...] = _rms(x_ref[...], g_ref[...]).astype(BF16)

    o_ref[...] = _dot(xn_ref[...], w_ref[...]).astype(o_ref.dtype)


def _norm_matmul(x, gain, w_bf16, tm, tn, out_dtype):
    t, d = x.shape
    n = w_bf16.shape[1]
    return pl.pallas_call(
        _norm_matmul_kernel,
        grid=(t // tm, n // tn),
        in_specs=[
            pl.BlockSpec((tm, d), lambda i, j: (i, 0)),
            pl.BlockSpec((1, d), lambda i, j: (0, 0)),
            pl.BlockSpec((d, tn), lambda i, j: (0, j)),
        ],
        out_specs=pl.BlockSpec((tm, tn), lambda i, j: (i, j)),
        out_shape=jax.ShapeDtypeStruct((t, n), out_dtype),
        scratch_shapes=[pltpu.VMEM((tm, d), BF16)],
        compiler_params=_params("parallel", "arbitrary"),
        name="norm_matmul",
    )(x, gain.reshape(1, d), w_bf16)


def _swa_kernel(sink_ref, q_ref, kc_ref, kp_ref, vc_ref, vp_ref, cc_ref, sc_ref, cp_ref, sp_ref,
                gain_ref, o_ref):
    n = pl.program_id(1)
    lane = lax.broadcasted_iota(jnp.int32, (1, LANES), 1)
    first_half = (lane % HEAD_DIM) < (HEAD_DIM // 2)
    lo = lane < HEAD_DIM

    def rope(x, c, s):
        partner = jnp.where(first_half, pltpu.roll(x, LANES - HEAD_DIM // 2, 1),
                            pltpu.roll(x, HEAD_DIM // 2, 1))
        return x * c + partner * s

    cc, sc = cc_ref[...], sc_ref[...]
    k = jnp.concatenate([rope(kp_ref[...], cp_ref[...], sp_ref[...]), rope(kc_ref[...], cc, sc)], axis=0)
    v = jnp.concatenate([vp_ref[...], vc_ref[...]], axis=0)
    k_sw = pltpu.roll(k, HEAD_DIM, 1)
    v_sw = pltpu.roll(v, HEAD_DIM, 1)

    def place(a, a_sw, c, half):
        src = a if c == half else a_sw
        keep = lo if half == 0 else jnp.logical_not(lo)
        return jnp.where(keep, src, 0.0).astype(BF16)

    kvar = {(c, h): place(k, k_sw, c, h) for c in range(ATT_KV_HEADS) for h in range(2)}
    vvar = {(c, h): place(v, v_sw, c, h) for c in range(ATT_KV_HEADS) for h in range(2)}

    qi = lax.broadcasted_iota(jnp.int32, (BLOCK, 2 * BLOCK), 0) + BLOCK
    ki = lax.broadcasted_iota(jnp.int32, (BLOCK, 2 * BLOCK), 1)
    dist = qi - ki
    kmin = jnp.where(n > 0, 0, BLOCK)
    valid = (dist >= 0) & (dist < BLOCK) & (ki >= kmin)

    group = ATT_Q_HEADS // ATT_KV_HEADS
    for j in range(ATT_WIDTH // LANES):
        c = (2 * j) // group
        cols = slice(j * LANES, (j + 1) * LANES)
        qg = (rope(q_ref[:, cols], cc, sc) * (HEAD_DIM ** -0.5)).astype(BF16)
        out = jnp.zeros((BLOCK, LANES), F32)
        for half in range(2):
            s = jnp.where(valid, _dot_nt(qg, kvar[(c, half)]), -jnp.inf)
            sink = sink_ref[2 * j + half]
            m = jnp.maximum(jnp.max(s, axis=-1, keepdims=True), sink)
            p = jnp.exp(s - m)
            denom = jnp.sum(p, axis=-1, keepdims=True) + jnp.exp(sink - m)
            out = out + _dot((p / denom).astype(BF16), vvar[(c, half)])
        sq = out * out
        ss_lo = jnp.sum(jnp.where(lo, sq, 0.0), axis=-1, keepdims=True)
        ss_hi = jnp.sum(jnp.where(lo, 0.0, sq), axis=-1, keepdims=True)
        ms = jnp.where(lo, ss_lo, ss_hi) * (1.0 / HEAD_DIM)
        o_ref[:, cols] = (out * lax.rsqrt(ms + EPS) * gain_ref[:, cols]).astype(o_ref.dtype)


def _swa(h, cos_a, sin_a, sinks, gain, batch, seq):
    nb = seq // BLOCK
    t = batch * seq
    kcol, vcol = _KA0 // LANES, _VA0 // LANES
    cur = lambda b, n: b * nb + n
    prev = lambda b, n: b * nb + jnp.maximum(n - 1, 0)
    row = lambda col, f: pl.BlockSpec((BLOCK, LANES), lambda b, n: (f(b, n), col))
    return pl.pallas_call(
        _swa_kernel,
        grid=(batch, nb),
        in_specs=[
            pl.BlockSpec(memory_space=pltpu.SMEM),
            pl.BlockSpec((BLOCK, ATT_WIDTH), lambda b, n: (cur(b, n), 0)),
            row(kcol, cur), row(kcol, prev), row(vcol, cur), row(vcol, prev),
            row(0, cur), row(0, cur), row(0, prev), row(0, prev),
            pl.BlockSpec((1, ATT_WIDTH), lambda b, n: (0, 0)),
        ],
        out_specs=pl.BlockSpec((BLOCK, ATT_WIDTH), lambda b, n: (cur(b, n), 0)),
        out_shape=jax.ShapeDtypeStruct((t, ATT_WIDTH), BF16),
        compiler_params=_params("parallel", "parallel"),
        name="swa_attention",
    )(sinks, h, h, h, h, h, cos_a, sin_a, cos_a, sin_a, gain.reshape(1, ATT_WIDTH))


def _ret_kernel(lg_ref, cd_ref, q_ref, k_ref, v_ref, g_ref, c_ref, s_ref, gain_ref, o_ref, state_ref):
    hh = pl.program_id(1)
    n = pl.program_id(2)

    @pl.when(n == 0)
    def _():
        state_ref[...] = jnp.zeros_like(state_ref)

    lg = lg_ref[hh]
    c, s = c_ref[...], s_ref[...]
    half = RET_HEAD_DIM // 2

    def rope(x):
        x1, x2 = x[:, :half], x[:, half:]
        return jnp.concatenate([x1 * c - x2 * s, x2 * c + x1 * s], axis=1)

    q = rope(q_ref[...])
    k = rope(k_ref[...]) * (RET_HEAD_DIM ** -0.5)
    v = v_ref[...].astype(BF16)

    ri = lax.broadcasted_iota(jnp.int32, (RET_CHUNK, RET_CHUNK), 0).astype(F32)
    ci = lax.broadcasted_iota(jnp.int32, (RET_CHUNK, RET_CHUNK), 1).astype(F32)
    diff = ri - ci
    decay = jnp.where(diff >= 0, jnp.exp(jnp.maximum(diff, 0.0) * lg), 0.0)
    zeta = jnp.exp((RET_CHUNK - 1 - ri) * lg)
    xi = jnp.exp((ri + 1.0) * lg)
    zeta2 = jnp.concatenate([zeta, zeta], axis=1)
    xi2 = jnp.concatenate([xi, xi], axis=1)

    inner = _dot_nt(q.astype(BF16), k.astype(BF16)) * decay
    state = state_ref[...]
    out = _dot(inner.astype(BF16), v) + _dot((q * xi2).astype(BF16), state.astype(BF16))
    kv = lax.dot_general((k * zeta2).astype(BF16), v, (((0,), (0,)), ((), ())), preferred_element_type=F32)
    state_ref[...] = state * cd_ref[hh] + kv

    mu = jnp.mean(out, axis=-1, keepdims=True)
    cen = out - mu
    var = jnp.mean(cen * cen, axis=-1, keepdims=True)
    g = g_ref[...]
    o = cen * lax.rsqrt(var + GN_EPS) * gain_ref[...] * (g * jax.nn.sigmoid(g))
    o_ref[...] = o.astype(o_ref.dtype)


def _retention(h, cos_r, sin_r, gain, batch, seq):
    nc = seq // RET_CHUNK
    t = batch * seq
    hd = RET_HEAD_DIM
    log_gamma = jnp.log1p(-jnp.exp2(-5.0 - jnp.arange(RET_HEADS, dtype=F32)))
    chunk_decay = jnp.exp(RET_CHUNK * log_gamma)
    col = lambda c0: pl.BlockSpec((RET_CHUNK, hd), lambda b, hh, n: (b * nc + n, c0 // hd + hh))
    tab = pl.BlockSpec((RET_CHUNK, hd // 2), lambda b, hh, n: (b * nc + n, 0))
    return pl.pallas_call(
        _ret_kernel,
        grid=(batch, RET_HEADS, nc),
        in_specs=[
            pl.BlockSpec(memory_space=pltpu.SMEM),
            pl.BlockSpec(memory_space=pltpu.SMEM),
            col(_QR0), col(_KR0), col(_VR0), col(_GR0), tab, tab,
            pl.BlockSpec((1, hd), lambda b, hh, n: (0, hh)),
        ],
        out_specs=pl.BlockSpec((RET_CHUNK, hd), lambda b, hh, n: (b * nc + n, hh)),
        out_shape=jax.ShapeDtypeStruct((t, RET_WIDTH), BF16),
        scratch_shapes=[pltpu.VMEM((hd, hd), F32)],
        compiler_params=_params("parallel", "parallel", "arbitrary"),
        name="retention",
    )(log_gamma, chunk_decay, h, h, h, h, cos_r, sin_r, gain.reshape(1, RET_WIDTH))


def _outproj_kernel(x_ref, oa_ref, or_ref, wa_ref, wr_ref, g_ref, wq_ref, x1_ref, qx_ref):
    x1 = x_ref[...] + _dot(oa_ref[...], wa_ref[...]) + _dot(or_ref[...], wr_ref[...])
    x1_ref[...] = x1
    qx_ref[...] = _dot(_rms(x1, g_ref[...]).astype(BF16), wq_ref[...]).astype(qx_ref.dtype)


def _outproj(x, oa, orr, w_out_bf16, gain, w_xq_bf16, tm):
    t, d = x.shape
    full = lambda shape: pl.BlockSpec(shape, lambda i: (0, 0))
    return pl.pallas_call(
        _outproj_kernel,
        grid=(t // tm,),
        in_specs=[
            pl.BlockSpec((tm, d), lambda i: (i, 0)),
            pl.BlockSpec((tm, ATT_WIDTH), lambda i: (i, 0)),
            pl.BlockSpec((tm, RET_WIDTH), lambda i: (i, 0)),
            pl.BlockSpec((ATT_WIDTH, d), lambda i: (0, 0)),
            pl.BlockSpec((RET_WIDTH, d), lambda i: (1, 0)),
            full((1, d)),
            full((d, XATT_WIDTH)),
        ],
        out_specs=[pl.BlockSpec((tm, d), lambda i: (i, 0)), pl.BlockSpec((tm, XATT_WIDTH), lambda i: (i, 0))],
        out_shape=[jax.ShapeDtypeStruct((t, d), F32), jax.ShapeDtypeStruct((t, XATT_WIDTH), BF16)],
        compiler_params=_params("parallel"),
        name="out_proj",
    )(x, oa, orr, w_out_bf16, w_out_bf16, gain.reshape(1, d), w_xq_bf16)


def _xattn_kernel(qx_ref, k_ref, v_ref, x1_ref, wo_ref, g_ref, wpq_ref, x2_ref, pq_ref):
    heads = []
    for hh in range(XATT_HEADS):
        cols = slice(hh * XATT_HEAD_DIM, (hh + 1) * XATT_HEAD_DIM)
        s = _dot_nt(qx_ref[:, cols], k_ref[:, cols]) * (XATT_HEAD_DIM ** -0.5)
        p = jnp.exp(s - jnp.max(s, axis=-1, keepdims=True))
        p = p / jnp.sum(p, axis=-1, keepdims=True)
        heads.append(_dot(p.astype(BF16), v_ref[:, cols]))
    o = jnp.concatenate(heads, axis=1).astype(BF16)
    x2 = x1_ref[...] + _dot(o, wo_ref[...])
    x2_ref[...] = x2
    pq_ref[...] = _dot(_rms(x2, g_ref[...]).astype(BF16), wpq_ref[...])


def _xattn(qx, kv_mem, x1, w_xo_bf16, gain, w_pq_bf16, batch, seq, mem_len, tm):
    t, d = x1.shape
    npq = w_pq_bf16.shape[1]
    nt = seq // tm
    rows = lambda width: pl.BlockSpec((tm, width), lambda b, i: (b * nt + i, 0))
    full = lambda shape: pl.BlockSpec(shape, lambda b, i: (0, 0))
    return pl.pallas_call(
        _xattn_kernel,
        grid=(batch, nt),
        in_specs=[
            rows(XATT_WIDTH),
            pl.BlockSpec((mem_len, XATT_WIDTH), lambda b, i: (b, 0)),
            pl.BlockSpec((mem_len, XATT_WIDTH), lambda b, i: (b, 1)),
            rows(d),
            full((XATT_WIDTH, d)),
            full((1, d)),
            full((d, npq)),
        ],
        out_specs=[rows(d), rows(npq)],
        out_shape=[jax.ShapeDtypeStruct((t, d), F32), jax.ShapeDtypeStruct((t, npq), F32)],
        compiler_params=_params("parallel", "parallel"),
        name="cross_attention",
    )(qx, kv_mem, kv_mem, x1, w_xo_bf16, gain.reshape(1, d), w_pq_bf16)


def _topk_rows(s, k, payload=None):
    rows = lax.broadcasted_iota(jnp.int32, s.shape, 0)
    vals, sel = [], []
    for _ in range(k):
        m = jnp.max(s, axis=0, keepdims=True)
        am = jnp.min(jnp.where(s == m, rows, s.shape[0]), axis=0, keepdims=True)
        hit = rows == am
        vals.append(m)
        sel.append(am if payload is None else jnp.max(jnp.where(hit, payload, -1), axis=0, keepdims=True))
        s = jnp.where(hit, -jnp.inf, s)
    return jnp.concatenate(vals, axis=0), jnp.concatenate(sel, axis=0)


def _peer_topk_kernel(q_ref, keys_ref, ids_ref, gates_ref):
    tops = []
    for p in range(2):
        qp = q_ref[:, p * PEER_HALF_DIM:(p + 1) * PEER_HALF_DIM].astype(BF16)
        tops.append(_topk_rows(_dot_nt(keys_ref[0, p], qp), PEER_TOPK))
    (s1, i1), (s2, i2) = tops
    cand_s = jnp.concatenate([s1[a:a + 1, :] + s2 for a in range(PEER_TOPK)], axis=0)
    cand_i = jnp.concatenate([i1[a:a + 1, :] * PEER_N_KEYS + i2 for a in range(PEER_TOPK)], axis=0)
    top_s, top_e = _topk_rows(cand_s, PEER_TOPK, payload=cand_i)
    e = jnp.exp(top_s - top_s[0:1, :])
    ids_ref[0] = top_e
    gates_ref[0] = e / jnp.sum(e, axis=0, keepdims=True)


def _peer_topk(pq, keys_bf16, tt):
    t = pq.shape[0]
    width = 2 * PEER_HALF_DIM
    out = pl.BlockSpec((1, PEER_TOPK, tt), lambda i, hh: (hh, 0, i))
    return pl.pallas_call(
        _peer_topk_kernel,
        grid=(t // tt, PEER_HEADS),
        in_specs=[
            pl.BlockSpec((tt, width), lambda i, hh: (i, hh)),
            pl.BlockSpec((1, 2, PEER_N_KEYS, PEER_HALF_DIM), lambda i, hh: (hh, 0, 0, 0)),
        ],
        out_specs=[out, out],
        out_shape=[jax.ShapeDtypeStruct((PEER_HEADS, PEER_TOPK, t), jnp.int32),
                   jax.ShapeDtypeStruct((PEER_HEADS, PEER_TOPK, t), F32)],
        compiler_params=_params("parallel", "parallel"),
        name="peer_topk",
    )(pq, keys_bf16)


GATHER_TOKENS = 128
GATHER_BUFS = 4


def _peer_gather_kernel(ids_hbm, tbl_hbm, x2_ref, gates_ref, fg_ref, og_ref, out_ref,
                        ids_smem, ids_sem, gbuf, gsem, xn_ref, y_ref, yacc_ref, *, final_norm):
    i = pl.program_id(0)
    nsteps = pl.num_programs(0)
    slot = i % 2
    tb = GATHER_TOKENS
    d = D_MODEL
    nchunk = d // LANES
    nids = tb * PEER_SLOTS

    def ids_copy(step, sl):
        return pltpu.make_async_copy(ids_hbm.at[pl.ds(pl.multiple_of(step * nids, nids), nids)],
                                     ids_smem.at[pl.ds(pl.multiple_of(sl * nids, nids), nids)], ids_sem.at[sl])

    @pl.when(i == 0)
    def _():
        ids_copy(0, 0).start()

    ids_copy(i, slot).wait()

    @pl.when(i + 1 < nsteps)
    def _():
        ids_copy(i + 1, 1 - slot).start()

    xn_ref[...] = _rms(x2_ref[...], fg_ref[...])

    def row_copy(e, b, k):
        return pltpu.make_async_copy(tbl_hbm.at[pl.ds(e, 1)], gbuf.at[b, pl.ds(k, 1)], gsem.at[b])

    def issue(tok, b):
        base = slot * nids + tok * PEER_SLOTS
        for k in range(PEER_SLOTS):
            row_copy(ids_smem[base + k], b, k).start()

    def wait_rows(b):
        for k in range(PEER_SLOTS):
            row_copy(0, b, k).wait()

    for tok in range(GATHER_BUFS - 1):
        issue(tok, tok)

    lane = lax.broadcasted_iota(jnp.int32, (PEER_SLOTS, LANES), 1)
    sub = lax.broadcasted_iota(jnp.int32, (SUBLANES, d), 0)
    hi_mask = jnp.uint32(0xFFFF0000)

    def body(tok, carry):
        b = tok % GATHER_BUFS
        nxt = tok + GATHER_BUFS - 1

        @pl.when(nxt < tb)
        def _():
            issue(nxt, nxt % GATHER_BUFS)

        wait_rows(b)
        xrow = xn_ref[pl.ds(tok, 1), :]
        acc = jnp.zeros((PEER_SLOTS, LANES), F32)
        for c in range(nchunk):
            w = gbuf[b, :, c * LANES:(c + 1) * LANES]
            u = lax.bitcast_convert_type(w << 16, F32)
            acc = acc + u * xrow[:, c * LANES:(c + 1) * LANES]
        a = jnp.sum(acc, axis=-1, keepdims=True)
        gate = jnp.sum(jnp.where(lane == tok, gates_ref[...], 0.0), axis=-1, keepdims=True)
        coeff = 0.5 * a * (1.0 + lax.erf(a * (2.0 ** -0.5))) * gate
        pieces = []
        for c in range(nchunk):
            w = gbuf[b, :, c * LANES:(c + 1) * LANES]
            vv = lax.bitcast_convert_type(w & hi_mask, F32)
            pieces.append(jnp.sum(vv * coeff, axis=0, keepdims=True))
        yrow = jnp.concatenate(pieces, axis=1)
        r = tok % SUBLANES
        yacc_ref[...] = jnp.where(sub == r, yrow, yacc_ref[...])

        @pl.when(r == SUBLANES - 1)
        def _():
            start = pl.multiple_of(tok - (SUBLANES - 1), SUBLANES)
            y_ref[pl.ds(start, SUBLANES), :] = yacc_ref[...]

        return carry

    lax.fori_loop(0, tb, body, 0)
    x3 = x2_ref[...] + y_ref[...]
    out_ref[...] = _rms(x3, og_ref[...]) if final_norm else x3


def _peer_gather(ids_flat, table, x2, gates_t, ffn_gain, final_gain, final_norm):
    t, d = x2.shape
    tb = GATHER_TOKENS
    nsteps = t // tb
    full = lambda shape: pl.BlockSpec(shape, lambda i: (0, 0))
    return pl.pallas_call(
        functools.partial(_peer_gather_kernel, final_norm=final_norm),
        grid=(nsteps,),
        in_specs=[
            pl.BlockSpec(memory_space=pl.ANY),
            pl.BlockSpec(memory_space=pl.ANY),
            pl.BlockSpec((tb, d), lambda i: (i, 0)),
            pl.BlockSpec((PEER_SLOTS, tb), lambda i: (0, i)),
            full((1, d)),
            full((1, d)),
        ],
        out_specs=pl.BlockSpec((tb, d), lambda i: (i, 0)),
        out_shape=jax.ShapeDtypeStruct((t, d), F32),
        scratch_shapes=[
            pltpu.SMEM((2 * tb * PEER_SLOTS,), jnp.int32),
            pltpu.SemaphoreType.DMA((2,)),
            pltpu.VMEM((GATHER_BUFS, PEER_SLOTS, d), jnp.uint32),
            pltpu.SemaphoreType.DMA((GATHER_BUFS,)),
            pltpu.VMEM((tb, d), F32),
            pltpu.VMEM((tb, d), F32),
            pltpu.VMEM((SUBLANES, d), F32),
        ],
        compiler_params=_params("arbitrary"),
        name="peer_gather",
    )(ids_flat, table, x2, gates_t, ffn_gain.reshape(1, d), final_gain.reshape(1, d))


def _rope_angles(positions, dh):
    inv_freq = ROPE_THETA ** (-jnp.arange(0, dh, 2, dtype=F32) / dh)
    ang = positions.astype(F32).reshape(-1, 1) * inv_freq
    return jnp.cos(ang), jnp.sin(ang)


def _rope_tables_att(positions):
    cos, sin = _rope_angles(positions, HEAD_DIM)
    copies = LANES // HEAD_DIM
    return (jnp.tile(jnp.concatenate([cos, cos], axis=1), (1, copies)),
            jnp.tile(jnp.concatenate([-sin, sin], axis=1), (1, copies)))


def _pack_expert_table(u, v):
    ub = lax.bitcast_convert_type(u.astype(BF16), jnp.uint16).astype(jnp.uint32)
    vb = lax.bitcast_convert_type(v.astype(BF16), jnp.uint16).astype(jnp.uint32)
    return (vb << 16) | ub


def kernel(x, mem, positions, mix_norm_gain, w_in, att_sinks, att_out_gain, ret_out_gain, w_out,
           cross_norm_gain, mem_norm_gain, w_xq, w_xk, w_xv, w_xo, ffn_norm_gain,
           w_peer_q, peer_sub_keys, peer_u, peer_v, final_norm_gain):
    batch, seq, d = x.shape
    mem_len = mem.shape[1]
    t = batch * seq
    depth = w_in.shape[0]
    assert d == D_MODEL and seq % BLOCK == 0 and t % GATHER_TOKENS == 0

    cos_a, sin_a = _rope_tables_att(positions)
    cos_r, sin_r = _rope_angles(positions, RET_HEAD_DIM)
    xf = x.reshape(t, d)
    memf = mem.reshape(batch * mem_len, d)
    tm = min(512, seq)
    for l in range(depth):
        h = _norm_matmul(xf, mix_norm_gain[l], w_in[l].astype(BF16), tm, 768, F32)
        oa = _swa(h, cos_a, sin_a, att_sinks[l], att_out_gain[l], batch, seq)
        orr = _retention(h, cos_r, sin_r, ret_out_gain[l], batch, seq)
        x1, qx = _outproj(xf, oa, orr, w_out[l].astype(BF16), cross_norm_gain[l], w_xq[l].astype(BF16),
                          min(256, seq))
        w_kv = jnp.concatenate([w_xk[l], w_xv[l]], axis=1).astype(BF16)
        kv_mem = _norm_matmul(memf, mem_norm_gain[l], w_kv, mem_len, 2 * XATT_WIDTH, BF16)
        x2, pq = _xattn(qx, kv_mem, x1, w_xo[l].astype(BF16), ffn_norm_gain[l], w_peer_q[l].astype(BF16),
                        batch, seq, mem_len, min(256, seq))
        ids_t, gates_t = _peer_topk(pq, peer_sub_keys[l].astype(BF16), 128)
        ids = ids_t.reshape(PEER_SLOTS, t).T.reshape(t * PEER_SLOTS)
        table = _pack_expert_table(peer_u[l], peer_v[l])
        xf = _peer_gather(ids, table, x2, gates_t.reshape(PEER_SLOTS, t), ffn_norm_gain[l], final_norm_gain,
                          final_norm=(l == depth - 1))
    return xf.reshape(batch, seq, d)
```

```python
import functools
import math

import jax
import jax.numpy as jnp
from jax import lax
from jax.experimental import pallas as pl
from jax.experimental.pallas import tpu as pltpu

F32 = jnp.float32
BF16 = jnp.bfloat16

EPS = 1e-6
GN_EPS = 1e-5
ROPE_THETA = 10000.0
D_MODEL = 2048
HEAD_DIM = 64
ATT_Q_HEADS = 16
ATT_KV_HEADS = 2
ATT_WIDTH = ATT_Q_HEADS * HEAD_DIM
ATT_KV_WIDTH = ATT_KV_HEADS * HEAD_DIM
BLOCK = 128
RET_HEADS = 4
RET_HEAD_DIM = 256
RET_WIDTH = RET_HEADS * RET_HEAD_DIM
RET_CHUNK = 128
IN_COLS = ATT_WIDTH + 2 * ATT_KV_WIDTH + 4 * RET_WIDTH
XATT_HEADS = 4
XATT_HEAD_DIM = 128
XATT_WIDTH = XATT_HEADS * XATT_HEAD_DIM
PEER_HEADS = 8
PEER_N_KEYS = 128
PEER_HALF_DIM = 128
PEER_TOPK = 16
PEER_SLOTS = PEER_HEADS * PEER_TOPK

LANES = 128
SUBLANES = 8
VMEM_LIMIT = 56 * 1024 * 1024

_KA0 = ATT_WIDTH
_VA0 = _KA0 + ATT_KV_WIDTH
_QR0 = _VA0 + ATT_KV_WIDTH
_KR0 = _QR0 + RET_WIDTH
_VR0 = _KR0 + RET_WIDTH
_GR0 = _VR0 + RET_WIDTH


def _params(*sem):
    return pltpu.CompilerParams(dimension_semantics=sem, vmem_limit_bytes=VMEM_LIMIT)


def _rms(xf, gain):
    ms = jnp.mean(xf * xf, axis=-1, keepdims=True)
    return xf * lax.rsqrt(ms + EPS) * gain


def _dot(a, b):
    return jnp.dot(a, b, preferred_element_type=F32)


def _dot_nt(a, b):
    return lax.dot_general(a, b, (((1,), (1,)), ((), ())), preferred_element_type=F32)


def _norm_matmul_kernel(x_ref, g_ref, w_ref, o_ref, xn_ref):
    @pl.when(pl.program_id(1) == 0)
    def _():
        xn_ref[...] = _rms(x_ref[...], g_ref[...]).astype(BF16)

    o_ref[...] = _dot(xn_ref[...], w_ref[...]).astype(o_ref.dtype)


def _norm_matmul(x, gain, w_bf16, tm, tn, out_dtype):
    t, d = x.shape
    n = w_bf16.shape[1]
    return pl.pallas_call(
        _norm_matmul_kernel,
        grid=(t // tm, n // tn),
        in_specs=[
            pl.BlockSpec((tm, d), lambda i, j: (i, 0)),
            pl.BlockSpec((1, d), lambda i, j: (0, 0)),
            pl.BlockSpec((d, tn), lambda i, j: (0, j)),
        ],
        out_specs=pl.BlockSpec((tm, tn), lambda i, j: (i, j)),
        out_shape=jax.ShapeDtypeStruct((t, n), out_dtype),
        scratch_shapes=[pltpu.VMEM((tm, d), BF16)],
        compiler_params=_params("parallel", "arbitrary"),
        name="norm_matmul",
    )(x, gain.reshape(1, d), w_bf16)


def _swa_kernel(sink_ref, q_ref, kc_ref, kp_ref, vc_ref, vp_ref, cc_ref, sc_ref, cp_ref, sp_ref,
                gain_ref, o_ref):
    n = pl.program_id(1)
    lane = lax.broadcasted_iota(jnp.int32, (1, LANES), 1)
    first_half = (lane % HEAD_DIM) < (HEAD_DIM // 2)
    lo = lane < HEAD_DIM

    def rope(x, c, s):
        partner = jnp.where(first_half, pltpu.roll(x, LANES - HEAD_DIM // 2, 1),
                            pltpu.roll(x, HEAD_DIM // 2, 1))
        return x * c + partner * s

    cc, sc = cc_ref[...], sc_ref[...]
    k = jnp.concatenate([rope(kp_ref[...], cp_ref[...], sp_ref[...]), rope(kc_ref[...], cc, sc)], axis=0)
    v = jnp.concatenate([vp_ref[...], vc_ref[...]], axis=0)
    k_sw = pltpu.roll(k, HEAD_DIM, 1)
    v_sw = pltpu.roll(v, HEAD_DIM, 1)

    def place(a, a_sw, c, half):
        src = a if c == half else a_sw
        keep = lo if half == 0 else jnp.logical_not(lo)
        return jnp.where(keep, src, 0.0).astype(BF16)

    kvar = {(c, h): place(k, k_sw, c, h) for c in range(ATT_KV_HEADS) for h in range(2)}
    vvar = {(c, h): place(v, v_sw, c, h) for c in range(ATT_KV_HEADS) for h in range(2)}

    qi = lax.broadcasted_iota(jnp.int32, (BLOCK, 2 * BLOCK), 0) + BLOCK
    ki = lax.broadcasted_iota(jnp.int32, (BLOCK, 2 * BLOCK), 1)
    dist = qi - ki
    kmin = jnp.where(n > 0, 0, BLOCK)
    valid = (dist >= 0) & (dist < BLOCK) & (ki >= kmin)

    group = ATT_Q_HEADS // ATT_KV_HEADS
    for j in range(ATT_WIDTH // LANES):
        c = (2 * j) // group
        cols = slice(j * LANES, (j + 1) * LANES)
        qg = (rope(q_ref[:, cols], cc, sc) * (HEAD_DIM ** -0.5)).astype(BF16)
        out = jnp.zeros((BLOCK, LANES), F32)
        for half in range(2):
            s = jnp.where(valid, _dot_nt(qg, kvar[(c, half)]), -jnp.inf)
            sink = sink_ref[2 * j + half]
            m = jnp.maximum(jnp.max(s, axis=-1, keepdims=True), sink)
            p = jnp.exp(s - m)
            denom = jnp.sum(p, axis=-1, keepdims=True) + jnp.exp(sink - m)
            out = out + _dot((p / denom).astype(BF16), vvar[(c, half)])
        sq = out * out
        ss_lo = jnp.sum(jnp.where(lo, sq, 0.0), axis=-1, keepdims=True)
        ss_hi = jnp.sum(jnp.where(lo, 0.0, sq), axis=-1, keepdims=True)
        ms = jnp.where(lo, ss_lo, ss_hi) * (1.0 / HEAD_DIM)
        o_ref[:, cols] = (out * lax.rsqrt(ms + EPS) * gain_ref[:, cols]).astype(o_ref.dtype)


def _swa(h, cos_a, sin_a, sinks, gain, batch, seq):
    nb = seq // BLOCK
    t = batch * seq
    kcol, vcol = _KA0 // LANES, _VA0 // LANES
    cur = lambda b, n: b * nb + n
    prev = lambda b, n: b * nb + jnp.maximum(n - 1, 0)
    row = lambda col, f: pl.BlockSpec((BLOCK, LANES), lambda b, n: (f(b, n), col))
    return pl.pallas_call(
        _swa_kernel,
        grid=(batch, nb),
        in_specs=[
            pl.BlockSpec(memory_space=pltpu.SMEM),
            pl.BlockSpec((BLOCK, ATT_WIDTH), lambda b, n: (cur(b, n), 0)),
            row(kcol, cur), row(kcol, prev), row(vcol, cur), row(vcol, prev),
            row(0, cur), row(0, cur), row(0, prev), row(0, prev),
            pl.BlockSpec((1, ATT_WIDTH), lambda b, n: (0, 0)),
        ],
        out_specs=pl.BlockSpec((BLOCK, ATT_WIDTH), lambda b, n: (cur(b, n), 0)),
        out_shape=jax.ShapeDtypeStruct((t, ATT_WIDTH), BF16),
        compiler_params=_params("parallel", "parallel"),
        name="swa_attention",
    )(sinks, h, h, h, h, h, cos_a, sin_a, cos_a, sin_a, gain.reshape(1, ATT_WIDTH))


def _ret_kernel(lg_ref, cd_ref, q_ref, k_ref, v_ref, g_ref, c_ref, s_ref, gain_ref, o_ref, state_ref):
    hh = pl.program_id(1)
    n = pl.program_id(2)

    @pl.when(n == 0)
    def _():
        state_ref[...] = jnp.zeros_like(state_ref)

    lg = lg_ref[hh]
    c, s = c_ref[...], s_ref[...]
    half = RET_HEAD_DIM // 2

    def rope(x):
        x1, x2 = x[:, :half], x[:, half:]
        return jnp.concatenate([x1 * c - x2 * s, x2 * c + x1 * s], axis=1)

    q = rope(q_ref[...])
    k = rope(k_ref[...]) * (RET_HEAD_DIM ** -0.5)
    v = v_ref[...].astype(BF16)

    ri = lax.broadcasted_iota(jnp.int32, (RET_CHUNK, RET_CHUNK), 0).astype(F32)
    ci = lax.broadcasted_iota(jnp.int32, (RET_CHUNK, RET_CHUNK), 1).astype(F32)
    diff = ri - ci
    decay = jnp.where(diff >= 0, jnp.exp(jnp.maximum(diff, 0.0) * lg), 0.0)
    zeta = jnp.exp((RET_CHUNK - 1 - ri) * lg)
    xi = jnp.exp((ri + 1.0) * lg)
    zeta2 = jnp.concatenate([zeta, zeta], axis=1)
    xi2 = jnp.concatenate([xi, xi], axis=1)

    inner = _dot_nt(q.astype(BF16), k.astype(BF16)) * decay
    state = state_ref[...]
    out = _dot(inner.astype(BF16), v) + _dot((q * xi2).astype(BF16), state.astype(BF16))
    kv = lax.dot_general((k * zeta2).astype(BF16), v, (((0,), (0,)), ((), ())), preferred_element_type=F32)
    state_ref[...] = state * cd_ref[hh] + kv

    mu = jnp.mean(out, axis=-1, keepdims=True)
    cen = out - mu
    var = jnp.mean(cen * cen, axis=-1, keepdims=True)
    g = g_ref[...]
    o = cen * lax.rsqrt(var + GN_EPS) * gain_ref[...] * (g * jax.nn.sigmoid(g))
    o_ref[...] = o.astype(o_ref.dtype)


def _retention(h, cos_r, sin_r, gain, batch, seq):
    nc = seq // RET_CHUNK
    t = batch * seq
    hd = RET_HEAD_DIM
    log_gamma = jnp.log1p(-jnp.exp2(-5.0 - jnp.arange(RET_HEADS, dtype=F32)))
    chunk_decay = jnp.exp(RET_CHUNK * log_gamma)
    col = lambda c0: pl.BlockSpec((RET_CHUNK, hd), lambda b, hh, n: (b * nc + n, c0 // hd + hh))
    tab = pl.BlockSpec((RET_CHUNK, hd // 2), lambda b, hh, n: (b * nc + n, 0))
    return pl.pallas_call(
        _ret_kernel,
        grid=(batch, RET_HEADS, nc),
        in_specs=[
            pl.BlockSpec(memory_space=pltpu.SMEM),
            pl.BlockSpec(memory_space=pltpu.SMEM),
            col(_QR0), col(_KR0), col(_VR0), col(_GR0), tab, tab,
            pl.BlockSpec((1, hd), lambda b, hh, n: (0, hh)),
        ],
        out_specs=pl.BlockSpec((RET_CHUNK, hd), lambda b, hh, n: (b * nc + n, hh)),
        out_shape=jax.ShapeDtypeStruct((t, RET_WIDTH), BF16),
        scratch_shapes=[pltpu.VMEM((hd, hd), F32)],
        compiler_params=_params("parallel", "parallel", "arbitrary"),
        name="retention",
    )(log_gamma, chunk_decay, h, h, h, h, cos_r, sin_r, gain.reshape(1, RET_WIDTH))


def _outproj_kernel(x_ref, oa_ref, or_ref, wa_ref, wr_ref, g_ref, wq_ref, x1_ref, qx_ref):
    x1 = x_ref[...] + _dot(oa_ref[...], wa_ref[...]) + _dot(or_ref[...], wr_ref[...])
    x1_ref[...] = x1
    qx_ref[...] = _dot(_rms(x1, g_ref[...]).astype(BF16), wq_ref[...]).astype(qx_ref.dtype)


def _outproj(x, oa, orr, w_out_bf16, gain, w_xq_bf16, tm):
    t, d = x.shape
    full = lambda shape: pl.BlockSpec(shape, lambda i: (0, 0))
    return pl.pallas_call(
        _outproj_kernel,
        grid=(t // tm,),
        in_specs=[
            pl.BlockSpec((tm, d), lambda i: (i, 0)),
            pl.BlockSpec((tm, ATT_WIDTH), lambda i: (i, 0)),
            pl.BlockSpec((tm, RET_WIDTH), lambda i: (i, 0)),
            pl.BlockSpec((ATT_WIDTH, d), lambda i: (0, 0)),
            pl.BlockSpec((RET_WIDTH, d), lambda i: (1, 0)),
            full((1, d)),
            full((d, XATT_WIDTH)),
        ],
        out_specs=[pl.BlockSpec((tm, d), lambda i: (i, 0)), pl.BlockSpec((tm, XATT_WIDTH), lambda i: (i, 0))],
        out_shape=[jax.ShapeDtypeStruct((t, d), F32), jax.ShapeDtypeStruct((t, XATT_WIDTH), BF16)],
        compiler_params=_params("parallel"),
        name="out_proj",
    )(x, oa, orr, w_out_bf16, w_out_bf16, gain.reshape(1, d), w_xq_bf16)


def _xattn_kernel(qx_ref, k_ref, v_ref, x1_ref, wo_ref, g_ref, wpq_ref, x2_ref, pq_ref):
    heads = []
    for hh in range(XATT_HEADS):
        cols = slice(hh * XATT_HEAD_DIM, (hh + 1) * XATT_HEAD_DIM)
        s = _dot_nt(qx_ref[:, cols], k_ref[:, cols]) * (XATT_HEAD_DIM ** -0.5)
        p = jnp.exp(s - jnp.max(s, axis=-1, keepdims=True))
        p = p / jnp.sum(p, axis=-1, keepdims=True)
        heads.append(_dot(p.astype(BF16), v_ref[:, cols]))
    o = jnp.concatenate(heads, axis=1).astype(BF16)
    x2 = x1_ref[...] + _dot(o, wo_ref[...])
    x2_ref[...] = x2
    pq_ref[...] = _dot(_rms(x2, g_ref[...]).astype(BF16), wpq_ref[...])


def _xattn(qx, kv_mem, x1, w_xo_bf16, gain, w_pq_bf16, batch, seq, mem_len, tm):
    t, d = x1.shape
    npq = w_pq_bf16.shape[1]
    nt = seq // tm
    rows = lambda width: pl.BlockSpec((tm, width), lambda b, i: (b * nt + i, 0))
    full = lambda shape: pl.BlockSpec(shape, lambda b, i: (0, 0))
    return pl.pallas_call(
        _xattn_kernel,
        grid=(batch, nt),
        in_specs=[
            rows(XATT_WIDTH),
            pl.BlockSpec((mem_len, XATT_WIDTH), lambda b, i: (b, 0)),
            pl.BlockSpec((mem_len, XATT_WIDTH), lambda b, i: (b, 1)),
            rows(d),
            full((XATT_WIDTH, d)),
            full((1, d)),
            full((d, npq)),
        ],
        out_specs=[rows(d), rows(npq)],
        out_shape=[jax.ShapeDtypeStruct((t, d), F32), jax.ShapeDtypeStruct((t, npq), F32)],
        compiler_params=_params("parallel", "parallel"),
        name="cross_attention",
    )(qx, kv_mem, kv_mem, x1, w_xo_bf16, gain.reshape(1, d), w_pq_bf16)


def _topk_rows(s, k, payload=None):
    rows = lax.broadcasted_iota(jnp.int32, s.shape, 0)
    vals, sel = [], []
    for _ in range(k):
        m = jnp.max(s, axis=0, keepdims=True)
        am = jnp.min(jnp.where(s == m, rows, s.shape[0]), axis=0, keepdims=True)
        hit = rows == am
        vals.append(m)
        sel.append(am if payload is None else jnp.max(jnp.where(hit, payload, -1), axis=0, keepdims=True))
        s = jnp.where(hit, -jnp.inf, s)
    return jnp.concatenate(vals, axis=0), jnp.concatenate(sel, axis=0)


def _peer_topk_kernel(q_ref, keys_ref, ids_ref, gates_ref):
    tops = []
    for p in range(2):
        qp = q_ref[:, p * PEER_HALF_DIM:(p + 1) * PEER_HALF_DIM].astype(BF16)
        tops.append(_topk_rows(_dot_nt(keys_ref[0, p], qp), PEER_TOPK))
    (s1, i1), (s2, i2) = tops
    cand_s = jnp.concatenate([s1[a:a + 1, :] + s2 for a in range(PEER_TOPK)], axis=0)
    cand_i = jnp.concatenate([i1[a:a + 1, :] * PEER_N_KEYS + i2 for a in range(PEER_TOPK)], axis=0)
    top_s, top_e = _topk_rows(cand_s, PEER_TOPK, payload=cand_i)
    e = jnp.exp(top_s - top_s[0:1, :])
    ids_ref[0] = top_e
    gates_ref[0] = e / jnp.sum(e, axis=0, keepdims=True)


def _peer_topk(pq, keys_bf16, tt):
    t = pq.shape[0]
    width = 2 * PEER_HALF_DIM
    out = pl.BlockSpec((1, PEER_TOPK, tt), lambda i, hh: (hh, 0, i))
    return pl.pallas_call(
        _peer_topk_kernel,
        grid=(t // tt, PEER_HEADS),
        in_specs=[
            pl.BlockSpec((tt, width), lambda i, hh: (i, hh)),
            pl.BlockSpec((1, 2, PEER_N_KEYS, PEER_HALF_DIM), lambda i, hh: (hh, 0, 0, 0)),
        ],
        out_specs=[out, out],
        out_shape=[jax.ShapeDtypeStruct((PEER_HEADS, PEER_TOPK, t), jnp.int32),
                   jax.ShapeDtypeStruct((PEER_HEADS, PEER_TOPK, t), F32)],
        compiler_params=_params("parallel", "parallel"),
        name="peer_topk",
    )(pq, keys_bf16)


GATHER_TOKENS = 128
GATHER_BUFS = 4


def _peer_gather_kernel(ids_hbm, tbl_hbm, x2_ref, gates_ref, fg_ref, og_ref, out_ref,
                        ids_smem, ids_sem, *scratch, final_norm):
    gbufs = scratch[:GATHER_BUFS]
    gsem, xn_ref, y_ref, yacc_ref = scratch[GATHER_BUFS:]
    i = pl.program_id(0)
    has_next = i + 1 < pl.num_programs(0)
    slot = i % 2
    other = 1 - slot
    tb = GATHER_TOKENS
    nchunk = D_MODEL // LANES
    nids = tb * PEER_SLOTS
    ahead = GATHER_BUFS - 1
    group = SUBLANES

    def ids_copy(step, sl):
        return pltpu.make_async_copy(ids_hbm.at[pl.ds(pl.multiple_of(step * nids, nids), nids)],
                                     ids_smem.at[pl.ds(pl.multiple_of(sl * nids, nids), nids)], ids_sem.at[sl])

    def row_copy(e, b, k):
        return pltpu.make_async_copy(tbl_hbm.at[e], gbufs[b].at[pl.ds(k, 1)], gsem.at[b])

    def issue(sl, tok, b):
        base = sl * nids + tok * PEER_SLOTS
        for k in range(PEER_SLOTS):
            row_copy(ids_smem[base + k], b, k).start()

    def wait_rows(b):
        for k in range(PEER_SLOTS):
            row_copy(0, b, k).wait()

    @pl.when(i == 0)
    def _():
        ids_copy(0, 0).start()
        ids_copy(0, 0).wait()
        for tok in range(ahead):
            issue(0, tok, tok)

    @pl.when(has_next)
    def _():
        ids_copy(i + 1, other).start()

    xn_ref[...] = _rms(x2_ref[...], fg_ref[...])

    lane = lax.broadcasted_iota(jnp.int32, (PEER_SLOTS, LANES), 1)
    hi_mask = jnp.uint32(0xFFFF0000)

    def compute(tok, r, b):
        gbuf = gbufs[b]
        xrow = xn_ref[pl.ds(tok, 1), :]
        acc = jnp.zeros((PEER_SLOTS, LANES), F32)
        for c in range(nchunk):
            w = gbuf[:, c * LANES:(c + 1) * LANES]
            u = lax.bitcast_convert_type(w << 16, F32)
            acc = acc + u * xrow[:, c * LANES:(c + 1) * LANES]
        a = jnp.sum(acc, axis=-1, keepdims=True)
        gate = jnp.sum(jnp.where(lane == tok, gates_ref[...], 0.0), axis=-1, keepdims=True)
        coeff = 0.5 * a * (1.0 + lax.erf(a * (2.0 ** -0.5))) * gate
        for c in range(nchunk):
            w = gbuf[:, c * LANES:(c + 1) * LANES]
            vv = lax.bitcast_convert_type(w & hi_mask, F32)
            yacc_ref[r:r + 1, c * LANES:(c + 1) * LANES] = jnp.sum(vv * coeff, axis=0, keepdims=True)

    def run_group(g, last):
        for r in range(group):
            b = r % GATHER_BUFS
            nb = (r + ahead) % GATHER_BUFS
            tok = g * group + r
            wait_rows(b)
            if not last or r + ahead < group:
                issue(slot, tok + ahead, nb)
            else:
                if r + ahead == group:
                    @pl.when(has_next)
                    def _():
                        ids_copy(i + 1, other).wait()

                @pl.when(has_next)
                def _():
                    issue(other, r + ahead - group, nb)
            compute(tok, r, b)
        y_ref[pl.ds(pl.multiple_of(g * group, group), group), :] = yacc_ref[...]

    def body(g, carry):
        run_group(g, last=False)
        return carry

    ngroups = tb // group
    lax.fori_loop(0, ngroups - 1, body, 0)
    run_group(ngroups - 1, last=True)
    x3 = x2_ref[...] + y_ref[...]
    out_ref[...] = _rms(x3, og_ref[...]) if final_norm else x3


def _peer_gather(ids_flat, table, x2, gates_t, ffn_gain, final_gain, final_norm):
    t, d = x2.shape
    tb = GATHER_TOKENS
    nsteps = t // tb
    full = lambda shape: pl.BlockSpec(shape, lambda i: (0, 0))
    return pl.pallas_call(
        functools.partial(_peer_gather_kernel, final_norm=final_norm),
        grid=(nsteps,),
        in_specs=[
            pl.BlockSpec(memory_space=pl.ANY),
            pl.BlockSpec(memory_space=pl.ANY),
            pl.BlockSpec((tb, d), lambda i: (i, 0)),
            pl.BlockSpec((PEER_SLOTS, tb), lambda i: (0, i)),
            full((1, d)),
            full((1, d)),
        ],
        out_specs=pl.BlockSpec((tb, d), lambda i: (i, 0)),
        out_shape=jax.ShapeDtypeStruct((t, d), F32),
        scratch_shapes=[
            pltpu.SMEM((2 * tb * PEER_SLOTS,), jnp.int32),
            pltpu.SemaphoreType.DMA((2,)),
            *[pltpu.VMEM((PEER_SLOTS, d), jnp.uint32) for _ in range(GATHER_BUFS)],
            pltpu.SemaphoreType.DMA((GATHER_BUFS,)),
            pltpu.VMEM((tb, d), F32),
            pltpu.VMEM((tb, d), F32),
            pltpu.VMEM((SUBLANES, d), F32),
        ],
        compiler_params=_params("arbitrary"),
        name="peer_gather",
    )(ids_flat, table, x2, gates_t, ffn_gain.reshape(1, d), final_gain.reshape(1, d))


def _rope_angles(positions, dh):
    inv_freq = ROPE_THETA ** (-jnp.arange(0, dh, 2, dtype=F32) / dh)
    ang = positions.astype(F32).reshape(-1, 1) * inv_freq
    return jnp.cos(ang), jnp.sin(ang)


def _rope_tables_att(positions):
    cos, sin = _rope_angles(positions, HEAD_DIM)
    copies = LANES // HEAD_DIM
    return (jnp.tile(jnp.concatenate([cos, cos], axis=1), (1, copies)),
            jnp.tile(jnp.concatenate([-sin, sin], axis=1), (1, copies)))


def _pack_expert_table(u, v):
    ub = lax.bitcast_convert_type(u.astype(BF16), jnp.uint16).astype(jnp.uint32)
    vb = lax.bitcast_convert_type(v.astype(BF16), jnp.uint16).astype(jnp.uint32)
    return ((vb << 16) | ub).reshape(u.shape[0], 1, u.shape[1])


def kernel(x, mem, positions, mix_norm_gain, w_in, att_sinks, att_out_gain, ret_out_gain, w_out,
           cross_norm_gain, mem_norm_gain, w_xq, w_xk, w_xv, w_xo, ffn_norm_gain,
           w_peer_q, peer_sub_keys, peer_u, peer_v, final_norm_gain):
    batch, seq, d = x.shape
    mem_len = mem.shape[1]
    t = batch * seq
    depth = w_in.shape[0]
    assert d == D_MODEL and seq % BLOCK == 0 and t % GATHER_TOKENS == 0

    cos_a, sin_a = _rope_tables_att(positions)
    cos_r, sin_r = _rope_angles(positions, RET_HEAD_DIM)
    xf = x.reshape(t, d)
    memf = mem.reshape(batch * mem_len, d)
    tm = min(512, seq)
    for l in range(depth):
        h = _norm_matmul(xf, mix_norm_gain[l], w_in[l].astype(BF16), tm, 768, F32)
        oa = _swa(h, cos_a, sin_a, att_sinks[l], att_out_gain[l], batch, seq)
        orr = _retention(h, cos_r, sin_r, ret_out_gain[l], batch, seq)
        x1, qx = _outproj(xf, oa, orr, w_out[l].astype(BF16), cross_norm_gain[l], w_xq[l].astype(BF16),
                          min(256, seq))
        w_kv = jnp.concatenate([w_xk[l], w_xv[l]], axis=1).astype(BF16)
        kv_mem = _norm_matmul(memf, mem_norm_gain[l], w_kv, mem_len, 2 * XATT_WIDTH, BF16)
        x2, pq = _xattn(qx, kv_mem, x1, w_xo[l].astype(BF16), ffn_norm_gain[l], w_peer_q[l].astype(BF16),
                        batch, seq, mem_len, min(256, seq))
        ids_t, gates_t = _peer_topk(pq, peer_sub_keys[l].astype(BF16), 128)
        ids = ids_t.reshape(PEER_SLOTS, t).T.reshape(t * PEER_SLOTS)
        table = _pack_expert_table(peer_u[l], peer_v[l])
        xf = _peer_gather(ids, table, x2, gates_t.reshape(PEER_SLOTS, t), ffn_norm_gain[l], final_norm_gain,
                          final_norm=(l == depth - 1))
    return xf.reshape(batch, seq, d)
```

```python
import functools
import math

import jax
import jax.numpy as jnp
from jax import lax
from jax.experimental import pallas as pl
from jax.experimental.pallas import tpu as pltpu

F32 = jnp.float32
BF16 = jnp.bfloat16

EPS = 1e-6
GN_EPS = 1e-5
ROPE_THETA = 10000.0
D_MODEL = 2048
HEAD_DIM = 64
ATT_Q_HEADS = 16
ATT_KV_HEADS = 2
ATT_WIDTH = ATT_Q_HEADS * HEAD_DIM
ATT_KV_WIDTH = ATT_KV_HEADS * HEAD_DIM
BLOCK = 128
RET_HEADS = 4
RET_HEAD_DIM = 256
RET_WIDTH = RET_HEADS * RET_HEAD_DIM
RET_CHUNK = 128
IN_COLS = ATT_WIDTH + 2 * ATT_KV_WIDTH + 4 * RET_WIDTH
XATT_HEADS = 4
XATT_HEAD_DIM = 128
XATT_WIDTH = XATT_HEADS * XATT_HEAD_DIM
PEER_HEADS = 8
PEER_N_KEYS = 128
PEER_HALF_DIM = 128
PEER_TOPK = 16
PEER_SLOTS = PEER_HEADS * PEER_TOPK

LANES = 128
SUBLANES = 8
VMEM_LIMIT = 56 * 1024 * 1024

_KA0 = ATT_WIDTH
_VA0 = _KA0 + ATT_KV_WIDTH
_QR0 = _VA0 + ATT_KV_WIDTH
_KR0 = _QR0 + RET_WIDTH
_VR0 = _KR0 + RET_WIDTH
_GR0 = _VR0 + RET_WIDTH


def _params(*sem):
    return pltpu.CompilerParams(dimension_semantics=sem, vmem_limit_bytes=VMEM_LIMIT)


def _rms(xf, gain):
    ms = jnp.mean(xf * xf, axis=-1, keepdims=True)
    return xf * lax.rsqrt(ms + EPS) * gain


def _dot(a, b):
    return jnp.dot(a, b, preferred_element_type=F32)


def _dot_nt(a, b):
    return lax.dot_general(a, b, (((1,), (1,)), ((), ())), preferred_element_type=F32)


def _norm_matmul_kernel(x_ref, g_ref, w_ref, o_ref, xn_ref):
    @pl.when(pl.program_id(1) == 0)
    def _():
        xn_ref[...] = _rms(x_ref[...], g_ref[...]).astype(BF16)

    o_ref[...] = _dot(xn_ref[...], w_ref[...]).astype(o_ref.dtype)


def _norm_matmul(x, gain, w_bf16, tm, tn, out_dtype):
    t, d = x.shape
    n = w_bf16.shape[1]
    return pl.pallas_call(
        _norm_matmul_kernel,
        grid=(t // tm, n // tn),
        in_specs=[
            pl.BlockSpec((tm, d), lambda i, j: (i, 0)),
            pl.BlockSpec((1, d), lambda i, j: (0, 0)),
            pl.BlockSpec((d, tn), lambda i, j: (0, j)),
        ],
        out_specs=pl.BlockSpec((tm, tn), lambda i, j: (i, j)),
        out_shape=jax.ShapeDtypeStruct((t, n), out_dtype),
        scratch_shapes=[pltpu.VMEM((tm, d), BF16)],
        compiler_params=_params("parallel", "arbitrary"),
        name="norm_matmul",
    )(x, gain.reshape(1, d), w_bf16)


def _swa_kernel(sink_ref, q_ref, kc_ref, kp_ref, vc_ref, vp_ref, cc_ref, sc_ref, cp_ref, sp_ref,
                gain_ref, o_ref):
    n = pl.program_id(1)
    lane = lax.broadcasted_iota(jnp.int32, (1, LANES), 1)
    first_half = (lane % HEAD_DIM) < (HEAD_DIM // 2)
    lo = lane < HEAD_DIM

    def rope(x, c, s):
        partner = jnp.where(first_half, pltpu.roll(x, LANES - HEAD_DIM // 2, 1),
                            pltpu.roll(x, HEAD_DIM // 2, 1))
        return x * c + partner * s

    cc, sc = cc_ref[...], sc_ref[...]
    k = jnp.concatenate([rope(kp_ref[...], cp_ref[...], sp_ref[...]), rope(kc_ref[...], cc, sc)], axis=0)
    v = jnp.concatenate([vp_ref[...], vc_ref[...]], axis=0)
    k_sw = pltpu.roll(k, HEAD_DIM, 1)
    v_sw = pltpu.roll(v, HEAD_DIM, 1)

    def place(a, a_sw, c, half):
        src = a if c == half else a_sw
        keep = lo if half == 0 else jnp.logical_not(lo)
        return jnp.where(keep, src, 0.0).astype(BF16)

    kvar = {(c, h): place(k, k_sw, c, h) for c in range(ATT_KV_HEADS) for h in range(2)}
    vvar = {(c, h): place(v, v_sw, c, h) for c in range(ATT_KV_HEADS) for h in range(2)}

    qi = lax.broadcasted_iota(jnp.int32, (BLOCK, 2 * BLOCK), 0) + BLOCK
    ki = lax.broadcasted_iota(jnp.int32, (BLOCK, 2 * BLOCK), 1)
    dist = qi - ki
    kmin = jnp.where(n > 0, 0, BLOCK)
    valid = (dist >= 0) & (dist < BLOCK) & (ki >= kmin)

    group = ATT_Q_HEADS // ATT_KV_HEADS
    for j in range(ATT_WIDTH // LANES):
        c = (2 * j) // group
        cols = slice(j * LANES, (j + 1) * LANES)
        qg = (rope(q_ref[:, cols], cc, sc) * (HEAD_DIM ** -0.5)).astype(BF16)
        out = jnp.zeros((BLOCK, LANES), F32)
        for half in range(2):
            s = jnp.where(valid, _dot_nt(qg, kvar[(c, half)]), -jnp.inf)
            sink = sink_ref[2 * j + half]
            m = jnp.maximum(jnp.max(s, axis=-1, keepdims=True), sink)
            p = jnp.exp(s - m)
            denom = jnp.sum(p, axis=-1, keepdims=True) + jnp.exp(sink - m)
            out = out + _dot((p / denom).astype(BF16), vvar[(c, half)])
        sq = out * out
        ss_lo = jnp.sum(jnp.where(lo, sq, 0.0), axis=-1, keepdims=True)
        ss_hi = jnp.sum(jnp.where(lo, 0.0, sq), axis=-1, keepdims=True)
        ms = jnp.where(lo, ss_lo, ss_hi) * (1.0 / HEAD_DIM)
        o_ref[:, cols] = (out * lax.rsqrt(ms + EPS) * gain_ref[:, cols]).astype(o_ref.dtype)


def _swa(h, cos_a, sin_a, sinks, gain, batch, seq):
    nb = seq // BLOCK
    t = batch * seq
    kcol, vcol = _KA0 // LANES, _VA0 // LANES
    cur = lambda b, n: b * nb + n
    prev = lambda b, n: b * nb + jnp.maximum(n - 1, 0)
    row = lambda col, f: pl.BlockSpec((BLOCK, LANES), lambda b, n: (f(b, n), col))
    return pl.pallas_call(
        _swa_kernel,
        grid=(batch, nb),
        in_specs=[
            pl.BlockSpec(memory_space=pltpu.SMEM),
            pl.BlockSpec((BLOCK, ATT_WIDTH), lambda b, n: (cur(b, n), 0)),
            row(kcol, cur), row(kcol, prev), row(vcol, cur), row(vcol, prev),
            row(0, cur), row(0, cur), row(0, prev), row(0, prev),
            pl.BlockSpec((1, ATT_WIDTH), lambda b, n: (0, 0)),
        ],
        out_specs=pl.BlockSpec((BLOCK, ATT_WIDTH), lambda b, n: (cur(b, n), 0)),
        out_shape=jax.ShapeDtypeStruct((t, ATT_WIDTH), BF16),
        compiler_params=_params("parallel", "parallel"),
        name="swa_attention",
    )(sinks, h, h, h, h, h, cos_a, sin_a, cos_a, sin_a, gain.reshape(1, ATT_WIDTH))


def _ret_kernel(lg_ref, cd_ref, q_ref, k_ref, v_ref, g_ref, c_ref, s_ref, gain_ref, o_ref, state_ref):
    hh = pl.program_id(1)
    n = pl.program_id(2)

    @pl.when(n == 0)
    def _():
        state_ref[...] = jnp.zeros_like(state_ref)

    lg = lg_ref[hh]
    c, s = c_ref[...], s_ref[...]
    half = RET_HEAD_DIM // 2

    def rope(x):
        x1, x2 = x[:, :half], x[:, half:]
        return jnp.concatenate([x1 * c - x2 * s, x2 * c + x1 * s], axis=1)

    q = rope(q_ref[...])
    k = rope(k_ref[...]) * (RET_HEAD_DIM ** -0.5)
    v = v_ref[...].astype(BF16)

    ri = lax.broadcasted_iota(jnp.int32, (RET_CHUNK, RET_CHUNK), 0).astype(F32)
    ci = lax.broadcasted_iota(jnp.int32, (RET_CHUNK, RET_CHUNK), 1).astype(F32)
    diff = ri - ci
    decay = jnp.where(diff >= 0, jnp.exp(jnp.maximum(diff, 0.0) * lg), 0.0)
    zeta = jnp.exp((RET_CHUNK - 1 - ri) * lg)
    xi = jnp.exp((ri + 1.0) * lg)
    zeta2 = jnp.concatenate([zeta, zeta], axis=1)
    xi2 = jnp.concatenate([xi, xi], axis=1)

    inner = _dot_nt(q.astype(BF16), k.astype(BF16)) * decay
    state = state_ref[...]
    out = _dot(inner.astype(BF16), v) + _dot((q * xi2).astype(BF16), state.astype(BF16))
    kv = lax.dot_general((k * zeta2).astype(BF16), v, (((0,), (0,)), ((), ())), preferred_element_type=F32)
    state_ref[...] = state * cd_ref[hh] + kv

    mu = jnp.mean(out, axis=-1, keepdims=True)
    cen = out - mu
    var = jnp.mean(cen * cen, axis=-1, keepdims=True)
    g = g_ref[...]
    o = cen * lax.rsqrt(var + GN_EPS) * gain_ref[...] * (g * jax.nn.sigmoid(g))
    o_ref[...] = o.astype(o_ref.dtype)


def _retention(h, cos_r, sin_r, gain, batch, seq):
    nc = seq // RET_CHUNK
    t = batch * seq
    hd = RET_HEAD_DIM
    log_gamma = jnp.log1p(-jnp.exp2(-5.0 - jnp.arange(RET_HEADS, dtype=F32)))
    chunk_decay = jnp.exp(RET_CHUNK * log_gamma)
    col = lambda c0: pl.BlockSpec((RET_CHUNK, hd), lambda b, hh, n: (b * nc + n, c0 // hd + hh))
    tab = pl.BlockSpec((RET_CHUNK, hd // 2), lambda b, hh, n: (b * nc + n, 0))
    return pl.pallas_call(
        _ret_kernel,
        grid=(batch, RET_HEADS, nc),
        in_specs=[
            pl.BlockSpec(memory_space=pltpu.SMEM),
            pl.BlockSpec(memory_space=pltpu.SMEM),
            col(_QR0), col(_KR0), col(_VR0), col(_GR0), tab, tab,
            pl.BlockSpec((1, hd), lambda b, hh, n: (0, hh)),
        ],
        out_specs=pl.BlockSpec((RET_CHUNK, hd), lambda b, hh, n: (b * nc + n, hh)),
        out_shape=jax.ShapeDtypeStruct((t, RET_WIDTH), BF16),
        scratch_shapes=[pltpu.VMEM((hd, hd), F32)],
        compiler_params=_params("parallel", "parallel", "arbitrary"),
        name="retention",
    )(log_gamma, chunk_decay, h, h, h, h, cos_r, sin_r, gain.reshape(1, RET_WIDTH))


def _outproj_kernel(x_ref, oa_ref, or_ref, wa_ref, wr_ref, g_ref, wq_ref, x1_ref, qx_ref):
    x1 = x_ref[...] + _dot(oa_ref[...], wa_ref[...]) + _dot(or_ref[...], wr_ref[...])
    x1_ref[...] = x1
    qx_ref[...] = _dot(_rms(x1, g_ref[...]).astype(BF16), wq_ref[...]).astype(qx_ref.dtype)


def _outproj(x, oa, orr, w_out_bf16, gain, w_xq_bf16, tm):
    t, d = x.shape
    full = lambda shape: pl.BlockSpec(shape, lambda i: (0, 0))
    return pl.pallas_call(
        _outproj_kernel,
        grid=(t // tm,),
        in_specs=[
            pl.BlockSpec((tm, d), lambda i: (i, 0)),
            pl.BlockSpec((tm, ATT_WIDTH), lambda i: (i, 0)),
            pl.BlockSpec((tm, RET_WIDTH), lambda i: (i, 0)),
            pl.BlockSpec((ATT_WIDTH, d), lambda i: (0, 0)),
            pl.BlockSpec((RET_WIDTH, d), lambda i: (1, 0)),
            full((1, d)),
            full((d, XATT_WIDTH)),
        ],
        out_specs=[pl.BlockSpec((tm, d), lambda i: (i, 0)), pl.BlockSpec((tm, XATT_WIDTH), lambda i: (i, 0))],
        out_shape=[jax.ShapeDtypeStruct((t, d), F32), jax.ShapeDtypeStruct((t, XATT_WIDTH), BF16)],
        compiler_params=_params("parallel"),
        name="out_proj",
    )(x, oa, orr, w_out_bf16, w_out_bf16, gain.reshape(1, d), w_xq_bf16)


def _xattn_kernel(qx_ref, k_ref, v_ref, x1_ref, wo_ref, g_ref, wpq_ref, x2_ref, pq_ref):
    heads = []
    for hh in range(XATT_HEADS):
        cols = slice(hh * XATT_HEAD_DIM, (hh + 1) * XATT_HEAD_DIM)
        s = _dot_nt(qx_ref[:, cols], k_ref[:, cols]) * (XATT_HEAD_DIM ** -0.5)
        p = jnp.exp(s - jnp.max(s, axis=-1, keepdims=True))
        p = p / jnp.sum(p, axis=-1, keepdims=True)
        heads.append(_dot(p.astype(BF16), v_ref[:, cols]))
    o = jnp.concatenate(heads, axis=1).astype(BF16)
    x2 = x1_ref[...] + _dot(o, wo_ref[...])
    x2_ref[...] = x2
    pq_ref[...] = _dot(_rms(x2, g_ref[...]).astype(BF16), wpq_ref[...])


def _xattn(qx, kv_mem, x1, w_xo_bf16, gain, w_pq_bf16, batch, seq, mem_len, tm):
    t, d = x1.shape
    npq = w_pq_bf16.shape[1]
    nt = seq // tm
    rows = lambda width: pl.BlockSpec((tm, width), lambda b, i: (b * nt + i, 0))
    full = lambda shape: pl.BlockSpec(shape, lambda b, i: (0, 0))
    return pl.pallas_call(
        _xattn_kernel,
        grid=(batch, nt),
        in_specs=[
            rows(XATT_WIDTH),
            pl.BlockSpec((mem_len, XATT_WIDTH), lambda b, i: (b, 0)),
            pl.BlockSpec((mem_len, XATT_WIDTH), lambda b, i: (b, 1)),
            rows(d),
            full((XATT_WIDTH, d)),
            full((1, d)),
            full((d, npq)),
        ],
        out_specs=[rows(d), rows(npq)],
        out_shape=[jax.ShapeDtypeStruct((t, d), F32), jax.ShapeDtypeStruct((t, npq), F32)],
        compiler_params=_params("parallel", "parallel"),
        name="cross_attention",
    )(qx, kv_mem, kv_mem, x1, w_xo_bf16, gain.reshape(1, d), w_pq_bf16)


def _topk_rows(s, k, payload=None):
    nrows = s.shape[0]
    rows = lax.broadcasted_iota(jnp.int32, s.shape, 0).astype(F32)
    vals, sel = [], []
    for _ in range(k):
        m = jnp.max(s, axis=0, keepdims=True)
        am = jnp.min(jnp.where(s == m, rows, float(nrows)), axis=0, keepdims=True)
        hit = rows == am
        vals.append(m)
        sel.append(am if payload is None else jnp.max(jnp.where(hit, payload, -1.0), axis=0, keepdims=True))
        s = jnp.where(hit, -jnp.inf, s)
    return jnp.concatenate(vals, axis=0), jnp.concatenate(sel, axis=0)


def _staircase(t1, t2):
    kk, half = PEER_TOPK, SUBLANES
    blocks = [(t1[0:1, :], t2, kk), (t1[1:2, :], t2[0:half, :], half)]
    blocks += [(t1[a:a + 1, :], t2[0:half, :], kk // (a + 1)) for a in range(2, half)]
    blocks.append((t1[half:kk, :], t2[0:1, :], half))
    return blocks


def _peer_topk_kernel(q_ref, keys_ref, ids_ref, gates_ref, *, heads):
    width = 2 * PEER_HALF_DIM
    row8 = lax.broadcasted_iota(jnp.int32, (SUBLANES, q_ref.shape[0]), 0)
    for hh in range(heads):
        tops = []
        for p in range(2):
            c0 = hh * width + p * PEER_HALF_DIM
            qp = q_ref[:, c0:c0 + PEER_HALF_DIM].astype(BF16)
            tops.append(_topk_rows(_dot_nt(keys_ref[hh, p], qp), PEER_TOPK))
        (s1, i1), (s2, i2) = tops
        cand_s, cand_i = [], []
        for (a_s, b_s, live), (a_i, b_i, _) in zip(_staircase(s1, s2), _staircase(i1 * float(PEER_N_KEYS), i2)):
            blk = a_s + b_s
            if live < blk.shape[0]:
                blk = jnp.where(row8 < live, blk, -jnp.inf)
            cand_s.append(blk)
            cand_i.append(a_i + b_i)
        top_s, top_e = _topk_rows(jnp.concatenate(cand_s, axis=0), PEER_TOPK,
                                  payload=jnp.concatenate(cand_i, axis=0))
        e = jnp.exp(top_s - top_s[0:1, :])
        ids_ref[hh] = top_e.astype(jnp.int32)
        gates_ref[hh] = e / jnp.sum(e, axis=0, keepdims=True)


def _peer_topk(pq, keys_bf16, tt, heads):
    t = pq.shape[0]
    width = 2 * PEER_HALF_DIM
    out = pl.BlockSpec((heads, PEER_TOPK, tt), lambda i, hh: (hh, 0, i))
    return pl.pallas_call(
        functools.partial(_peer_topk_kernel, heads=heads),
        grid=(t // tt, PEER_HEADS // heads),
        in_specs=[
            pl.BlockSpec((tt, heads * width), lambda i, hh: (i, hh)),
            pl.BlockSpec((heads, 2, PEER_N_KEYS, PEER_HALF_DIM), lambda i, hh: (hh, 0, 0, 0)),
        ],
        out_specs=[out, out],
        out_shape=[jax.ShapeDtypeStruct((PEER_HEADS, PEER_TOPK, t), jnp.int32),
                   jax.ShapeDtypeStruct((PEER_HEADS, PEER_TOPK, t), F32)],
        compiler_params=_params("parallel", "parallel"),
        name="peer_topk",
    )(pq, keys_bf16)


GATHER_TOKENS = 128
GATHER_BUFS = 4


def _peer_gather_kernel(ids_hbm, tbl_hbm, x2_ref, gates_ref, fg_ref, og_ref, out_ref,
                        ids_smem, ids_sem, *scratch, final_norm):
    gbufs = scratch[:GATHER_BUFS]
    gsem, xn_ref, y_ref, yacc_ref = scratch[GATHER_BUFS:]
    i = pl.program_id(0)
    has_next = i + 1 < pl.num_programs(0)
    slot = i % 2
    other = 1 - slot
    tb = GATHER_TOKENS
    nchunk = D_MODEL // LANES
    nids = tb * PEER_SLOTS
    ahead = GATHER_BUFS - 1
    group = SUBLANES

    def ids_copy(step, sl):
        return pltpu.make_async_copy(ids_hbm.at[pl.ds(pl.multiple_of(step * nids, nids), nids)],
                                     ids_smem.at[pl.ds(pl.multiple_of(sl * nids, nids), nids)], ids_sem.at[sl])

    def row_copy(e, b, k):
        return pltpu.make_async_copy(tbl_hbm.at[e], gbufs[b].at[pl.ds(k, 1)], gsem.at[b])

    def issue(sl, tok, b):
        base = sl * nids + tok * PEER_SLOTS
        for k in range(PEER_SLOTS):
            row_copy(ids_smem[base + k], b, k).start(priority=k % 2)

    def wait_rows(b):
        for k in range(PEER_SLOTS):
            row_copy(0, b, k).wait()

    @pl.when(i == 0)
    def _():
        ids_copy(0, 0).start()
        ids_copy(0, 0).wait()
        for tok in range(ahead):
            issue(0, tok, tok)

    @pl.when(has_next)
    def _():
        ids_copy(i + 1, other).start()

    xn_ref[...] = _rms(x2_ref[...], fg_ref[...])

    lane = lax.broadcasted_iota(jnp.int32, (PEER_SLOTS, LANES), 1)
    hi_mask = jnp.uint32(0xFFFF0000)

    def compute(tok, r, b):
        gbuf = gbufs[b]
        xrow = xn_ref[pl.ds(tok, 1), :]
        acc = jnp.zeros((PEER_SLOTS, LANES), F32)
        for c in range(nchunk):
            w = gbuf[:, c * LANES:(c + 1) * LANES]
            u = lax.bitcast_convert_type(w << 16, F32)
            acc = acc + u * xrow[:, c * LANES:(c + 1) * LANES]
        a = jnp.sum(acc, axis=-1, keepdims=True)
        gate = jnp.sum(jnp.where(lane == tok, gates_ref[...], 0.0), axis=-1, keepdims=True)
        coeff = 0.5 * a * (1.0 + lax.erf(a * (2.0 ** -0.5))) * gate
        for c in range(nchunk):
            w = gbuf[:, c * LANES:(c + 1) * LANES]
            vv = lax.bitcast_convert_type(w & hi_mask, F32)
            yacc_ref[r:r + 1, c * LANES:(c + 1) * LANES] = jnp.sum(vv * coeff, axis=0, keepdims=True)

    def run_group(g, last):
        for r in range(group):
            b = r % GATHER_BUFS
            nb = (r + ahead) % GATHER_BUFS
            tok = g * group + r
            wait_rows(b)
            if not last or r + ahead < group:
                issue(slot, tok + ahead, nb)
            else:
                if r + ahead == group:
                    @pl.when(has_next)
                    def _():
                        ids_copy(i + 1, other).wait()

                @pl.when(has_next)
                def _():
                    issue(other, r + ahead - group, nb)
            compute(tok, r, b)
        y_ref[pl.ds(pl.multiple_of(g * group, group), group), :] = yacc_ref[...]

    def body(g, carry):
        run_group(g, last=False)
        return carry

    ngroups = tb // group
    lax.fori_loop(0, ngroups - 1, body, 0)
    run_group(ngroups - 1, last=True)
    x3 = x2_ref[...] + y_ref[...]
    out_ref[...] = _rms(x3, og_ref[...]) if final_norm else x3


def _peer_gather(ids_flat, table, x2, gates_t, ffn_gain, final_gain, final_norm):
    t, d = x2.shape
    tb = GATHER_TOKENS
    nsteps = t // tb
    full = lambda shape: pl.BlockSpec(shape, lambda i: (0, 0))
    return pl.pallas_call(
        functools.partial(_peer_gather_kernel, final_norm=final_norm),
        grid=(nsteps,),
        in_specs=[
            pl.BlockSpec(memory_space=pl.ANY),
            pl.BlockSpec(memory_space=pl.ANY),
            pl.BlockSpec((tb, d), lambda i: (i, 0)),
            pl.BlockSpec((PEER_SLOTS, tb), lambda i: (0, i)),
            full((1, d)),
            full((1, d)),
        ],
        out_specs=pl.BlockSpec((tb, d), lambda i: (i, 0)),
        out_shape=jax.ShapeDtypeStruct((t, d), F32),
        scratch_shapes=[
            pltpu.SMEM((2 * tb * PEER_SLOTS,), jnp.int32),
            pltpu.SemaphoreType.DMA((2,)),
            *[pltpu.VMEM((PEER_SLOTS, d), jnp.uint32) for _ in range(GATHER_BUFS)],
            pltpu.SemaphoreType.DMA((GATHER_BUFS,)),
            pltpu.VMEM((tb, d), F32),
            pltpu.VMEM((tb, d), F32),
            pltpu.VMEM((SUBLANES, d), F32),
        ],
        compiler_params=_params("arbitrary"),
        name="peer_gather",
    )(ids_flat, table, x2, gates_t, ffn_gain.reshape(1, d), final_gain.reshape(1, d))


def _rope_angles(positions, dh):
    inv_freq = ROPE_THETA ** (-jnp.arange(0, dh, 2, dtype=F32) / dh)
    ang = positions.astype(F32).reshape(-1, 1) * inv_freq
    return jnp.cos(ang), jnp.sin(ang)


def _rope_tables_att(positions):
    cos, sin = _rope_angles(positions, HEAD_DIM)
    copies = LANES // HEAD_DIM
    return (jnp.tile(jnp.concatenate([cos, cos], axis=1), (1, copies)),
            jnp.tile(jnp.concatenate([-sin, sin], axis=1), (1, copies)))


def _pack_expert_table(u, v):
    ub = lax.bitcast_convert_type(u.astype(BF16), jnp.uint16).astype(jnp.uint32)
    vb = lax.bitcast_convert_type(v.astype(BF16), jnp.uint16).astype(jnp.uint32)
    return ((vb << 16) | ub).reshape(u.shape[0], 1, u.shape[1])


def kernel(x, mem, positions, mix_norm_gain, w_in, att_sinks, att_out_gain, ret_out_gain, w_out,
           cross_norm_gain, mem_norm_gain, w_xq, w_xk, w_xv, w_xo, ffn_norm_gain,
           w_peer_q, peer_sub_keys, peer_u, peer_v, final_norm_gain):
    batch, seq, d = x.shape
    mem_len = mem.shape[1]
    t = batch * seq
    depth = w_in.shape[0]
    assert d == D_MODEL and seq % BLOCK == 0 and t % GATHER_TOKENS == 0

    cos_a, sin_a = _rope_tables_att(positions)
    cos_r, sin_r = _rope_angles(positions, RET_HEAD_DIM)
    xf = x.reshape(t, d)
    memf = mem.reshape(batch * mem_len, d)
    tm = min(512, seq)
    for l in range(depth):
        h = _norm_matmul(xf, mix_norm_gain[l], w_in[l].astype(BF16), tm, 768, F32)
        oa = _swa(h, cos_a, sin_a, att_sinks[l], att_out_gain[l], batch, seq)
        orr = _retention(h, cos_r, sin_r, ret_out_gain[l], batch, seq)
        x1, qx = _outproj(xf, oa, orr, w_out[l].astype(BF16), cross_norm_gain[l], w_xq[l].astype(BF16),
                          min(256, seq))
        w_kv = jnp.concatenate([w_xk[l], w_xv[l]], axis=1).astype(BF16)
        kv_mem = _norm_matmul(memf, mem_norm_gain[l], w_kv, mem_len, 2 * XATT_WIDTH, BF16)
        x2, pq = _xattn(qx, kv_mem, x1, w_xo[l].astype(BF16), ffn_norm_gain[l], w_peer_q[l].astype(BF16),
                        batch, seq, mem_len, min(256, seq))
        ids_t, gates_t = _peer_topk(pq, peer_sub_keys[l].astype(BF16), LANES, 4)
        ids = ids_t.reshape(PEER_SLOTS, t).T.reshape(t * PEER_SLOTS)
        table = _pack_expert_table(peer_u[l], peer_v[l])
        xf = _peer_gather(ids, table, x2, gates_t.reshape(PEER_SLOTS, t), ffn_norm_gain[l], final_norm_gain,
                          final_norm=(l == depth - 1))
    return xf.reshape(batch, seq, d)
```

```python
import functools
import math

import jax
import jax.numpy as jnp
from jax import lax
from jax.experimental import pallas as pl
from jax.experimental.pallas import tpu as pltpu

F32 = jnp.float32
BF16 = jnp.bfloat16

EPS = 1e-6
GN_EPS = 1e-5
ROPE_THETA = 10000.0
D_MODEL = 2048
HEAD_DIM = 64
ATT_Q_HEADS = 16
ATT_KV_HEADS = 2
ATT_WIDTH = ATT_Q_HEADS * HEAD_DIM
ATT_KV_WIDTH = ATT_KV_HEADS * HEAD_DIM
BLOCK = 128
RET_HEADS = 4
RET_HEAD_DIM = 256
RET_WIDTH = RET_HEADS * RET_HEAD_DIM
RET_CHUNK = 128
IN_COLS = ATT_WIDTH + 2 * ATT_KV_WIDTH + 4 * RET_WIDTH
XATT_HEADS = 4
XATT_HEAD_DIM = 128
XATT_WIDTH = XATT_HEADS * XATT_HEAD_DIM
PEER_HEADS = 8
PEER_N_KEYS = 128
PEER_HALF_DIM = 128
PEER_TOPK = 16
PEER_SLOTS = PEER_HEADS * PEER_TOPK

LANES = 128
SUBLANES = 8
VMEM_LIMIT = 56 * 1024 * 1024

_KA0 = ATT_WIDTH
_VA0 = _KA0 + ATT_KV_WIDTH
_QR0 = _VA0 + ATT_KV_WIDTH
_KR0 = _QR0 + RET_WIDTH
_VR0 = _KR0 + RET_WIDTH
_GR0 = _VR0 + RET_WIDTH


def _params(*sem):
    return pltpu.CompilerParams(dimension_semantics=sem, vmem_limit_bytes=VMEM_LIMIT)


def _rms(xf, gain):
    ms = jnp.mean(xf * xf, axis=-1, keepdims=True)
    return xf * lax.rsqrt(ms + EPS) * gain


def _dot(a, b):
    return jnp.dot(a, b, preferred_element_type=F32)


def _dot_nt(a, b):
    return lax.dot_general(a, b, (((1,), (1,)), ((), ())), preferred_element_type=F32)


def _norm_matmul_kernel(x_ref, g_ref, w_ref, o_ref, xn_ref):
    @pl.when(pl.program_id(1) == 0)
    def _():
        xn_ref[...] = _rms(x_ref[...], g_ref[...]).astype(BF16)

    o_ref[...] = _dot(xn_ref[...], w_ref[...]).astype(o_ref.dtype)


def _norm_matmul(x, gain, w_bf16, tm, tn, out_dtype):
    t, d = x.shape
    n = w_bf16.shape[1]
    return pl.pallas_call(
        _norm_matmul_kernel,
        grid=(t // tm, n // tn),
        in_specs=[
            pl.BlockSpec((tm, d), lambda i, j: (i, 0)),
            pl.BlockSpec((1, d), lambda i, j: (0, 0)),
            pl.BlockSpec((d, tn), lambda i, j: (0, j)),
        ],
        out_specs=pl.BlockSpec((tm, tn), lambda i, j: (i, j)),
        out_shape=jax.ShapeDtypeStruct((t, n), out_dtype),
        scratch_shapes=[pltpu.VMEM((tm, d), BF16)],
        compiler_params=_params("parallel", "arbitrary"),
        name="norm_matmul",
    )(x, gain.reshape(1, d), w_bf16)


def _swa_kernel(sink_ref, q_ref, kc_ref, kp_ref, vc_ref, vp_ref, cc_ref, sc_ref, cp_ref, sp_ref,
                gain_ref, o_ref):
    n = pl.program_id(1)
    lane = lax.broadcasted_iota(jnp.int32, (1, LANES), 1)
    first_half = (lane % HEAD_DIM) < (HEAD_DIM // 2)
    lo = lane < HEAD_DIM

    def rope(x, c, s):
        partner = jnp.where(first_half, pltpu.roll(x, LANES - HEAD_DIM // 2, 1),
                            pltpu.roll(x, HEAD_DIM // 2, 1))
        return x * c + partner * s

    cc, sc = cc_ref[...], sc_ref[...]
    k = jnp.concatenate([rope(kp_ref[...].astype(F32), cp_ref[...], sp_ref[...]),
                         rope(kc_ref[...].astype(F32), cc, sc)], axis=0)
    v = jnp.concatenate([vp_ref[...].astype(F32), vc_ref[...].astype(F32)], axis=0)
    k_sw = pltpu.roll(k, HEAD_DIM, 1)
    v_sw = pltpu.roll(v, HEAD_DIM, 1)

    def place(a, a_sw, c, half):
        src = a if c == half else a_sw
        keep = lo if half == 0 else jnp.logical_not(lo)
        return jnp.where(keep, src, 0.0).astype(BF16)

    kvar = {(c, h): place(k, k_sw, c, h) for c in range(ATT_KV_HEADS) for h in range(2)}
    vvar = {(c, h): place(v, v_sw, c, h) for c in range(ATT_KV_HEADS) for h in range(2)}

    qi = lax.broadcasted_iota(jnp.int32, (BLOCK, 2 * BLOCK), 0) + BLOCK
    ki = lax.broadcasted_iota(jnp.int32, (BLOCK, 2 * BLOCK), 1)
    dist = qi - ki
    kmin = jnp.where(n > 0, 0, BLOCK)
    valid = (dist >= 0) & (dist < BLOCK) & (ki >= kmin)

    group = ATT_Q_HEADS // ATT_KV_HEADS
    for j in range(ATT_WIDTH // LANES):
        c = (2 * j) // group
        cols = slice(j * LANES, (j + 1) * LANES)
        qg = (rope(q_ref[:, cols].astype(F32), cc, sc) * (HEAD_DIM ** -0.5)).astype(BF16)
        out = jnp.zeros((BLOCK, LANES), F32)
        for half in range(2):
            s = jnp.where(valid, _dot_nt(qg, kvar[(c, half)]), -jnp.inf)
            sink = sink_ref[2 * j + half]
            m = jnp.maximum(jnp.max(s, axis=-1, keepdims=True), sink)
            p = jnp.exp(s - m)
            denom = jnp.sum(p, axis=-1, keepdims=True) + jnp.exp(sink - m)
            out = out + _dot((p / denom).astype(BF16), vvar[(c, half)])
        sq = out * out
        ss_lo = jnp.sum(jnp.where(lo, sq, 0.0), axis=-1, keepdims=True)
        ss_hi = jnp.sum(jnp.where(lo, 0.0, sq), axis=-1, keepdims=True)
        ms = jnp.where(lo, ss_lo, ss_hi) * (1.0 / HEAD_DIM)
        o_ref[:, cols] = (out * lax.rsqrt(ms + EPS) * gain_ref[:, cols]).astype(o_ref.dtype)


def _swa(h, cos_a, sin_a, sinks, gain, batch, seq):
    nb = seq // BLOCK
    t = batch * seq
    kcol, vcol = _KA0 // LANES, _VA0 // LANES
    cur = lambda b, n: b * nb + n
    prev = lambda b, n: b * nb + jnp.maximum(n - 1, 0)
    row = lambda col, f: pl.BlockSpec((BLOCK, LANES), lambda b, n: (f(b, n), col))
    return pl.pallas_call(
        _swa_kernel,
        grid=(batch, nb),
        in_specs=[
            pl.BlockSpec(memory_space=pltpu.SMEM),
            pl.BlockSpec((BLOCK, ATT_WIDTH), lambda b, n: (cur(b, n), 0)),
            row(kcol, cur), row(kcol, prev), row(vcol, cur), row(vcol, prev),
            row(0, cur), row(0, cur), row(0, prev), row(0, prev),
            pl.BlockSpec((1, ATT_WIDTH), lambda b, n: (0, 0)),
        ],
        out_specs=pl.BlockSpec((BLOCK, ATT_WIDTH), lambda b, n: (cur(b, n), 0)),
        out_shape=jax.ShapeDtypeStruct((t, ATT_WIDTH), BF16),
        compiler_params=_params("parallel", "parallel"),
        name="swa_attention",
    )(sinks, h, h, h, h, h, cos_a, sin_a, cos_a, sin_a, gain.reshape(1, ATT_WIDTH))


def _ret_kernel(lg_ref, cd_ref, q_ref, k_ref, v_ref, g_ref, c_ref, s_ref, gain_ref, o_ref, state_ref):
    hh = pl.program_id(1)
    n = pl.program_id(2)

    @pl.when(n == 0)
    def _():
        state_ref[...] = jnp.zeros_like(state_ref)

    lg = lg_ref[hh]
    c, s = c_ref[...], s_ref[...]
    half = RET_HEAD_DIM // 2

    def rope(x):
        x1, x2 = x[:, :half], x[:, half:]
        return jnp.concatenate([x1 * c - x2 * s, x2 * c + x1 * s], axis=1)

    q = rope(q_ref[...].astype(F32))
    k = rope(k_ref[...].astype(F32)) * (RET_HEAD_DIM ** -0.5)
    v = v_ref[...].astype(BF16)

    ri = lax.broadcasted_iota(jnp.int32, (RET_CHUNK, RET_CHUNK), 0).astype(F32)
    ci = lax.broadcasted_iota(jnp.int32, (RET_CHUNK, RET_CHUNK), 1).astype(F32)
    diff = ri - ci
    decay = jnp.where(diff >= 0, jnp.exp(jnp.maximum(diff, 0.0) * lg), 0.0)
    zeta = jnp.exp((RET_CHUNK - 1 - ri) * lg)
    xi = jnp.exp((ri + 1.0) * lg)
    zeta2 = jnp.concatenate([zeta, zeta], axis=1)
    xi2 = jnp.concatenate([xi, xi], axis=1)

    inner = _dot_nt(q.astype(BF16), k.astype(BF16)) * decay
    state = state_ref[...]
    out = _dot(inner.astype(BF16), v) + _dot((q * xi2).astype(BF16), state.astype(BF16))
    kv = lax.dot_general((k * zeta2).astype(BF16), v, (((0,), (0,)), ((), ())), preferred_element_type=F32)
    state_ref[...] = state * cd_ref[hh] + kv

    mu = jnp.mean(out, axis=-1, keepdims=True)
    cen = out - mu
    var = jnp.mean(cen * cen, axis=-1, keepdims=True)
    g = g_ref[...].astype(F32)
    o = cen * lax.rsqrt(var + GN_EPS) * gain_ref[...] * (g * jax.nn.sigmoid(g))
    o_ref[...] = o.astype(o_ref.dtype)


def _retention(h, cos_r, sin_r, gain, batch, seq):
    nc = seq // RET_CHUNK
    t = batch * seq
    hd = RET_HEAD_DIM
    log_gamma = jnp.log1p(-jnp.exp2(-5.0 - jnp.arange(RET_HEADS, dtype=F32)))
    chunk_decay = jnp.exp(RET_CHUNK * log_gamma)
    col = lambda c0: pl.BlockSpec((RET_CHUNK, hd), lambda b, hh, n: (b * nc + n, c0 // hd + hh))
    tab = pl.BlockSpec((RET_CHUNK, hd // 2), lambda b, hh, n: (b * nc + n, 0))
    return pl.pallas_call(
        _ret_kernel,
        grid=(batch, RET_HEADS, nc),
        in_specs=[
            pl.BlockSpec(memory_space=pltpu.SMEM),
            pl.BlockSpec(memory_space=pltpu.SMEM),
            col(_QR0), col(_KR0), col(_VR0), col(_GR0), tab, tab,
            pl.BlockSpec((1, hd), lambda b, hh, n: (0, hh)),
        ],
        out_specs=pl.BlockSpec((RET_CHUNK, hd), lambda b, hh, n: (b * nc + n, hh)),
        out_shape=jax.ShapeDtypeStruct((t, RET_WIDTH), BF16),
        scratch_shapes=[pltpu.VMEM((hd, hd), F32)],
        compiler_params=_params("parallel", "parallel", "arbitrary"),
        name="retention",
    )(log_gamma, chunk_decay, h, h, h, h, cos_r, sin_r, gain.reshape(1, RET_WIDTH))


def _outproj_kernel(x_ref, oa_ref, or_ref, wa_ref, wr_ref, g_ref, wq_ref, x1_ref, qx_ref):
    x1 = x_ref[...] + _dot(oa_ref[...], wa_ref[...]) + _dot(or_ref[...], wr_ref[...])
    x1_ref[...] = x1
    qx_ref[...] = _dot(_rms(x1, g_ref[...]).astype(BF16), wq_ref[...]).astype(qx_ref.dtype)


def _outproj(x, oa, orr, w_out_bf16, gain, w_xq_bf16, tm):
    t, d = x.shape
    full = lambda shape: pl.BlockSpec(shape, lambda i: (0, 0))
    return pl.pallas_call(
        _outproj_kernel,
        grid=(t // tm,),
        in_specs=[
            pl.BlockSpec((tm, d), lambda i: (i, 0)),
            pl.BlockSpec((tm, ATT_WIDTH), lambda i: (i, 0)),
            pl.BlockSpec((tm, RET_WIDTH), lambda i: (i, 0)),
            pl.BlockSpec((ATT_WIDTH, d), lambda i: (0, 0)),
            pl.BlockSpec((RET_WIDTH, d), lambda i: (1, 0)),
            full((1, d)),
            full((d, XATT_WIDTH)),
        ],
        out_specs=[pl.BlockSpec((tm, d), lambda i: (i, 0)), pl.BlockSpec((tm, XATT_WIDTH), lambda i: (i, 0))],
        out_shape=[jax.ShapeDtypeStruct((t, d), F32), jax.ShapeDtypeStruct((t, XATT_WIDTH), BF16)],
        compiler_params=_params("parallel"),
        name="out_proj",
    )(x, oa, orr, w_out_bf16, w_out_bf16, gain.reshape(1, d), w_xq_bf16)


def _xattn_kernel(qx_ref, k_ref, v_ref, x1_ref, wo_ref, g_ref, wpq_ref, x2_ref, pq_ref):
    heads = []
    for hh in range(XATT_HEADS):
        cols = slice(hh * XATT_HEAD_DIM, (hh + 1) * XATT_HEAD_DIM)
        s = _dot_nt(qx_ref[:, cols], k_ref[:, cols]) * (XATT_HEAD_DIM ** -0.5)
        p = jnp.exp(s - jnp.max(s, axis=-1, keepdims=True))
        p = p / jnp.sum(p, axis=-1, keepdims=True)
        heads.append(_dot(p.astype(BF16), v_ref[:, cols]))
    o = jnp.concatenate(heads, axis=1).astype(BF16)
    x2 = x1_ref[...] + _dot(o, wo_ref[...])
    x2_ref[...] = x2
    pq_ref[...] = _dot(_rms(x2, g_ref[...]).astype(BF16), wpq_ref[...])


def _xattn(qx, kv_mem, x1, w_xo_bf16, gain, w_pq_bf16, batch, seq, mem_len, tm):
    t, d = x1.shape
    npq = w_pq_bf16.shape[1]
    nt = seq // tm
    rows = lambda width: pl.BlockSpec((tm, width), lambda b, i: (b * nt + i, 0))
    full = lambda shape: pl.BlockSpec(shape, lambda b, i: (0, 0))
    return pl.pallas_call(
        _xattn_kernel,
        grid=(batch, nt),
        in_specs=[
            rows(XATT_WIDTH),
            pl.BlockSpec((mem_len, XATT_WIDTH), lambda b, i: (b, 0)),
            pl.BlockSpec((mem_len, XATT_WIDTH), lambda b, i: (b, 1)),
            rows(d),
            full((XATT_WIDTH, d)),
            full((1, d)),
            full((d, npq)),
        ],
        out_specs=[rows(d), rows(npq)],
        out_shape=[jax.ShapeDtypeStruct((t, d), F32), jax.ShapeDtypeStruct((t, npq), F32)],
        compiler_params=_params("parallel", "parallel"),
        name="cross_attention",
    )(qx, kv_mem, kv_mem, x1, w_xo_bf16, gain.reshape(1, d), w_pq_bf16)


def _topk_rows(s, k, payload=None):
    nrows = s.shape[0]
    rows = lax.broadcasted_iota(jnp.int32, s.shape, 0).astype(F32)
    vals, sel = [], []
    for _ in range(k):
        m = jnp.max(s, axis=0, keepdims=True)
        am = jnp.min(jnp.where(s == m, rows, float(nrows)), axis=0, keepdims=True)
        hit = rows == am
        vals.append(m)
        sel.append(am if payload is None else jnp.max(jnp.where(hit, payload, -1.0), axis=0, keepdims=True))
        s = jnp.where(hit, -jnp.inf, s)
    return jnp.concatenate(vals, axis=0), jnp.concatenate(sel, axis=0)


def _staircase(t1, t2):
    kk, half = PEER_TOPK, SUBLANES
    blocks = [(t1[0:1, :], t2, kk), (t1[1:2, :], t2[0:half, :], half)]
    blocks += [(t1[a:a + 1, :], t2[0:half, :], kk // (a + 1)) for a in range(2, half)]
    blocks.append((t1[half:kk, :], t2[0:1, :], half))
    return blocks


def _peer_topk_kernel(q_ref, keys_ref, ids_ref, gates_ref, *, heads):
    width = 2 * PEER_HALF_DIM
    row8 = lax.broadcasted_iota(jnp.int32, (SUBLANES, q_ref.shape[0]), 0)
    for hh in range(heads):
        tops = []
        for p in range(2):
            c0 = hh * width + p * PEER_HALF_DIM
            qp = q_ref[:, c0:c0 + PEER_HALF_DIM].astype(BF16)
            tops.append(_topk_rows(_dot_nt(keys_ref[hh, p], qp), PEER_TOPK))
        (s1, i1), (s2, i2) = tops
        cand_s, cand_i = [], []
        for (a_s, b_s, live), (a_i, b_i, _) in zip(_staircase(s1, s2), _staircase(i1 * float(PEER_N_KEYS), i2)):
            blk = a_s + b_s
            if live < blk.shape[0]:
                blk = jnp.where(row8 < live, blk, -jnp.inf)
            cand_s.append(blk)
            cand_i.append(a_i + b_i)
        top_s, top_e = _topk_rows(jnp.concatenate(cand_s, axis=0), PEER_TOPK,
                                  payload=jnp.concatenate(cand_i, axis=0))
        e = jnp.exp(top_s - top_s[0:1, :])
        ids_ref[hh] = top_e.astype(jnp.int32)
        gates_ref[hh] = e / jnp.sum(e, axis=0, keepdims=True)


def _peer_topk(pq, keys_bf16, tt, heads):
    t = pq.shape[0]
    width = 2 * PEER_HALF_DIM
    out = pl.BlockSpec((heads, PEER_TOPK, tt), lambda i, hh: (hh, 0, i))
    return pl.pallas_call(
        functools.partial(_peer_topk_kernel, heads=heads),
        grid=(t // tt, PEER_HEADS // heads),
        in_specs=[
            pl.BlockSpec((tt, heads * width), lambda i, hh: (i, hh)),
            pl.BlockSpec((heads, 2, PEER_N_KEYS, PEER_HALF_DIM), lambda i, hh: (hh, 0, 0, 0)),
        ],
        out_specs=[out, out],
        out_shape=[jax.ShapeDtypeStruct((PEER_HEADS, PEER_TOPK, t), jnp.int32),
                   jax.ShapeDtypeStruct((PEER_HEADS, PEER_TOPK, t), F32)],
        compiler_params=_params("parallel", "parallel"),
        name="peer_topk",
    )(pq, keys_bf16)


GATHER_TOKENS = 128
GATHER_BUFS = 8


def _peer_gather_kernel(ids_hbm, tbl_hbm, x2_ref, gates_ref, fg_ref, og_ref, out_ref,
                        ids_smem, ids_sem, *scratch, final_norm):
    gbufs = scratch[:GATHER_BUFS]
    gsem, xn_ref, y_ref, yacc_ref = scratch[GATHER_BUFS:]
    i = pl.program_id(0)
    has_next = i + 1 < pl.num_programs(0)
    slot = i % 2
    other = 1 - slot
    tb = GATHER_TOKENS
    nchunk = D_MODEL // LANES
    nids = tb * PEER_SLOTS
    ahead = GATHER_BUFS - 1
    group = SUBLANES

    def ids_copy(step, sl):
        return pltpu.make_async_copy(ids_hbm.at[pl.ds(pl.multiple_of(step * nids, nids), nids)],
                                     ids_smem.at[pl.ds(pl.multiple_of(sl * nids, nids), nids)], ids_sem.at[sl])

    def row_copy(e, b, k):
        return pltpu.make_async_copy(tbl_hbm.at[e], gbufs[b].at[pl.ds(k, 1)], gsem.at[b])

    def issue(sl, tok, b):
        base = sl * nids + tok * PEER_SLOTS
        for k in range(PEER_SLOTS):
            row_copy(ids_smem[base + k], b, k).start(priority=k % 2)

    def wait_rows(b):
        for k in range(PEER_SLOTS):
            row_copy(0, b, k).wait()

    @pl.when(i == 0)
    def _():
        ids_copy(0, 0).start()
        ids_copy(0, 0).wait()
        for tok in range(ahead):
            issue(0, tok, tok)

    @pl.when(has_next)
    def _():
        ids_copy(i + 1, other).start()

    xn_ref[...] = _rms(x2_ref[...], fg_ref[...])

    lane = lax.broadcasted_iota(jnp.int32, (PEER_SLOTS, LANES), 1)
    hi_mask = jnp.uint32(0xFFFF0000)

    def compute(tok, r, b):
        gbuf = gbufs[b]
        xrow = xn_ref[pl.ds(tok, 1), :]
        acc = jnp.zeros((PEER_SLOTS, LANES), F32)
        for c in range(nchunk):
            w = gbuf[:, c * LANES:(c + 1) * LANES]
            u = lax.bitcast_convert_type(w << 16, F32)
            acc = acc + u * xrow[:, c * LANES:(c + 1) * LANES]
        a = jnp.sum(acc, axis=-1, keepdims=True)
        gate = jnp.sum(jnp.where(lane == tok, gates_ref[...], 0.0), axis=-1, keepdims=True)
        coeff = 0.5 * a * (1.0 + lax.erf(a * (2.0 ** -0.5))) * gate
        for c in range(nchunk):
            w = gbuf[:, c * LANES:(c + 1) * LANES]
            vv = lax.bitcast_convert_type(w & hi_mask, F32)
            yacc_ref[r:r + 1, c * LANES:(c + 1) * LANES] = jnp.sum(vv * coeff, axis=0, keepdims=True)

    def run_group(g, last):
        for r in range(group):
            b = r % GATHER_BUFS
            nb = (r + ahead) % GATHER_BUFS
            tok = g * group + r
            wait_rows(b)
            if not last or r + ahead < group:
                issue(slot, tok + ahead, nb)
            else:
                if r + ahead == group:
                    @pl.when(has_next)
                    def _():
                        ids_copy(i + 1, other).wait()

                @pl.when(has_next)
                def _():
                    issue(other, r + ahead - group, nb)
            compute(tok, r, b)
        y_ref[pl.ds(pl.multiple_of(g * group, group), group), :] = yacc_ref[...]

    def body(g, carry):
        run_group(g, last=False)
        return carry

    ngroups = tb // group
    lax.fori_loop(0, ngroups - 1, body, 0)
    run_group(ngroups - 1, last=True)
    x3 = x2_ref[...] + y_ref[...]
    out_ref[...] = _rms(x3, og_ref[...]) if final_norm else x3


def _peer_gather(ids_flat, table, x2, gates_t, ffn_gain, final_gain, final_norm):
    t, d = x2.shape
    tb = GATHER_TOKENS
    nsteps = t // tb
    full = lambda shape: pl.BlockSpec(shape, lambda i: (0, 0))
    return pl.pallas_call(
        functools.partial(_peer_gather_kernel, final_norm=final_norm),
        grid=(nsteps,),
        in_specs=[
            pl.BlockSpec(memory_space=pl.ANY),
            pl.BlockSpec(memory_space=pl.ANY),
            pl.BlockSpec((tb, d), lambda i: (i, 0)),
            pl.BlockSpec((PEER_SLOTS, tb), lambda i: (0, i)),
            full((1, d)),
            full((1, d)),
        ],
        out_specs=pl.BlockSpec((tb, d), lambda i: (i, 0)),
        out_shape=jax.ShapeDtypeStruct((t, d), F32),
        scratch_shapes=[
            pltpu.SMEM((2 * tb * PEER_SLOTS,), jnp.int32),
            pltpu.SemaphoreType.DMA((2,)),
            *[pltpu.VMEM((PEER_SLOTS, d), jnp.uint32) for _ in range(GATHER_BUFS)],
            pltpu.SemaphoreType.DMA((GATHER_BUFS,)),
            pltpu.VMEM((tb, d), F32),
            pltpu.VMEM((tb, d), F32),
            pltpu.VMEM((SUBLANES, d), F32),
        ],
        compiler_params=_params("arbitrary"),
        name="peer_gather",
    )(ids_flat, table, x2, gates_t, ffn_gain.reshape(1, d), final_gain.reshape(1, d))


def _rope_angles(positions, dh):
    inv_freq = ROPE_THETA ** (-jnp.arange(0, dh, 2, dtype=F32) / dh)
    ang = positions.astype(F32).reshape(-1, 1) * inv_freq
    return jnp.cos(ang), jnp.sin(ang)


def _rope_tables_att(positions):
    cos, sin = _rope_angles(positions, HEAD_DIM)
    copies = LANES // HEAD_DIM
    return (jnp.tile(jnp.concatenate([cos, cos], axis=1), (1, copies)),
            jnp.tile(jnp.concatenate([-sin, sin], axis=1), (1, copies)))


def _pack_expert_table(u, v):
    ub = lax.bitcast_convert_type(u.astype(BF16), jnp.uint16).astype(jnp.uint32)
    vb = lax.bitcast_convert_type(v.astype(BF16), jnp.uint16).astype(jnp.uint32)
    return ((vb << 16) | ub).reshape(u.shape[0], 1, u.shape[1])


def kernel(x, mem, positions, mix_norm_gain, w_in, att_sinks, att_out_gain, ret_out_gain, w_out,
           cross_norm_gain, mem_norm_gain, w_xq, w_xk, w_xv, w_xo, ffn_norm_gain,
           w_peer_q, peer_sub_keys, peer_u, peer_v, final_norm_gain):
    batch, seq, d = x.shape
    mem_len = mem.shape[1]
    t = batch * seq
    depth = w_in.shape[0]
    assert d == D_MODEL and seq % BLOCK == 0 and t % GATHER_TOKENS == 0

    cos_a, sin_a = _rope_tables_att(positions)
    cos_r, sin_r = _rope_angles(positions, RET_HEAD_DIM)
    xf = x.reshape(t, d)
    memf = mem.reshape(batch * mem_len, d)
    for l in range(depth):
        h = _norm_matmul(xf, mix_norm_gain[l], w_in[l].astype(BF16), min(1024, t), 768, BF16)
        oa = _swa(h, cos_a, sin_a, att_sinks[l], att_out_gain[l], batch, seq)
        orr = _retention(h, cos_r, sin_r, ret_out_gain[l], batch, seq)
        x1, qx = _outproj(xf, oa, orr, w_out[l].astype(BF16), cross_norm_gain[l], w_xq[l].astype(BF16),
                          min(256, seq))
        w_kv = jnp.concatenate([w_xk[l], w_xv[l]], axis=1).astype(BF16)
        kv_mem = _norm_matmul(memf, mem_norm_gain[l], w_kv, mem_len, 2 * XATT_WIDTH, BF16)
        x2, pq = _xattn(qx, kv_mem, x1, w_xo[l].astype(BF16), ffn_norm_gain[l], w_peer_q[l].astype(BF16),
                        batch, seq, mem_len, min(256, seq))
        ids_t, gates_t = _peer_topk(pq, peer_sub_keys[l].astype(BF16), LANES, 4)
        ids = ids_t.reshape(PEER_SLOTS, t).T.reshape(t * PEER_SLOTS)
        table = _pack_expert_table(peer_u[l], peer_v[l])
        xf = _peer_gather(ids, table, x2, gates_t.reshape(PEER_SLOTS, t), ffn_norm_gain[l], final_norm_gain,
                          final_norm=(l == depth - 1))
    return xf.reshape(batch, seq, d)
```

```python
import functools
import math

import jax
import jax.numpy as jnp
from jax import lax
from jax.experimental import pallas as pl
from jax.experimental.pallas import tpu as pltpu

F32 = jnp.float32
BF16 = jnp.bfloat16

EPS = 1e-6
GN_EPS = 1e-5
ROPE_THETA = 10000.0
D_MODEL = 2048
HEAD_DIM = 64
ATT_Q_HEADS = 16
ATT_KV_HEADS = 2
ATT_WIDTH = ATT_Q_HEADS * HEAD_DIM
ATT_KV_WIDTH = ATT_KV_HEADS * HEAD_DIM
BLOCK = 128
RET_HEADS = 4
RET_HEAD_DIM = 256
RET_WIDTH = RET_HEADS * RET_HEAD_DIM
RET_CHUNK = 128
IN_COLS = ATT_WIDTH + 2 * ATT_KV_WIDTH + 4 * RET_WIDTH
XATT_HEADS = 4
XATT_HEAD_DIM = 128
XATT_WIDTH = XATT_HEADS * XATT_HEAD_DIM
PEER_HEADS = 8
PEER_N_KEYS = 128
PEER_HALF_DIM = 128
PEER_TOPK = 16
PEER_SLOTS = PEER_HEADS * PEER_TOPK

LANES = 128
SUBLANES = 8
VMEM_LIMIT = 56 * 1024 * 1024

_REF_QR0 = ATT_WIDTH + 2 * ATT_KV_WIDTH
_QR0 = 0
_KR0 = _QR0 + RET_WIDTH
_VR0 = _KR0 + RET_WIDTH
_GR0 = _VR0 + RET_WIDTH
_QA0 = _GR0 + RET_WIDTH
_KA0 = _QA0 + ATT_WIDTH
_VA0 = _KA0 + ATT_KV_WIDTH


def _reorder_w_in(w):
    return jnp.concatenate([w[:, _REF_QR0:], w[:, :_REF_QR0]], axis=1)


def _params(*sem):
    return pltpu.CompilerParams(dimension_semantics=sem, vmem_limit_bytes=VMEM_LIMIT)


def _rms(xf, gain):
    ms = jnp.mean(xf * xf, axis=-1, keepdims=True)
    return xf * lax.rsqrt(ms + EPS) * gain


def _dot(a, b):
    return jnp.dot(a, b, preferred_element_type=F32)


def _dot_nt(a, b):
    return lax.dot_general(a, b, (((1,), (1,)), ((), ())), preferred_element_type=F32)


def _norm_matmul_kernel(x_ref, g_ref, w_ref, o_ref, xn_ref):
    @pl.when(pl.program_id(1) == 0)
    def _():
        xn_ref[...] = _rms(x_ref[...], g_ref[...]).astype(BF16)

    o_ref[...] = _dot(xn_ref[...], w_ref[...]).astype(o_ref.dtype)


def _norm_matmul(x, gain, w_bf16, tm, tn, out_dtype):
    t, d = x.shape
    n = w_bf16.shape[1]
    return pl.pallas_call(
        _norm_matmul_kernel,
        grid=(t // tm, n // tn),
        in_specs=[
            pl.BlockSpec((tm, d), lambda i, j: (i, 0)),
            pl.BlockSpec((1, d), lambda i, j: (0, 0)),
            pl.BlockSpec((d, tn), lambda i, j: (0, j)),
        ],
        out_specs=pl.BlockSpec((tm, tn), lambda i, j: (i, j)),
        out_shape=jax.ShapeDtypeStruct((t, n), out_dtype),
        scratch_shapes=[pltpu.VMEM((tm, d), BF16)],
        compiler_params=_params("parallel", "arbitrary"),
        name="norm_matmul",
    )(x, gain.reshape(1, d), w_bf16)


def _swa_kernel(sink_ref, q_ref, kc_ref, kp_ref, vc_ref, vp_ref, cc_ref, sc_ref, cp_ref, sp_ref,
                gain_ref, o_ref):
    n = pl.program_id(1)
    lane = lax.broadcasted_iota(jnp.int32, (1, LANES), 1)
    first_half = (lane % HEAD_DIM) < (HEAD_DIM // 2)
    lo = lane < HEAD_DIM

    def rope(x, c, s):
        partner = jnp.where(first_half, pltpu.roll(x, LANES - HEAD_DIM // 2, 1),
                            pltpu.roll(x, HEAD_DIM // 2, 1))
        return x * c + partner * s

    cc, sc = cc_ref[...], sc_ref[...]
    k = jnp.concatenate([rope(kp_ref[...].astype(F32), cp_ref[...], sp_ref[...]),
                         rope(kc_ref[...].astype(F32), cc, sc)], axis=0)
    v = jnp.concatenate([vp_ref[...].astype(F32), vc_ref[...].astype(F32)], axis=0)
    k_sw = pltpu.roll(k, HEAD_DIM, 1)
    v_sw = pltpu.roll(v, HEAD_DIM, 1)

    def place(a, a_sw, c, half):
        src = a if c == half else a_sw
        keep = lo if half == 0 else jnp.logical_not(lo)
        return jnp.where(keep, src, 0.0).astype(BF16)

    kvar = {(c, h): place(k, k_sw, c, h) for c in range(ATT_KV_HEADS) for h in range(2)}
    vvar = {(c, h): place(v, v_sw, c, h) for c in range(ATT_KV_HEADS) for h in range(2)}

    qi = lax.broadcasted_iota(jnp.int32, (BLOCK, 2 * BLOCK), 0) + BLOCK
    ki = lax.broadcasted_iota(jnp.int32, (BLOCK, 2 * BLOCK), 1)
    dist = qi - ki
    kmin = jnp.where(n > 0, 0, BLOCK)
    valid = (dist >= 0) & (dist < BLOCK) & (ki >= kmin)

    group = ATT_Q_HEADS // ATT_KV_HEADS
    for j in range(ATT_WIDTH // LANES):
        c = (2 * j) // group
        cols = slice(j * LANES, (j + 1) * LANES)
        qg = (rope(q_ref[:, cols].astype(F32), cc, sc) * (HEAD_DIM ** -0.5)).astype(BF16)
        out = jnp.zeros((BLOCK, LANES), F32)
        for half in range(2):
            s = jnp.where(valid, _dot_nt(qg, kvar[(c, half)]), -jnp.inf)
            sink = sink_ref[2 * j + half]
            m = jnp.maximum(jnp.max(s, axis=-1, keepdims=True), sink)
            p = jnp.exp(s - m)
            denom = jnp.sum(p, axis=-1, keepdims=True) + jnp.exp(sink - m)
            out = out + _dot((p / denom).astype(BF16), vvar[(c, half)])
        sq = out * out
        ss_lo = jnp.sum(jnp.where(lo, sq, 0.0), axis=-1, keepdims=True)
        ss_hi = jnp.sum(jnp.where(lo, 0.0, sq), axis=-1, keepdims=True)
        ms = jnp.where(lo, ss_lo, ss_hi) * (1.0 / HEAD_DIM)
        o_ref[:, cols] = (out * lax.rsqrt(ms + EPS) * gain_ref[:, cols]).astype(o_ref.dtype)


def _swa(h, cos_a, sin_a, sinks, gain, batch, seq):
    nb = seq // BLOCK
    t = batch * seq
    kcol, vcol = _KA0 // LANES, _VA0 // LANES
    cur = lambda b, n: b * nb + n
    prev = lambda b, n: b * nb + jnp.maximum(n - 1, 0)
    row = lambda col, f: pl.BlockSpec((BLOCK, LANES), lambda b, n: (f(b, n), col))
    return pl.pallas_call(
        _swa_kernel,
        grid=(batch, nb),
        in_specs=[
            pl.BlockSpec(memory_space=pltpu.SMEM),
            pl.BlockSpec((BLOCK, ATT_WIDTH), lambda b, n: (cur(b, n), _QA0 // ATT_WIDTH)),
            row(kcol, cur), row(kcol, prev), row(vcol, cur), row(vcol, prev),
            row(0, cur), row(0, cur), row(0, prev), row(0, prev),
            pl.BlockSpec((1, ATT_WIDTH), lambda b, n: (0, 0)),
        ],
        out_specs=pl.BlockSpec((BLOCK, ATT_WIDTH), lambda b, n: (cur(b, n), 0)),
        out_shape=jax.ShapeDtypeStruct((t, ATT_WIDTH), BF16),
        compiler_params=_params("parallel", "parallel"),
        name="swa_attention",
    )(sinks, h, h, h, h, h, cos_a, sin_a, cos_a, sin_a, gain.reshape(1, ATT_WIDTH))


def _ret_kernel(lg_ref, cd_ref, q_ref, k_ref, v_ref, g_ref, c_ref, s_ref, gain_ref, o_ref, state_ref):
    n = pl.program_id(1)

    @pl.when(n == 0)
    def _():
        state_ref[...] = jnp.zeros_like(state_ref)

    c, s = c_ref[...], s_ref[...]
    hd = RET_HEAD_DIM
    half = hd // 2

    def rope(x):
        x1, x2 = x[:, :half], x[:, half:]
        return jnp.concatenate([x1 * c - x2 * s, x2 * c + x1 * s], axis=1)

    ri = lax.broadcasted_iota(jnp.int32, (RET_CHUNK, RET_CHUNK), 0).astype(F32)
    ci = lax.broadcasted_iota(jnp.int32, (RET_CHUNK, RET_CHUNK), 1).astype(F32)
    diff = ri - ci
    for hh in range(RET_HEADS):
        cols = slice(hh * hd, (hh + 1) * hd)
        lg = lg_ref[hh]
        q = rope(q_ref[:, cols].astype(F32))
        k = rope(k_ref[:, cols].astype(F32)) * (hd ** -0.5)
        v = v_ref[:, cols].astype(BF16)

        decay = jnp.where(diff >= 0, jnp.exp(jnp.maximum(diff, 0.0) * lg), 0.0)
        zeta = jnp.exp((RET_CHUNK - 1 - ri) * lg)
        xi = jnp.exp((ri + 1.0) * lg)
        zeta2 = jnp.concatenate([zeta, zeta], axis=1)
        xi2 = jnp.concatenate([xi, xi], axis=1)

        inner = _dot_nt(q.astype(BF16), k.astype(BF16)) * decay
        state = state_ref[hh]
        out = _dot(inner.astype(BF16), v) + _dot((q * xi2).astype(BF16), state.astype(BF16))
        kv = lax.dot_general((k * zeta2).astype(BF16), v, (((0,), (0,)), ((), ())), preferred_element_type=F32)
        state_ref[hh] = state * cd_ref[hh] + kv

        mu = jnp.mean(out, axis=-1, keepdims=True)
        cen = out - mu
        var = jnp.mean(cen * cen, axis=-1, keepdims=True)
        g = g_ref[:, cols].astype(F32)
        o = cen * lax.rsqrt(var + GN_EPS) * gain_ref[:, cols] * (g * jax.nn.sigmoid(g))
        o_ref[:, cols] = o.astype(o_ref.dtype)


def _retention(h, cos_r, sin_r, gain, batch, seq):
    nc = seq // RET_CHUNK
    t = batch * seq
    hd = RET_HEAD_DIM
    log_gamma = jnp.log1p(-jnp.exp2(-5.0 - jnp.arange(RET_HEADS, dtype=F32)))
    chunk_decay = jnp.exp(RET_CHUNK * log_gamma)
    col = lambda c0: pl.BlockSpec((RET_CHUNK, RET_WIDTH), lambda b, n: (b * nc + n, c0 // RET_WIDTH))
    tab = pl.BlockSpec((RET_CHUNK, hd // 2), lambda b, n: (b * nc + n, 0))
    return pl.pallas_call(
        _ret_kernel,
        grid=(batch, nc),
        in_specs=[
            pl.BlockSpec(memory_space=pltpu.SMEM),
            pl.BlockSpec(memory_space=pltpu.SMEM),
            col(_QR0), col(_KR0), col(_VR0), col(_GR0), tab, tab,
            pl.BlockSpec((1, RET_WIDTH), lambda b, n: (0, 0)),
        ],
        out_specs=pl.BlockSpec((RET_CHUNK, RET_WIDTH), lambda b, n: (b * nc + n, 0)),
        out_shape=jax.ShapeDtypeStruct((t, RET_WIDTH), BF16),
        scratch_shapes=[pltpu.VMEM((RET_HEADS, hd, hd), F32)],
        compiler_params=_params("parallel", "arbitrary"),
        name="retention",
    )(log_gamma, chunk_decay, h, h, h, h, cos_r, sin_r, gain.reshape(1, RET_WIDTH))


def _outproj_kernel(x_ref, oa_ref, or_ref, wa_ref, wr_ref, g_ref, wq_ref, x1_ref, qx_ref):
    x1 = x_ref[...] + _dot(oa_ref[...], wa_ref[...]) + _dot(or_ref[...], wr_ref[...])
    x1_ref[...] = x1
    qx_ref[...] = _dot(_rms(x1, g_ref[...]).astype(BF16), wq_ref[...]).astype(qx_ref.dtype)


def _outproj(x, oa, orr, w_out_bf16, gain, w_xq_bf16, tm):
    t, d = x.shape
    full = lambda shape: pl.BlockSpec(shape, lambda i: (0, 0))
    return pl.pallas_call(
        _outproj_kernel,
        grid=(t // tm,),
        in_specs=[
            pl.BlockSpec((tm, d), lambda i: (i, 0)),
            pl.BlockSpec((tm, ATT_WIDTH), lambda i: (i, 0)),
            pl.BlockSpec((tm, RET_WIDTH), lambda i: (i, 0)),
            pl.BlockSpec((ATT_WIDTH, d), lambda i: (0, 0)),
            pl.BlockSpec((RET_WIDTH, d), lambda i: (1, 0)),
            full((1, d)),
            full((d, XATT_WIDTH)),
        ],
        out_specs=[pl.BlockSpec((tm, d), lambda i: (i, 0)), pl.BlockSpec((tm, XATT_WIDTH), lambda i: (i, 0))],
        out_shape=[jax.ShapeDtypeStruct((t, d), F32), jax.ShapeDtypeStruct((t, XATT_WIDTH), BF16)],
        compiler_params=_params("parallel"),
        name="out_proj",
    )(x, oa, orr, w_out_bf16, w_out_bf16, gain.reshape(1, d), w_xq_bf16)


def _xattn_kernel(qx_ref, k_ref, v_ref, x1_ref, wo_ref, g_ref, wpq_ref, x2_ref, pq_ref):
    heads = []
    for hh in range(XATT_HEADS):
        cols = slice(hh * XATT_HEAD_DIM, (hh + 1) * XATT_HEAD_DIM)
        s = _dot_nt(qx_ref[:, cols], k_ref[:, cols]) * (XATT_HEAD_DIM ** -0.5)
        p = jnp.exp(s - jnp.max(s, axis=-1, keepdims=True))
        p = p / jnp.sum(p, axis=-1, keepdims=True)
        heads.append(_dot(p.astype(BF16), v_ref[:, cols]))
    o = jnp.concatenate(heads, axis=1).astype(BF16)
    x2 = x1_ref[...] + _dot(o, wo_ref[...])
    x2_ref[...] = x2
    pq_ref[...] = _dot(_rms(x2, g_ref[...]).astype(BF16), wpq_ref[...])


def _xattn(qx, kv_mem, x1, w_xo_bf16, gain, w_pq_bf16, batch, seq, mem_len, tm):
    t, d = x1.shape
    npq = w_pq_bf16.shape[1]
    nt = seq // tm
    rows = lambda width: pl.BlockSpec((tm, width), lambda b, i: (b * nt + i, 0))
    full = lambda shape: pl.BlockSpec(shape, lambda b, i: (0, 0))
    return pl.pallas_call(
        _xattn_kernel,
        grid=(batch, nt),
        in_specs=[
            rows(XATT_WIDTH),
            pl.BlockSpec((mem_len, XATT_WIDTH), lambda b, i: (b, 0)),
            pl.BlockSpec((mem_len, XATT_WIDTH), lambda b, i: (b, 1)),
            rows(d),
            full((XATT_WIDTH, d)),
            full((1, d)),
            full((d, npq)),
        ],
        out_specs=[rows(d), rows(npq)],
        out_shape=[jax.ShapeDtypeStruct((t, d), F32), jax.ShapeDtypeStruct((t, npq), F32)],
        compiler_params=_params("parallel", "parallel"),
        name="cross_attention",
    )(qx, kv_mem, kv_mem, x1, w_xo_bf16, gain.reshape(1, d), w_pq_bf16)


def _topk_rows(s, k, payload=None):
    nrows = s.shape[0]
    rows = lax.broadcasted_iota(jnp.int32, s.shape, 0).astype(F32)
    vals, sel = [], []
    for _ in range(k):
        m = jnp.max(s, axis=0, keepdims=True)
        am = jnp.min(jnp.where(s == m, rows, float(nrows)), axis=0, keepdims=True)
        hit = rows == am
        vals.append(m)
        sel.append(am if payload is None else jnp.max(jnp.where(hit, payload, -1.0), axis=0, keepdims=True))
        s = jnp.where(hit, -jnp.inf, s)
    return jnp.concatenate(vals, axis=0), jnp.concatenate(sel, axis=0)


def _staircase(t1, t2):
    kk, half = PEER_TOPK, SUBLANES
    blocks = [(t1[0:1, :], t2, kk), (t1[1:2, :], t2[0:half, :], half)]
    blocks += [(t1[a:a + 1, :], t2[0:half, :], kk // (a + 1)) for a in range(2, half)]
    blocks.append((t1[half:kk, :], t2[0:1, :], half))
    return blocks


def _peer_topk_kernel(q_ref, keys_ref, ids_ref, gates_ref, *, heads):
    width = 2 * PEER_HALF_DIM
    row8 = lax.broadcasted_iota(jnp.int32, (SUBLANES, q_ref.shape[0]), 0)
    for hh in range(heads):
        tops = []
        for p in range(2):
            c0 = hh * width + p * PEER_HALF_DIM
            qp = q_ref[:, c0:c0 + PEER_HALF_DIM].astype(BF16)
            tops.append(_topk_rows(_dot_nt(keys_ref[hh, p], qp), PEER_TOPK))
        (s1, i1), (s2, i2) = tops
        cand_s, cand_i = [], []
        for (a_s, b_s, live), (a_i, b_i, _) in zip(_staircase(s1, s2), _staircase(i1 * float(PEER_N_KEYS), i2)):
            blk = a_s + b_s
            if live < blk.shape[0]:
                blk = jnp.where(row8 < live, blk, -jnp.inf)
            cand_s.append(blk)
            cand_i.append(a_i + b_i)
        top_s, top_e = _topk_rows(jnp.concatenate(cand_s, axis=0), PEER_TOPK,
                                  payload=jnp.concatenate(cand_i, axis=0))
        e = jnp.exp(top_s - top_s[0:1, :])
        ids_ref[hh] = top_e.astype(jnp.int32)
        gates_ref[hh] = e / jnp.sum(e, axis=0, keepdims=True)


def _peer_topk(pq, keys_bf16, tt, heads):
    t = pq.shape[0]
    width = 2 * PEER_HALF_DIM
    out = pl.BlockSpec((heads, PEER_TOPK, tt), lambda i, hh: (hh, 0, i))
    return pl.pallas_call(
        functools.partial(_peer_topk_kernel, heads=heads),
        grid=(t // tt, PEER_HEADS // heads),
        in_specs=[
            pl.BlockSpec((tt, heads * width), lambda i, hh: (i, hh)),
            pl.BlockSpec((heads, 2, PEER_N_KEYS, PEER_HALF_DIM), lambda i, hh: (hh, 0, 0, 0)),
        ],
        out_specs=[out, out],
        out_shape=[jax.ShapeDtypeStruct((PEER_HEADS, PEER_TOPK, t), jnp.int32),
                   jax.ShapeDtypeStruct((PEER_HEADS, PEER_TOPK, t), F32)],
        compiler_params=_params("parallel", "parallel"),
        name="peer_topk",
    )(pq, keys_bf16)


GATHER_TOKENS = 128
GATHER_BUFS = 8
GATHER_GROUP = 8
assert GATHER_GROUP % GATHER_BUFS == 0 and GATHER_GROUP % SUBLANES == 0 and GATHER_TOKENS % GATHER_GROUP == 0


def _peer_gather_kernel(ids_hbm, tbl_hbm, x2_ref, gates_ref, fg_ref, og_ref, out_ref,
                        ids_smem, ids_sem, *scratch, final_norm):
    gbufs = scratch[:GATHER_BUFS]
    gsem, xn_ref, y_ref, yacc_ref = scratch[GATHER_BUFS:]
    i = pl.program_id(0)
    has_next = i + 1 < pl.num_programs(0)
    slot = i % 2
    other = 1 - slot
    tb = GATHER_TOKENS
    nchunk = D_MODEL // LANES
    nids = tb * PEER_SLOTS
    ahead = GATHER_BUFS - 1
    group = GATHER_GROUP

    def ids_copy(step, sl):
        return pltpu.make_async_copy(ids_hbm.at[pl.ds(pl.multiple_of(step * nids, nids), nids)],
                                     ids_smem.at[pl.ds(pl.multiple_of(sl * nids, nids), nids)], ids_sem.at[sl])

    def row_copy(e, b, k):
        return pltpu.make_async_copy(tbl_hbm.at[e], gbufs[b].at[pl.ds(k, 1)], gsem.at[b])

    def issue(sl, tok, b):
        base = sl * nids + tok * PEER_SLOTS
        for k in range(PEER_SLOTS):
            row_copy(ids_smem[base + k], b, k).start(priority=k % 2)

    def wait_rows(b):
        for k in range(PEER_SLOTS):
            row_copy(0, b, k).wait()

    @pl.when(i == 0)
    def _():
        ids_copy(0, 0).start()
        ids_copy(0, 0).wait()
        for tok in range(ahead):
            issue(0, tok, tok)

    @pl.when(has_next)
    def _():
        ids_copy(i + 1, other).start()

    xn_ref[...] = _rms(x2_ref[...], fg_ref[...])

    lane = lax.broadcasted_iota(jnp.int32, (PEER_SLOTS, LANES), 1)
    hi_mask = jnp.uint32(0xFFFF0000)

    def compute(tok, r, b):
        gbuf = gbufs[b]
        xrow = xn_ref[pl.ds(tok, 1), :]
        acc = jnp.zeros((PEER_SLOTS, LANES), F32)
        for c in range(nchunk):
            w = gbuf[:, c * LANES:(c + 1) * LANES]
            u = lax.bitcast_convert_type(w << 16, F32)
            acc = acc + u * xrow[:, c * LANES:(c + 1) * LANES]
        a = jnp.sum(acc, axis=-1, keepdims=True)
        gate = jnp.sum(jnp.where(lane == tok, gates_ref[...], 0.0), axis=-1, keepdims=True)
        coeff = 0.5 * a * (1.0 + lax.erf(a * (2.0 ** -0.5))) * gate
        for c in range(nchunk):
            w = gbuf[:, c * LANES:(c + 1) * LANES]
            vv = lax.bitcast_convert_type(w & hi_mask, F32)
            yacc_ref[r:r + 1, c * LANES:(c + 1) * LANES] = jnp.sum(vv * coeff, axis=0, keepdims=True)

    def run_group(g, last):
        for r in range(group):
            b = r % GATHER_BUFS
            nb = (r + ahead) % GATHER_BUFS
            tok = g * group + r
            wait_rows(b)
            if not last or r + ahead < group:
                issue(slot, tok + ahead, nb)
            else:
                if r + ahead == group:
                    @pl.when(has_next)
                    def _():
                        ids_copy(i + 1, other).wait()

                @pl.when(has_next)
                def _():
                    issue(other, r + ahead - group, nb)
            compute(tok, r, b)
        y_ref[pl.ds(pl.multiple_of(g * group, group), group), :] = yacc_ref[...]

    def body(g, carry):
        run_group(g, last=False)
        return carry

    ngroups = tb // group
    lax.fori_loop(0, ngroups - 1, body, 0)
    run_group(ngroups - 1, last=True)
    x3 = x2_ref[...] + y_ref[...]
    out_ref[...] = _rms(x3, og_ref[...]) if final_norm else x3


def _peer_gather(ids_flat, table, x2, gates_t, ffn_gain, final_gain, final_norm):
    t, d = x2.shape
    tb = GATHER_TOKENS
    nsteps = t // tb
    full = lambda shape: pl.BlockSpec(shape, lambda i: (0, 0))
    return pl.pallas_call(
        functools.partial(_peer_gather_kernel, final_norm=final_norm),
        grid=(nsteps,),
        in_specs=[
            pl.BlockSpec(memory_space=pl.ANY),
            pl.BlockSpec(memory_space=pl.ANY),
            pl.BlockSpec((tb, d), lambda i: (i, 0)),
            pl.BlockSpec((PEER_SLOTS, tb), lambda i: (0, i)),
            full((1, d)),
            full((1, d)),
        ],
        out_specs=pl.BlockSpec((tb, d), lambda i: (i, 0)),
        out_shape=jax.ShapeDtypeStruct((t, d), F32),
        scratch_shapes=[
            pltpu.SMEM((2 * tb * PEER_SLOTS,), jnp.int32),
            pltpu.SemaphoreType.DMA((2,)),
            *[pltpu.VMEM((PEER_SLOTS, d), jnp.uint32) for _ in range(GATHER_BUFS)],
            pltpu.SemaphoreType.DMA((GATHER_BUFS,)),
            pltpu.VMEM((tb, d), F32),
            pltpu.VMEM((tb, d), F32),
            pltpu.VMEM((GATHER_GROUP, d), F32),
        ],
        compiler_params=_params("arbitrary"),
        name="peer_gather",
    )(ids_flat, table, x2, gates_t, ffn_gain.reshape(1, d), final_gain.reshape(1, d))


def _rope_angles(positions, dh):
    inv_freq = ROPE_THETA ** (-jnp.arange(0, dh, 2, dtype=F32) / dh)
    ang = positions.astype(F32).reshape(-1, 1) * inv_freq
    return jnp.cos(ang), jnp.sin(ang)


def _rope_tables_att(positions):
    cos, sin = _rope_angles(positions, HEAD_DIM)
    copies = LANES // HEAD_DIM
    return (jnp.tile(jnp.concatenate([cos, cos], axis=1), (1, copies)),
            jnp.tile(jnp.concatenate([-sin, sin], axis=1), (1, copies)))


def _pack_expert_table(u, v):
    ub = lax.bitcast_convert_type(u.astype(BF16), jnp.uint16).astype(jnp.uint32)
    vb = lax.bitcast_convert_type(v.astype(BF16), jnp.uint16).astype(jnp.uint32)
    return ((vb << 16) | ub).reshape(u.shape[0], 1, u.shape[1])


def kernel(x, mem, positions, mix_norm_gain, w_in, att_sinks, att_out_gain, ret_out_gain, w_out,
           cross_norm_gain, mem_norm_gain, w_xq, w_xk, w_xv, w_xo, ffn_norm_gain,
           w_peer_q, peer_sub_keys, peer_u, peer_v, final_norm_gain):
    batch, seq, d = x.shape
    mem_len = mem.shape[1]
    t = batch * seq
    depth = w_in.shape[0]
    assert d == D_MODEL and seq % BLOCK == 0 and t % GATHER_TOKENS == 0

    cos_a, sin_a = _rope_tables_att(positions)
    cos_r, sin_r = _rope_angles(positions, RET_HEAD_DIM)
    xf = x.reshape(t, d)
    memf = mem.reshape(batch * mem_len, d)
    for l in range(depth):
        h = _norm_matmul(xf, mix_norm_gain[l], _reorder_w_in(w_in[l]).astype(BF16), min(1024, t), 768, BF16)
        oa = _swa(h, cos_a, sin_a, att_sinks[l], att_out_gain[l], batch, seq)
        orr = _retention(h, cos_r, sin_r, ret_out_gain[l], batch, seq)
        x1, qx = _outproj(xf, oa, orr, w_out[l].astype(BF16), cross_norm_gain[l], w_xq[l].astype(BF16),
                          min(256, seq))
        w_kv = jnp.concatenate([w_xk[l], w_xv[l]], axis=1).astype(BF16)
        kv_mem = _norm_matmul(memf, mem_norm_gain[l], w_kv, mem_len, 2 * XATT_WIDTH, BF16)
        x2, pq = _xattn(qx, kv_mem, x1, w_xo[l].astype(BF16), ffn_norm_gain[l], w_peer_q[l].astype(BF16),
                        batch, seq, mem_len, min(256, seq))
        ids_t, gates_t = _peer_topk(pq, peer_sub_keys[l].astype(BF16), LANES, PEER_HEADS)
        ids = ids_t.reshape(PEER_SLOTS, t).T.reshape(t * PEER_SLOTS)
        table = _pack_expert_table(peer_u[l], peer_v[l])
        xf = _peer_gather(ids, table, x2, gates_t.reshape(PEER_SLOTS, t), ffn_norm_gain[l], final_norm_gain,
                          final_norm=(l == depth - 1))
    return xf.reshape(batch, seq, d)
```

```python
import functools
import math

import jax
import jax.numpy as jnp
from jax import lax
from jax.experimental import pallas as pl
from jax.experimental.pallas import tpu as pltpu
from jax.experimental.pallas import tpu_sc as plsc

F32 = jnp.float32
BF16 = jnp.bfloat16

EPS = 1e-6
GN_EPS = 1e-5
ROPE_THETA = 10000.0
D_MODEL = 2048
HEAD_DIM = 64
ATT_Q_HEADS = 16
ATT_KV_HEADS = 2
ATT_WIDTH = ATT_Q_HEADS * HEAD_DIM
ATT_KV_WIDTH = ATT_KV_HEADS * HEAD_DIM
BLOCK = 128
RET_HEADS = 4
RET_HEAD_DIM = 256
RET_WIDTH = RET_HEADS * RET_HEAD_DIM
RET_CHUNK = 128
IN_COLS = ATT_WIDTH + 2 * ATT_KV_WIDTH + 4 * RET_WIDTH
XATT_HEADS = 4
XATT_HEAD_DIM = 128
XATT_WIDTH = XATT_HEADS * XATT_HEAD_DIM
PEER_HEADS = 8
PEER_N_KEYS = 128
PEER_HALF_DIM = 128
PEER_TOPK = 16
PEER_SLOTS = PEER_HEADS * PEER_TOPK

LANES = 128
SUBLANES = 8
VMEM_LIMIT = 56 * 1024 * 1024

_REF_QR0 = ATT_WIDTH + 2 * ATT_KV_WIDTH
_QR0 = 0
_KR0 = _QR0 + RET_WIDTH
_VR0 = _KR0 + RET_WIDTH
_GR0 = _VR0 + RET_WIDTH
_QA0 = _GR0 + RET_WIDTH
_KA0 = _QA0 + ATT_WIDTH
_VA0 = _KA0 + ATT_KV_WIDTH


def _reorder_w_in(w):
    return jnp.concatenate([w[:, _REF_QR0:], w[:, :_REF_QR0]], axis=1)


def _params(*sem):
    return pltpu.CompilerParams(dimension_semantics=sem, vmem_limit_bytes=VMEM_LIMIT)


def _rms(xf, gain):
    ms = jnp.mean(xf * xf, axis=-1, keepdims=True)
    return xf * lax.rsqrt(ms + EPS) * gain


def _dot(a, b):
    return jnp.dot(a, b, preferred_element_type=F32)


def _dot_nt(a, b):
    return lax.dot_general(a, b, (((1,), (1,)), ((), ())), preferred_element_type=F32)


def _norm_matmul_kernel(x_ref, g_ref, w_ref, o_ref, xn_ref):
    @pl.when(pl.program_id(1) == 0)
    def _():
        xn_ref[...] = _rms(x_ref[...], g_ref[...]).astype(BF16)

    o_ref[...] = _dot(xn_ref[...], w_ref[...]).astype(o_ref.dtype)


def _norm_matmul(x, gain, w_bf16, tm, tn, out_dtype):
    t, d = x.shape
    n = w_bf16.shape[1]
    return pl.pallas_call(
        _norm_matmul_kernel,
        grid=(t // tm, n // tn),
        in_specs=[
            pl.BlockSpec((tm, d), lambda i, j: (i, 0)),
            pl.BlockSpec((1, d), lambda i, j: (0, 0)),
            pl.BlockSpec((d, tn), lambda i, j: (0, j)),
        ],
        out_specs=pl.BlockSpec((tm, tn), lambda i, j: (i, j)),
        out_shape=jax.ShapeDtypeStruct((t, n), out_dtype),
        scratch_shapes=[pltpu.VMEM((tm, d), BF16)],
        compiler_params=_params("parallel", "arbitrary"),
        name="norm_matmul",
    )(x, gain.reshape(1, d), w_bf16)


def _swa_kernel(sink_ref, q_ref, kc_ref, kp_ref, vc_ref, vp_ref, cc_ref, sc_ref, cp_ref, sp_ref,
                gain_ref, o_ref):
    n = pl.program_id(1)
    lane = lax.broadcasted_iota(jnp.int32, (1, LANES), 1)
    first_half = (lane % HEAD_DIM) < (HEAD_DIM // 2)
    lo = lane < HEAD_DIM

    def rope(x, c, s):
        partner = jnp.where(first_half, pltpu.roll(x, LANES - HEAD_DIM // 2, 1),
                            pltpu.roll(x, HEAD_DIM // 2, 1))
        return x * c + partner * s

    cc, sc = cc_ref[...], sc_ref[...]
    k = jnp.concatenate([rope(kp_ref[...].astype(F32), cp_ref[...], sp_ref[...]),
                         rope(kc_ref[...].astype(F32), cc, sc)], axis=0)
    v = jnp.concatenate([vp_ref[...].astype(F32), vc_ref[...].astype(F32)], axis=0)
    k_sw = pltpu.roll(k, HEAD_DIM, 1)
    v_sw = pltpu.roll(v, HEAD_DIM, 1)

    def place(a, a_sw, c, half):
        src = a if c == half else a_sw
        keep = lo if half == 0 else jnp.logical_not(lo)
        return jnp.where(keep, src, 0.0).astype(BF16)

    kvar = {(c, h): place(k, k_sw, c, h) for c in range(ATT_KV_HEADS) for h in range(2)}
    vvar = {(c, h): place(v, v_sw, c, h) for c in range(ATT_KV_HEADS) for h in range(2)}

    qi = lax.broadcasted_iota(jnp.int32, (BLOCK, 2 * BLOCK), 0) + BLOCK
    ki = lax.broadcasted_iota(jnp.int32, (BLOCK, 2 * BLOCK), 1)
    dist = qi - ki
    kmin = jnp.where(n > 0, 0, BLOCK)
    valid = (dist >= 0) & (dist < BLOCK) & (ki >= kmin)

    group = ATT_Q_HEADS // ATT_KV_HEADS
    for j in range(ATT_WIDTH // LANES):
        c = (2 * j) // group
        cols = slice(j * LANES, (j + 1) * LANES)
        qg = (rope(q_ref[:, cols].astype(F32), cc, sc) * (HEAD_DIM ** -0.5)).astype(BF16)
        out = jnp.zeros((BLOCK, LANES), F32)
        for half in range(2):
            s = jnp.where(valid, _dot_nt(qg, kvar[(c, half)]), -jnp.inf)
            sink = sink_ref[2 * j + half]
            m = jnp.maximum(jnp.max(s, axis=-1, keepdims=True), sink)
            p = jnp.exp(s - m)
            denom = jnp.sum(p, axis=-1, keepdims=True) + jnp.exp(sink - m)
            out = out + _dot((p / denom).astype(BF16), vvar[(c, half)])
        sq = out * out
        ss_lo = jnp.sum(jnp.where(lo, sq, 0.0), axis=-1, keepdims=True)
        ss_hi = jnp.sum(jnp.where(lo, 0.0, sq), axis=-1, keepdims=True)
        ms = jnp.where(lo, ss_lo, ss_hi) * (1.0 / HEAD_DIM)
        o_ref[:, cols] = (out * lax.rsqrt(ms + EPS) * gain_ref[:, cols]).astype(o_ref.dtype)


def _swa(h, cos_a, sin_a, sinks, gain, batch, seq):
    nb = seq // BLOCK
    t = batch * seq
    kcol, vcol = _KA0 // LANES, _VA0 // LANES
    cur = lambda b, n: b * nb + n
    prev = lambda b, n: b * nb + jnp.maximum(n - 1, 0)
    row = lambda col, f: pl.BlockSpec((BLOCK, LANES), lambda b, n: (f(b, n), col))
    return pl.pallas_call(
        _swa_kernel,
        grid=(batch, nb),
        in_specs=[
            pl.BlockSpec(memory_space=pltpu.SMEM),
            pl.BlockSpec((BLOCK, ATT_WIDTH), lambda b, n: (cur(b, n), _QA0 // ATT_WIDTH)),
            row(kcol, cur), row(kcol, prev), row(vcol, cur), row(vcol, prev),
            row(0, cur), row(0, cur), row(0, prev), row(0, prev),
            pl.BlockSpec((1, ATT_WIDTH), lambda b, n: (0, 0)),
        ],
        out_specs=pl.BlockSpec((BLOCK, ATT_WIDTH), lambda b, n: (cur(b, n), 0)),
        out_shape=jax.ShapeDtypeStruct((t, ATT_WIDTH), BF16),
        compiler_params=_params("parallel", "parallel"),
        name="swa_attention",
    )(sinks, h, h, h, h, h, cos_a, sin_a, cos_a, sin_a, gain.reshape(1, ATT_WIDTH))


def _ret_kernel(lg_ref, cd_ref, q_ref, k_ref, v_ref, g_ref, c_ref, s_ref, gain_ref, o_ref, state_ref):
    n = pl.program_id(1)

    @pl.when(n == 0)
    def _():
        state_ref[...] = jnp.zeros_like(state_ref)

    c, s = c_ref[...], s_ref[...]
    hd = RET_HEAD_DIM
    half = hd // 2

    def rope(x):
        x1, x2 = x[:, :half], x[:, half:]
        return jnp.concatenate([x1 * c - x2 * s, x2 * c + x1 * s], axis=1)

    ri = lax.broadcasted_iota(jnp.int32, (RET_CHUNK, RET_CHUNK), 0).astype(F32)
    ci = lax.broadcasted_iota(jnp.int32, (RET_CHUNK, RET_CHUNK), 1).astype(F32)
    diff = ri - ci
    for hh in range(RET_HEADS):
        cols = slice(hh * hd, (hh + 1) * hd)
        lg = lg_ref[hh]
        q = rope(q_ref[:, cols].astype(F32))
        k = rope(k_ref[:, cols].astype(F32)) * (hd ** -0.5)
        v = v_ref[:, cols].astype(BF16)

        decay = jnp.where(diff >= 0, jnp.exp(jnp.maximum(diff, 0.0) * lg), 0.0)
        zeta = jnp.exp((RET_CHUNK - 1 - ri) * lg)
        xi = jnp.exp((ri + 1.0) * lg)
        zeta2 = jnp.concatenate([zeta, zeta], axis=1)
        xi2 = jnp.concatenate([xi, xi], axis=1)

        inner = _dot_nt(q.astype(BF16), k.astype(BF16)) * decay
        state = state_ref[hh]
        out = _dot(inner.astype(BF16), v) + _dot((q * xi2).astype(BF16), state.astype(BF16))
        kv = lax.dot_general((k * zeta2).astype(BF16), v, (((0,), (0,)), ((), ())), preferred_element_type=F32)
        state_ref[hh] = state * cd_ref[hh] + kv

        mu = jnp.mean(out, axis=-1, keepdims=True)
        cen = out - mu
        var = jnp.mean(cen * cen, axis=-1, keepdims=True)
        g = g_ref[:, cols].astype(F32)
        o = cen * lax.rsqrt(var + GN_EPS) * gain_ref[:, cols] * (g * jax.nn.sigmoid(g))
        o_ref[:, cols] = o.astype(o_ref.dtype)


def _retention(h, cos_r, sin_r, gain, batch, seq):
    nc = seq // RET_CHUNK
    t = batch * seq
    hd = RET_HEAD_DIM
    log_gamma = jnp.log1p(-jnp.exp2(-5.0 - jnp.arange(RET_HEADS, dtype=F32)))
    chunk_decay = jnp.exp(RET_CHUNK * log_gamma)
    col = lambda c0: pl.BlockSpec((RET_CHUNK, RET_WIDTH), lambda b, n: (b * nc + n, c0 // RET_WIDTH))
    tab = pl.BlockSpec((RET_CHUNK, hd // 2), lambda b, n: (b * nc + n, 0))
    return pl.pallas_call(
        _ret_kernel,
        grid=(batch, nc),
        in_specs=[
            pl.BlockSpec(memory_space=pltpu.SMEM),
            pl.BlockSpec(memory_space=pltpu.SMEM),
            col(_QR0), col(_KR0), col(_VR0), col(_GR0), tab, tab,
            pl.BlockSpec((1, RET_WIDTH), lambda b, n: (0, 0)),
        ],
        out_specs=pl.BlockSpec((RET_CHUNK, RET_WIDTH), lambda b, n: (b * nc + n, 0)),
        out_shape=jax.ShapeDtypeStruct((t, RET_WIDTH), BF16),
        scratch_shapes=[pltpu.VMEM((RET_HEADS, hd, hd), F32)],
        compiler_params=_params("parallel", "arbitrary"),
        name="retention",
    )(log_gamma, chunk_decay, h, h, h, h, cos_r, sin_r, gain.reshape(1, RET_WIDTH))


def _outproj_kernel(x_ref, oa_ref, or_ref, wa_ref, wr_ref, g_ref, wq_ref, x1_ref, qx_ref):
    x1 = x_ref[...] + _dot(oa_ref[...], wa_ref[...]) + _dot(or_ref[...], wr_ref[...])
    x1_ref[...] = x1
    qx_ref[...] = _dot(_rms(x1, g_ref[...]).astype(BF16), wq_ref[...]).astype(qx_ref.dtype)


def _outproj(x, oa, orr, w_out_bf16, gain, w_xq_bf16, tm):
    t, d = x.shape
    full = lambda shape: pl.BlockSpec(shape, lambda i: (0, 0))
    return pl.pallas_call(
        _outproj_kernel,
        grid=(t // tm,),
        in_specs=[
            pl.BlockSpec((tm, d), lambda i: (i, 0)),
            pl.BlockSpec((tm, ATT_WIDTH), lambda i: (i, 0)),
            pl.BlockSpec((tm, RET_WIDTH), lambda i: (i, 0)),
            pl.BlockSpec((ATT_WIDTH, d), lambda i: (0, 0)),
            pl.BlockSpec((RET_WIDTH, d), lambda i: (1, 0)),
            full((1, d)),
            full((d, XATT_WIDTH)),
        ],
        out_specs=[pl.BlockSpec((tm, d), lambda i: (i, 0)), pl.BlockSpec((tm, XATT_WIDTH), lambda i: (i, 0))],
        out_shape=[jax.ShapeDtypeStruct((t, d), F32), jax.ShapeDtypeStruct((t, XATT_WIDTH), BF16)],
        compiler_params=_params("parallel"),
        name="out_proj",
    )(x, oa, orr, w_out_bf16, w_out_bf16, gain.reshape(1, d), w_xq_bf16)


def _xattn_kernel(qx_ref, k_ref, v_ref, x1_ref, wo_ref, g_ref, wpq_ref, x2_ref, pq_ref):
    heads = []
    for hh in range(XATT_HEADS):
        cols = slice(hh * XATT_HEAD_DIM, (hh + 1) * XATT_HEAD_DIM)
        s = _dot_nt(qx_ref[:, cols], k_ref[:, cols]) * (XATT_HEAD_DIM ** -0.5)
        p = jnp.exp(s - jnp.max(s, axis=-1, keepdims=True))
        p = p / jnp.sum(p, axis=-1, keepdims=True)
        heads.append(_dot(p.astype(BF16), v_ref[:, cols]))
    o = jnp.concatenate(heads, axis=1).astype(BF16)
    x2 = x1_ref[...] + _dot(o, wo_ref[...])
    x2_ref[...] = x2
    pq_ref[...] = _dot(_rms(x2, g_ref[...]).astype(BF16), wpq_ref[...])


def _xattn(qx, kv_mem, x1, w_xo_bf16, gain, w_pq_bf16, batch, seq, mem_len, tm):
    t, d = x1.shape
    npq = w_pq_bf16.shape[1]
    nt = seq // tm
    rows = lambda width: pl.BlockSpec((tm, width), lambda b, i: (b * nt + i, 0))
    full = lambda shape: pl.BlockSpec(shape, lambda b, i: (0, 0))
    return pl.pallas_call(
        _xattn_kernel,
        grid=(batch, nt),
        in_specs=[
            rows(XATT_WIDTH),
            pl.BlockSpec((mem_len, XATT_WIDTH), lambda b, i: (b, 0)),
            pl.BlockSpec((mem_len, XATT_WIDTH), lambda b, i: (b, 1)),
            rows(d),
            full((XATT_WIDTH, d)),
            full((1, d)),
            full((d, npq)),
        ],
        out_specs=[rows(d), rows(npq)],
        out_shape=[jax.ShapeDtypeStruct((t, d), F32), jax.ShapeDtypeStruct((t, npq), F32)],
        compiler_params=_params("parallel", "parallel"),
        name="cross_attention",
    )(qx, kv_mem, kv_mem, x1, w_xo_bf16, gain.reshape(1, d), w_pq_bf16)


def _topk_rows(s, k, payload=None):
    nrows = s.shape[0]
    rows = lax.broadcasted_iota(jnp.int32, s.shape, 0).astype(F32)
    vals, sel = [], []
    for _ in range(k):
        m = jnp.max(s, axis=0, keepdims=True)
        am = jnp.min(jnp.where(s == m, rows, float(nrows)), axis=0, keepdims=True)
        hit = rows == am
        vals.append(m)
        sel.append(am if payload is None else jnp.max(jnp.where(hit, payload, -1.0), axis=0, keepdims=True))
        s = jnp.where(hit, -jnp.inf, s)
    return jnp.concatenate(vals, axis=0), jnp.concatenate(sel, axis=0)


def _staircase(t1, t2):
    kk, half = PEER_TOPK, SUBLANES
    blocks = [(t1[0:1, :], t2, kk), (t1[1:2, :], t2[0:half, :], half)]
    blocks += [(t1[a:a + 1, :], t2[0:half, :], kk // (a + 1)) for a in range(2, half)]
    blocks.append((t1[half:kk, :], t2[0:1, :], half))
    return blocks


def _peer_topk_kernel(q_ref, keys_ref, ids_ref, gates_ref, *, heads):
    width = 2 * PEER_HALF_DIM
    row8 = lax.broadcasted_iota(jnp.int32, (SUBLANES, q_ref.shape[0]), 0)
    for hh in range(heads):
        tops = []
        for p in range(2):
            c0 = hh * width + p * PEER_HALF_DIM
            qp = q_ref[:, c0:c0 + PEER_HALF_DIM].astype(BF16)
            tops.append(_topk_rows(_dot_nt(keys_ref[hh, p], qp), PEER_TOPK))
        (s1, i1), (s2, i2) = tops
        cand_s, cand_i = [], []
        for (a_s, b_s, live), (a_i, b_i, _) in zip(_staircase(s1, s2), _staircase(i1 * float(PEER_N_KEYS), i2)):
            blk = a_s + b_s
            if live < blk.shape[0]:
                blk = jnp.where(row8 < live, blk, -jnp.inf)
            cand_s.append(blk)
            cand_i.append(a_i + b_i)
        top_s, top_e = _topk_rows(jnp.concatenate(cand_s, axis=0), PEER_TOPK,
                                  payload=jnp.concatenate(cand_i, axis=0))
        e = jnp.exp(top_s - top_s[0:1, :])
        ids_ref[hh] = top_e.astype(jnp.int32)
        gates_ref[hh] = e / jnp.sum(e, axis=0, keepdims=True)


def _peer_topk(pq, keys_bf16, tt, heads):
    t = pq.shape[0]
    width = 2 * PEER_HALF_DIM
    out = pl.BlockSpec((heads, PEER_TOPK, tt), lambda i, hh: (hh, 0, i))
    return pl.pallas_call(
        functools.partial(_peer_topk_kernel, heads=heads),
        grid=(t // tt, PEER_HEADS // heads),
        in_specs=[
            pl.BlockSpec((tt, heads * width), lambda i, hh: (i, hh)),
            pl.BlockSpec((heads, 2, PEER_N_KEYS, PEER_HALF_DIM), lambda i, hh: (hh, 0, 0, 0)),
        ],
        out_specs=[out, out],
        out_shape=[jax.ShapeDtypeStruct((PEER_HEADS, PEER_TOPK, t), jnp.int32),
                   jax.ShapeDtypeStruct((PEER_HEADS, PEER_TOPK, t), F32)],
        compiler_params=_params("parallel", "parallel"),
        name="peer_topk",
    )(pq, keys_bf16)


GATHER_TOKENS = 128
GATHER_BUFS = 8
GATHER_GROUP = 8
assert GATHER_GROUP % GATHER_BUFS == 0 and GATHER_GROUP % SUBLANES == 0 and GATHER_TOKENS % GATHER_GROUP == 0


def _peer_gather_kernel(ids_hbm, tbl_hbm, x2_ref, gates_ref, fg_ref, og_ref, out_ref,
                        ids_smem, ids_sem, *scratch, final_norm, first_block):
    gbufs = scratch[:GATHER_BUFS]
    gsem, xn_ref, y_ref, yacc_ref = scratch[GATHER_BUFS:]
    i = pl.program_id(0)
    has_next = i + 1 < pl.num_programs(0)
    slot = i % 2
    other = 1 - slot
    tb = GATHER_TOKENS
    nchunk = D_MODEL // LANES
    nids = tb * PEER_SLOTS
    ahead = GATHER_BUFS - 1
    group = GATHER_GROUP

    def ids_copy(step, sl):
        return pltpu.make_async_copy(ids_hbm.at[pl.ds(pl.multiple_of((first_block + step) * nids, nids), nids)],
                                     ids_smem.at[pl.ds(pl.multiple_of(sl * nids, nids), nids)], ids_sem.at[sl])

    def row_copy(e, b, k):
        return pltpu.make_async_copy(tbl_hbm.at[e], gbufs[b].at[pl.ds(k, 1)], gsem.at[b])

    def issue(sl, tok, b):
        base = sl * nids + tok * PEER_SLOTS
        for k in range(PEER_SLOTS):
            row_copy(ids_smem[base + k], b, k).start(priority=k % 2)

    def wait_rows(b):
        for k in range(PEER_SLOTS):
            row_copy(0, b, k).wait()

    @pl.when(i == 0)
    def _():
        ids_copy(0, 0).start()
        ids_copy(0, 0).wait()
        for tok in range(ahead):
            issue(0, tok, tok)

    @pl.when(has_next)
    def _():
        ids_copy(i + 1, other).start()

    xn_ref[...] = _rms(x2_ref[...], fg_ref[...])

    lane = lax.broadcasted_iota(jnp.int32, (PEER_SLOTS, LANES), 1)
    hi_mask = jnp.uint32(0xFFFF0000)

    def compute(tok, r, b):
        gbuf = gbufs[b]
        xrow = xn_ref[pl.ds(tok, 1), :]
        acc = jnp.zeros((PEER_SLOTS, LANES), F32)
        for c in range(nchunk):
            w = gbuf[:, c * LANES:(c + 1) * LANES]
            u = lax.bitcast_convert_type(w << 16, F32)
            acc = acc + u * xrow[:, c * LANES:(c + 1) * LANES]
        a = jnp.sum(acc, axis=-1, keepdims=True)
        gate = jnp.sum(jnp.where(lane == tok, gates_ref[...], 0.0), axis=-1, keepdims=True)
        coeff = 0.5 * a * (1.0 + lax.erf(a * (2.0 ** -0.5))) * gate
        for c in range(nchunk):
            w = gbuf[:, c * LANES:(c + 1) * LANES]
            vv = lax.bitcast_convert_type(w & hi_mask, F32)
            yacc_ref[r:r + 1, c * LANES:(c + 1) * LANES] = jnp.sum(vv * coeff, axis=0, keepdims=True)

    def run_group(g, last):
        for r in range(group):
            b = r % GATHER_BUFS
            nb = (r + ahead) % GATHER_BUFS
            tok = g * group + r
            wait_rows(b)
            if not last or r + ahead < group:
                issue(slot, tok + ahead, nb)
            else:
                if r + ahead == group:
                    @pl.when(has_next)
                    def _():
                        ids_copy(i + 1, other).wait()

                @pl.when(has_next)
                def _():
                    issue(other, r + ahead - group, nb)
            compute(tok, r, b)
        y_ref[pl.ds(pl.multiple_of(g * group, group), group), :] = yacc_ref[...]

    def body(g, carry):
        run_group(g, last=False)
        return carry

    ngroups = tb // group
    lax.fori_loop(0, ngroups - 1, body, 0)
    run_group(ngroups - 1, last=True)
    x3 = x2_ref[...] + y_ref[...]
    out_ref[...] = _rms(x3, og_ref[...]) if final_norm else x3


def _peer_gather(ids_flat, table, x2, gates_t, ffn_gain, final_gain, final_norm, first_block=0):
    t, d = x2.shape
    tb = GATHER_TOKENS
    nsteps = t // tb - first_block
    full = lambda shape: pl.BlockSpec(shape, lambda i: (0, 0))
    return pl.pallas_call(
        functools.partial(_peer_gather_kernel, final_norm=final_norm, first_block=first_block),
        grid=(nsteps,),
        in_specs=[
            pl.BlockSpec(memory_space=pl.ANY),
            pl.BlockSpec(memory_space=pl.ANY),
            pl.BlockSpec((tb, d), lambda i: (i + first_block, 0)),
            pl.BlockSpec((PEER_SLOTS, tb), lambda i: (0, i + first_block)),
            full((1, d)),
            full((1, d)),
        ],
        out_specs=pl.BlockSpec((tb, d), lambda i: (i + first_block, 0)),
        out_shape=jax.ShapeDtypeStruct((t, d), F32),
        scratch_shapes=[
            pltpu.SMEM((2 * tb * PEER_SLOTS,), jnp.int32),
            pltpu.SemaphoreType.DMA((2,)),
            *[pltpu.VMEM((PEER_SLOTS, d), jnp.uint32) for _ in range(GATHER_BUFS)],
            pltpu.SemaphoreType.DMA((GATHER_BUFS,)),
            pltpu.VMEM((tb, d), F32),
            pltpu.VMEM((tb, d), F32),
            pltpu.VMEM((GATHER_GROUP, d), F32),
        ],
        compiler_params=_params("arbitrary"),
        name="peer_gather",
    )(ids_flat, table, x2, gates_t, ffn_gain.reshape(1, d), final_gain.reshape(1, d))


SC_CORES, SC_SUBCORES, SC_LANES = 2, 16, 16
SC_WORKERS = SC_CORES * SC_SUBCORES
SC_ROWS = SC_LANES
SC_COLS = 8
SC_SHARE = (9, 32)


def _sc_erf(x):
    ax = jnp.abs(x)
    t = 1.0 / (1.0 + 0.3275911 * ax)
    poly = t * (0.254829592 + t * (-0.284496736 + t * (1.421413741 + t * (-1.453152027 + t * 1.061405429))))
    y = 1.0 - poly * jnp.exp(-ax * ax)
    return jnp.where(x < 0, -y, y)


def _peer_sc(ids_flat, gates_tok, xn, table, ts):
    d = D_MODEL
    lanes, rows = SC_LANES, SC_ROWS
    nchunk = PEER_SLOTS // rows
    block = lanes * SC_COLS
    per_worker = ts // SC_WORKERS
    assert ts % SC_WORKERS == 0 and d % block == 0
    mesh = plsc.VectorSubcoreMesh(core_axis_name="c", subcore_axis_name="s")

    @functools.partial(
        pl.kernel, mesh=mesh,
        out_type=jax.ShapeDtypeStruct((ts, d), F32),
        scratch_types=[
            pltpu.VMEM((PEER_SLOTS,), jnp.int32),
            pltpu.VMEM((PEER_SLOTS,), F32),
            pltpu.VMEM((d,), F32),
            pltpu.VMEM((d,), F32),
            pltpu.VMEM((2, rows, 1, d), jnp.uint32),
            pltpu.VMEM((rows, lanes), F32),
            pltpu.VMEM((lanes,), F32),
            pltpu.VMEM((rows, lanes), jnp.int32),
            pltpu.SemaphoreType.DMA((2,)),
        ],
        compiler_params=pltpu.CompilerParams(needs_layout_passes=False),
        name="peer_sparsecore",
    )
    def run(ids_hbm, gates_hbm, xn_hbm, tbl_hbm, y_hbm, ids_v, gates_v, x_v, y_v, rows_v, acc_v, coeff_v, ridx_v, sem):
        worker = lax.axis_index("s") * SC_CORES + lax.axis_index("c")
        lane = lax.iota(jnp.int32, lanes)
        zero = jnp.zeros((lanes,), F32)
        hi_mask = jnp.full((lanes,), 0xFFFF0000, jnp.uint32)
        for r in range(rows):
            ridx_v[r, :] = jnp.full((lanes,), r, jnp.int32)

        def gather(c, slot):
            return pltpu.make_async_copy(tbl_hbm.at[ids_v[pl.ds(c * rows, rows)]], rows_v.at[slot], sem.at[slot])

        @pl.loop(0, per_worker)
        def _(i):
            tok = worker * per_worker + i
            pltpu.sync_copy(ids_hbm.at[pl.ds(tok * PEER_SLOTS, PEER_SLOTS)], ids_v)
            pltpu.sync_copy(gates_hbm.at[pl.ds(tok * PEER_SLOTS, PEER_SLOTS)], gates_v)
            pltpu.sync_copy(xn_hbm.at[tok], x_v)

            @pl.loop(0, d // lanes)
            def _(j):
                y_v[pl.ds(j * lanes, lanes)] = zero

            gather(0, 0).start()
            for c in range(nchunk):
                slot = c % 2
                if c + 1 < nchunk:
                    gather(c + 1, 1 - slot).start()
                gather(c, slot).wait()
                for r0 in range(0, rows, 4):
                    def dot_body(j, accs, r0=r0, slot=slot):
                        xj = x_v[pl.ds(j * lanes, lanes)]
                        return tuple(
                            accs[q] + plsc.bitcast(rows_v[slot, r0 + q, 0, pl.ds(j * lanes, lanes)] << 16, F32) * xj
                            for q in range(4))
                    accs = lax.fori_loop(0, d // lanes, dot_body, (zero,) * 4)
                    for q in range(4):
                        acc_v[r0 + q, :] = accs[q]
                a = zero
                for col in range(lanes):
                    a = a + plsc.load_gather(acc_v, [lane, ridx_v[col, :]])
                gate = gates_v[pl.ds(c * rows, rows)]
                coeff_v[...] = 0.5 * a * (1.0 + _sc_erf(a * (2.0 ** -0.5))) * gate
                coeffs = [plsc.load_gather(coeff_v, [ridx_v[r, :]]) for r in range(rows)]

                @pl.loop(0, d // block)
                def _(jb, slot=slot, coeffs=coeffs):
                    base = jb * block
                    accs = [zero] * SC_COLS
                    for r in range(rows):
                        for q in range(SC_COLS):
                            w = rows_v[slot, r, 0, pl.ds(base + q * lanes, lanes)]
                            accs[q] = accs[q] + plsc.bitcast(w & hi_mask, F32) * coeffs[r]
                    for q in range(SC_COLS):
                        plsc.addupdate(y_v.at[pl.ds(base + q * lanes, lanes)], accs[q])

            pltpu.sync_copy(y_v, y_hbm.at[tok])

    return run(ids_flat, gates_tok, xn, table)


def _rms_rows_kernel(x_ref, g_ref, o_ref):
    o_ref[...] = _rms(x_ref[...], g_ref[...])


def _rms_rows(x, gain, rows, tm):
    d = x.shape[1]
    return pl.pallas_call(
        _rms_rows_kernel,
        grid=(rows // tm,),
        in_specs=[pl.BlockSpec((tm, d), lambda i: (i, 0)), pl.BlockSpec((1, d), lambda i: (0, 0))],
        out_specs=pl.BlockSpec((tm, d), lambda i: (i, 0)),
        out_shape=jax.ShapeDtypeStruct((rows, d), F32),
        compiler_params=_params("parallel"),
        name="rms_rows",
    )(x, gain.reshape(1, d))


def _peer_finish_kernel(x2_ref, y_ref, og_ref, prev_ref, out_ref, *, final_norm):
    del prev_ref
    x3 = x2_ref[...] + y_ref[...]
    out_ref[...] = _rms(x3, og_ref[...]) if final_norm else x3


def _peer_finish(x2, y_head, out_rest, final_gain, final_norm):
    rows, d = y_head.shape
    tm = GATHER_TOKENS
    blk = pl.BlockSpec((tm, d), lambda i: (i, 0))
    return pl.pallas_call(
        functools.partial(_peer_finish_kernel, final_norm=final_norm),
        grid=(rows // tm,),
        in_specs=[blk, blk, pl.BlockSpec((1, d), lambda i: (0, 0)), pl.BlockSpec(memory_space=pl.ANY)],
        out_specs=blk,
        out_shape=jax.ShapeDtypeStruct(out_rest.shape, F32),
        input_output_aliases={3: 0},
        compiler_params=_params("parallel"),
        name="peer_finish",
    )(x2, y_head, final_gain.reshape(1, d), out_rest)


def _rope_angles(positions, dh):
    inv_freq = ROPE_THETA ** (-jnp.arange(0, dh, 2, dtype=F32) / dh)
    ang = positions.astype(F32).reshape(-1, 1) * inv_freq
    return jnp.cos(ang), jnp.sin(ang)


def _rope_tables_att(positions):
    cos, sin = _rope_angles(positions, HEAD_DIM)
    copies = LANES // HEAD_DIM
    return (jnp.tile(jnp.concatenate([cos, cos], axis=1), (1, copies)),
            jnp.tile(jnp.concatenate([-sin, sin], axis=1), (1, copies)))


def _pack_expert_table(u, v):
    ub = lax.bitcast_convert_type(u.astype(BF16), jnp.uint16).astype(jnp.uint32)
    vb = lax.bitcast_convert_type(v.astype(BF16), jnp.uint16).astype(jnp.uint32)
    return ((vb << 16) | ub).reshape(u.shape[0], 1, u.shape[1])


def kernel(x, mem, positions, mix_norm_gain, w_in, att_sinks, att_out_gain, ret_out_gain, w_out,
           cross_norm_gain, mem_norm_gain, w_xq, w_xk, w_xv, w_xo, ffn_norm_gain,
           w_peer_q, peer_sub_keys, peer_u, peer_v, final_norm_gain):
    batch, seq, d = x.shape
    mem_len = mem.shape[1]
    t = batch * seq
    depth = w_in.shape[0]
    assert d == D_MODEL and seq % BLOCK == 0 and t % GATHER_TOKENS == 0

    cos_a, sin_a = _rope_tables_att(positions)
    cos_r, sin_r = _rope_angles(positions, RET_HEAD_DIM)
    xf = x.reshape(t, d)
    memf = mem.reshape(batch * mem_len, d)
    for l in range(depth):
        h = _norm_matmul(xf, mix_norm_gain[l], _reorder_w_in(w_in[l]).astype(BF16), min(1024, t), 768, BF16)
        oa = _swa(h, cos_a, sin_a, att_sinks[l], att_out_gain[l], batch, seq)
        orr = _retention(h, cos_r, sin_r, ret_out_gain[l], batch, seq)
        x1, qx = _outproj(xf, oa, orr, w_out[l].astype(BF16), cross_norm_gain[l], w_xq[l].astype(BF16),
                          min(256, seq))
        w_kv = jnp.concatenate([w_xk[l], w_xv[l]], axis=1).astype(BF16)
        kv_mem = _norm_matmul(memf, mem_norm_gain[l], w_kv, mem_len, 2 * XATT_WIDTH, BF16)
        x2, pq = _xattn(qx, kv_mem, x1, w_xo[l].astype(BF16), ffn_norm_gain[l], w_peer_q[l].astype(BF16),
                        batch, seq, mem_len, min(256, seq))
        ids_t, gates_t = _peer_topk(pq, peer_sub_keys[l].astype(BF16), LANES, PEER_HEADS)
        ids = ids_t.reshape(PEER_SLOTS, t).T.reshape(t * PEER_SLOTS)
        table = _pack_expert_table(peer_u[l], peer_v[l])
        gates_t = gates_t.reshape(PEER_SLOTS, t)
        last = l == depth - 1
        nblocks = t // GATHER_TOKENS
        sc_blocks = nblocks * SC_SHARE[0] // SC_SHARE[1] if nblocks % SC_SHARE[1] == 0 else 0
        ts = sc_blocks * GATHER_TOKENS
        if sc_blocks:
            xn_head = _rms_rows(x2, ffn_norm_gain[l], ts, GATHER_TOKENS)
            y_head = _peer_sc(ids, gates_t[:, :ts].T.reshape(ts * PEER_SLOTS), xn_head, table, ts)
        xf = _peer_gather(ids, table, x2, gates_t, ffn_norm_gain[l], final_norm_gain, last, sc_blocks)
        if sc_blocks:
            xf = _peer_finish(x2, y_head, xf, final_norm_gain, last)
    return xf.reshape(batch, seq, d)
```

```python
import functools
import math

import jax
import jax.numpy as jnp
from jax import lax
from jax.experimental import pallas as pl
from jax.experimental.pallas import tpu as pltpu
from jax.experimental.pallas import tpu_sc as plsc

F32 = jnp.float32
BF16 = jnp.bfloat16

EPS = 1e-6
GN_EPS = 1e-5
ROPE_THETA = 10000.0
D_MODEL = 2048
HEAD_DIM = 64
ATT_Q_HEADS = 16
ATT_KV_HEADS = 2
ATT_WIDTH = ATT_Q_HEADS * HEAD_DIM
ATT_KV_WIDTH = ATT_KV_HEADS * HEAD_DIM
BLOCK = 128
RET_HEADS = 4
RET_HEAD_DIM = 256
RET_WIDTH = RET_HEADS * RET_HEAD_DIM
RET_CHUNK = 128
IN_COLS = ATT_WIDTH + 2 * ATT_KV_WIDTH + 4 * RET_WIDTH
XATT_HEADS = 4
XATT_HEAD_DIM = 128
XATT_WIDTH = XATT_HEADS * XATT_HEAD_DIM
PEER_HEADS = 8
PEER_N_KEYS = 128
PEER_HALF_DIM = 128
PEER_TOPK = 16
PEER_SLOTS = PEER_HEADS * PEER_TOPK

LANES = 128
SUBLANES = 8
VMEM_LIMIT = 56 * 1024 * 1024

_REF_QR0 = ATT_WIDTH + 2 * ATT_KV_WIDTH
_QR0 = 0
_KR0 = _QR0 + RET_WIDTH
_VR0 = _KR0 + RET_WIDTH
_GR0 = _VR0 + RET_WIDTH
_QA0 = _GR0 + RET_WIDTH
_KA0 = _QA0 + ATT_WIDTH
_VA0 = _KA0 + ATT_KV_WIDTH


def _reorder_w_in(w):
    return jnp.concatenate([w[:, _REF_QR0:], w[:, :_REF_QR0]], axis=1)


def _params(*sem):
    return pltpu.CompilerParams(dimension_semantics=sem, vmem_limit_bytes=VMEM_LIMIT)


def _rms(xf, gain):
    ms = jnp.mean(xf * xf, axis=-1, keepdims=True)
    return xf * lax.rsqrt(ms + EPS) * gain


def _dot(a, b):
    return jnp.dot(a, b, preferred_element_type=F32)


def _dot_nt(a, b):
    return lax.dot_general(a, b, (((1,), (1,)), ((), ())), preferred_element_type=F32)


def _norm_matmul_kernel(x_ref, g_ref, w_ref, o_ref, xn_ref):
    @pl.when(pl.program_id(1) == 0)
    def _():
        xn_ref[...] = _rms(x_ref[...], g_ref[...]).astype(BF16)

    o_ref[...] = _dot(xn_ref[...], w_ref[...]).astype(o_ref.dtype)


def _norm_matmul(x, gain, w_bf16, tm, tn, out_dtype):
    t, d = x.shape
    n = w_bf16.shape[1]
    return pl.pallas_call(
        _norm_matmul_kernel,
        grid=(t // tm, n // tn),
        in_specs=[
            pl.BlockSpec((tm, d), lambda i, j: (i, 0)),
            pl.BlockSpec((1, d), lambda i, j: (0, 0)),
            pl.BlockSpec((d, tn), lambda i, j: (0, j)),
        ],
        out_specs=pl.BlockSpec((tm, tn), lambda i, j: (i, j)),
        out_shape=jax.ShapeDtypeStruct((t, n), out_dtype),
        scratch_shapes=[pltpu.VMEM((tm, d), BF16)],
        compiler_params=_params("parallel", "arbitrary"),
        name="norm_matmul",
    )(x, gain.reshape(1, d), w_bf16)


def _swa_kernel(sink_ref, q_ref, kc_ref, kp_ref, vc_ref, vp_ref, cc_ref, sc_ref, cp_ref, sp_ref,
                gain_ref, o_ref):
    n = pl.program_id(1)
    lane = lax.broadcasted_iota(jnp.int32, (1, LANES), 1)
    first_half = (lane % HEAD_DIM) < (HEAD_DIM // 2)
    lo = lane < HEAD_DIM

    def rope(x, c, s):
        partner = jnp.where(first_half, pltpu.roll(x, LANES - HEAD_DIM // 2, 1),
                            pltpu.roll(x, HEAD_DIM // 2, 1))
        return x * c + partner * s

    cc, sc = cc_ref[...], sc_ref[...]
    k = jnp.concatenate([rope(kp_ref[...].astype(F32), cp_ref[...], sp_ref[...]),
                         rope(kc_ref[...].astype(F32), cc, sc)], axis=0)
    v = jnp.concatenate([vp_ref[...].astype(F32), vc_ref[...].astype(F32)], axis=0)
    k_sw = pltpu.roll(k, HEAD_DIM, 1)
    v_sw = pltpu.roll(v, HEAD_DIM, 1)

    def place(a, a_sw, c, half):
        src = a if c == half else a_sw
        keep = lo if half == 0 else jnp.logical_not(lo)
        return jnp.where(keep, src, 0.0).astype(BF16)

    kvar = {(c, h): place(k, k_sw, c, h) for c in range(ATT_KV_HEADS) for h in range(2)}
    vvar = {(c, h): place(v, v_sw, c, h) for c in range(ATT_KV_HEADS) for h in range(2)}

    qi = lax.broadcasted_iota(jnp.int32, (BLOCK, 2 * BLOCK), 0) + BLOCK
    ki = lax.broadcasted_iota(jnp.int32, (BLOCK, 2 * BLOCK), 1)
    dist = qi - ki
    kmin = jnp.where(n > 0, 0, BLOCK)
    valid = (dist >= 0) & (dist < BLOCK) & (ki >= kmin)

    group = ATT_Q_HEADS // ATT_KV_HEADS
    for j in range(ATT_WIDTH // LANES):
        c = (2 * j) // group
        cols = slice(j * LANES, (j + 1) * LANES)
        qg = (rope(q_ref[:, cols].astype(F32), cc, sc) * (HEAD_DIM ** -0.5)).astype(BF16)
        out = jnp.zeros((BLOCK, LANES), F32)
        for half in range(2):
            s = jnp.where(valid, _dot_nt(qg, kvar[(c, half)]), -jnp.inf)
            sink = sink_ref[2 * j + half]
            m = jnp.maximum(jnp.max(s, axis=-1, keepdims=True), sink)
            p = jnp.exp(s - m)
            denom = jnp.sum(p, axis=-1, keepdims=True) + jnp.exp(sink - m)
            out = out + _dot((p / denom).astype(BF16), vvar[(c, half)])
        sq = out * out
        ss_lo = jnp.sum(jnp.where(lo, sq, 0.0), axis=-1, keepdims=True)
        ss_hi = jnp.sum(jnp.where(lo, 0.0, sq), axis=-1, keepdims=True)
        ms = jnp.where(lo, ss_lo, ss_hi) * (1.0 / HEAD_DIM)
        o_ref[:, cols] = (out * lax.rsqrt(ms + EPS) * gain_ref[:, cols]).astype(o_ref.dtype)


def _swa(h, cos_a, sin_a, sinks, gain, batch, seq):
    nb = seq // BLOCK
    t = batch * seq
    kcol, vcol = _KA0 // LANES, _VA0 // LANES
    cur = lambda b, n: b * nb + n
    prev = lambda b, n: b * nb + jnp.maximum(n - 1, 0)
    row = lambda col, f: pl.BlockSpec((BLOCK, LANES), lambda b, n: (f(b, n), col))
    return pl.pallas_call(
        _swa_kernel,
        grid=(batch, nb),
        in_specs=[
            pl.BlockSpec(memory_space=pltpu.SMEM),
            pl.BlockSpec((BLOCK, ATT_WIDTH), lambda b, n: (cur(b, n), _QA0 // ATT_WIDTH)),
            row(kcol, cur), row(kcol, prev), row(vcol, cur), row(vcol, prev),
            row(0, cur), row(0, cur), row(0, prev), row(0, prev),
            pl.BlockSpec((1, ATT_WIDTH), lambda b, n: (0, 0)),
        ],
        out_specs=pl.BlockSpec((BLOCK, ATT_WIDTH), lambda b, n: (cur(b, n), 0)),
        out_shape=jax.ShapeDtypeStruct((t, ATT_WIDTH), BF16),
        compiler_params=_params("parallel", "parallel"),
        name="swa_attention",
    )(sinks, h, h, h, h, h, cos_a, sin_a, cos_a, sin_a, gain.reshape(1, ATT_WIDTH))


def _ret_kernel(lg_ref, cd_ref, q_ref, k_ref, v_ref, g_ref, c_ref, s_ref, gain_ref, o_ref, state_ref):
    n = pl.program_id(1)

    @pl.when(n == 0)
    def _():
        state_ref[...] = jnp.zeros_like(state_ref)

    c, s = c_ref[...], s_ref[...]
    hd = RET_HEAD_DIM
    half = hd // 2

    def rope(x):
        x1, x2 = x[:, :half], x[:, half:]
        return jnp.concatenate([x1 * c - x2 * s, x2 * c + x1 * s], axis=1)

    ri = lax.broadcasted_iota(jnp.int32, (RET_CHUNK, RET_CHUNK), 0).astype(F32)
    ci = lax.broadcasted_iota(jnp.int32, (RET_CHUNK, RET_CHUNK), 1).astype(F32)
    diff = ri - ci
    for hh in range(RET_HEADS):
        cols = slice(hh * hd, (hh + 1) * hd)
        lg = lg_ref[hh]
        q = rope(q_ref[:, cols].astype(F32))
        k = rope(k_ref[:, cols].astype(F32)) * (hd ** -0.5)
        v = v_ref[:, cols].astype(BF16)

        decay = jnp.where(diff >= 0, jnp.exp(jnp.maximum(diff, 0.0) * lg), 0.0)
        zeta = jnp.exp((RET_CHUNK - 1 - ri) * lg)
        xi = jnp.exp((ri + 1.0) * lg)
        zeta2 = jnp.concatenate([zeta, zeta], axis=1)
        xi2 = jnp.concatenate([xi, xi], axis=1)

        inner = _dot_nt(q.astype(BF16), k.astype(BF16)) * decay
        state = state_ref[hh]
        out = _dot(inner.astype(BF16), v) + _dot((q * xi2).astype(BF16), state.astype(BF16))
        kv = lax.dot_general((k * zeta2).astype(BF16), v, (((0,), (0,)), ((), ())), preferred_element_type=F32)
        state_ref[hh] = state * cd_ref[hh] + kv

        mu = jnp.mean(out, axis=-1, keepdims=True)
        cen = out - mu
        var = jnp.mean(cen * cen, axis=-1, keepdims=True)
        g = g_ref[:, cols].astype(F32)
        o = cen * lax.rsqrt(var + GN_EPS) * gain_ref[:, cols] * (g * jax.nn.sigmoid(g))
        o_ref[:, cols] = o.astype(o_ref.dtype)


def _retention(h, cos_r, sin_r, gain, batch, seq):
    nc = seq // RET_CHUNK
    t = batch * seq
    hd = RET_HEAD_DIM
    log_gamma = jnp.log1p(-jnp.exp2(-5.0 - jnp.arange(RET_HEADS, dtype=F32)))
    chunk_decay = jnp.exp(RET_CHUNK * log_gamma)
    col = lambda c0: pl.BlockSpec((RET_CHUNK, RET_WIDTH), lambda b, n: (b * nc + n, c0 // RET_WIDTH))
    tab = pl.BlockSpec((RET_CHUNK, hd // 2), lambda b, n: (b * nc + n, 0))
    return pl.pallas_call(
        _ret_kernel,
        grid=(batch, nc),
        in_specs=[
            pl.BlockSpec(memory_space=pltpu.SMEM),
            pl.BlockSpec(memory_space=pltpu.SMEM),
            col(_QR0), col(_KR0), col(_VR0), col(_GR0), tab, tab,
            pl.BlockSpec((1, RET_WIDTH), lambda b, n: (0, 0)),
        ],
        out_specs=pl.BlockSpec((RET_CHUNK, RET_WIDTH), lambda b, n: (b * nc + n, 0)),
        out_shape=jax.ShapeDtypeStruct((t, RET_WIDTH), BF16),
        scratch_shapes=[pltpu.VMEM((RET_HEADS, hd, hd), F32)],
        compiler_params=_params("parallel", "arbitrary"),
        name="retention",
    )(log_gamma, chunk_decay, h, h, h, h, cos_r, sin_r, gain.reshape(1, RET_WIDTH))


def _outproj_kernel(x_ref, oa_ref, or_ref, wa_ref, wr_ref, g_ref, wq_ref, x1_ref, qx_ref):
    x1 = x_ref[...] + _dot(oa_ref[...], wa_ref[...]) + _dot(or_ref[...], wr_ref[...])
    x1_ref[...] = x1
    qx_ref[...] = _dot(_rms(x1, g_ref[...]).astype(BF16), wq_ref[...]).astype(qx_ref.dtype)


def _outproj(x, oa, orr, w_out_bf16, gain, w_xq_bf16, tm):
    t, d = x.shape
    full = lambda shape: pl.BlockSpec(shape, lambda i: (0, 0))
    return pl.pallas_call(
        _outproj_kernel,
        grid=(t // tm,),
        in_specs=[
            pl.BlockSpec((tm, d), lambda i: (i, 0)),
            pl.BlockSpec((tm, ATT_WIDTH), lambda i: (i, 0)),
            pl.BlockSpec((tm, RET_WIDTH), lambda i: (i, 0)),
            pl.BlockSpec((ATT_WIDTH, d), lambda i: (0, 0)),
            pl.BlockSpec((RET_WIDTH, d), lambda i: (1, 0)),
            full((1, d)),
            full((d, XATT_WIDTH)),
        ],
        out_specs=[pl.BlockSpec((tm, d), lambda i: (i, 0)), pl.BlockSpec((tm, XATT_WIDTH), lambda i: (i, 0))],
        out_shape=[jax.ShapeDtypeStruct((t, d), F32), jax.ShapeDtypeStruct((t, XATT_WIDTH), BF16)],
        compiler_params=_params("parallel"),
        name="out_proj",
    )(x, oa, orr, w_out_bf16, w_out_bf16, gain.reshape(1, d), w_xq_bf16)


def _xattn_kernel(qx_ref, k_ref, v_ref, x1_ref, wo_ref, g_ref, wpq_ref, x2_ref, pq_ref):
    heads = []
    for hh in range(XATT_HEADS):
        cols = slice(hh * XATT_HEAD_DIM, (hh + 1) * XATT_HEAD_DIM)
        s = _dot_nt(qx_ref[:, cols], k_ref[:, cols]) * (XATT_HEAD_DIM ** -0.5)
        p = jnp.exp(s - jnp.max(s, axis=-1, keepdims=True))
        p = p / jnp.sum(p, axis=-1, keepdims=True)
        heads.append(_dot(p.astype(BF16), v_ref[:, cols]))
    o = jnp.concatenate(heads, axis=1).astype(BF16)
    x2 = x1_ref[...] + _dot(o, wo_ref[...])
    x2_ref[...] = x2
    pq_ref[...] = _dot(_rms(x2, g_ref[...]).astype(BF16), wpq_ref[...])


def _xattn(qx, kv_mem, x1, w_xo_bf16, gain, w_pq_bf16, batch, seq, mem_len, tm):
    t, d = x1.shape
    npq = w_pq_bf16.shape[1]
    nt = seq // tm
    rows = lambda width: pl.BlockSpec((tm, width), lambda b, i: (b * nt + i, 0))
    full = lambda shape: pl.BlockSpec(shape, lambda b, i: (0, 0))
    return pl.pallas_call(
        _xattn_kernel,
        grid=(batch, nt),
        in_specs=[
            rows(XATT_WIDTH),
            pl.BlockSpec((mem_len, XATT_WIDTH), lambda b, i: (b, 0)),
            pl.BlockSpec((mem_len, XATT_WIDTH), lambda b, i: (b, 1)),
            rows(d),
            full((XATT_WIDTH, d)),
            full((1, d)),
            full((d, npq)),
        ],
        out_specs=[rows(d), rows(npq)],
        out_shape=[jax.ShapeDtypeStruct((t, d), F32), jax.ShapeDtypeStruct((t, npq), F32)],
        compiler_params=_params("parallel", "parallel"),
        name="cross_attention",
    )(qx, kv_mem, kv_mem, x1, w_xo_bf16, gain.reshape(1, d), w_pq_bf16)


def _topk_rows(s, k, payload=None):
    nrows = s.shape[0]
    rows = lax.broadcasted_iota(jnp.int32, s.shape, 0).astype(F32)
    vals, sel = [], []
    for _ in range(k):
        m = jnp.max(s, axis=0, keepdims=True)
        am = jnp.min(jnp.where(s == m, rows, float(nrows)), axis=0, keepdims=True)
        hit = rows == am
        vals.append(m)
        sel.append(am if payload is None else jnp.max(jnp.where(hit, payload, -1.0), axis=0, keepdims=True))
        s = jnp.where(hit, -jnp.inf, s)
    return jnp.concatenate(vals, axis=0), jnp.concatenate(sel, axis=0)


def _staircase(t1, t2):
    kk, half = PEER_TOPK, SUBLANES
    blocks = [(t1[0:1, :], t2, kk), (t1[1:2, :], t2[0:half, :], half)]
    blocks += [(t1[a:a + 1, :], t2[0:half, :], kk // (a + 1)) for a in range(2, half)]
    blocks.append((t1[half:kk, :], t2[0:1, :], half))
    return blocks


def _peer_topk_kernel(q_ref, keys_ref, ids_ref, gates_ref, *, heads):
    width = 2 * PEER_HALF_DIM
    row8 = lax.broadcasted_iota(jnp.int32, (SUBLANES, q_ref.shape[0]), 0)
    for hh in range(heads):
        tops = []
        for p in range(2):
            c0 = hh * width + p * PEER_HALF_DIM
            qp = q_ref[:, c0:c0 + PEER_HALF_DIM].astype(BF16)
            tops.append(_topk_rows(_dot_nt(keys_ref[hh, p], qp), PEER_TOPK))
        (s1, i1), (s2, i2) = tops
        cand_s, cand_i = [], []
        for (a_s, b_s, live), (a_i, b_i, _) in zip(_staircase(s1, s2), _staircase(i1 * float(PEER_N_KEYS), i2)):
            blk = a_s + b_s
            if live < blk.shape[0]:
                blk = jnp.where(row8 < live, blk, -jnp.inf)
            cand_s.append(blk)
            cand_i.append(a_i + b_i)
        top_s, top_e = _topk_rows(jnp.concatenate(cand_s, axis=0), PEER_TOPK,
                                  payload=jnp.concatenate(cand_i, axis=0))
        e = jnp.exp(top_s - top_s[0:1, :])
        ids_ref[hh] = top_e.astype(jnp.int32)
        gates_ref[hh] = e / jnp.sum(e, axis=0, keepdims=True)


def _peer_topk(pq, keys_bf16, tt, heads):
    t = pq.shape[0]
    width = 2 * PEER_HALF_DIM
    out = pl.BlockSpec((heads, PEER_TOPK, tt), lambda i, hh: (hh, 0, i))
    return pl.pallas_call(
        functools.partial(_peer_topk_kernel, heads=heads),
        grid=(t // tt, PEER_HEADS // heads),
        in_specs=[
            pl.BlockSpec((tt, heads * width), lambda i, hh: (i, hh)),
            pl.BlockSpec((heads, 2, PEER_N_KEYS, PEER_HALF_DIM), lambda i, hh: (hh, 0, 0, 0)),
        ],
        out_specs=[out, out],
        out_shape=[jax.ShapeDtypeStruct((PEER_HEADS, PEER_TOPK, t), jnp.int32),
                   jax.ShapeDtypeStruct((PEER_HEADS, PEER_TOPK, t), F32)],
        compiler_params=_params("parallel", "parallel"),
        name="peer_topk",
    )(pq, keys_bf16)


GATHER_TOKENS = 128
GATHER_BUFS = 8
GATHER_GROUP = 8
assert GATHER_GROUP % GATHER_BUFS == 0 and GATHER_GROUP % SUBLANES == 0 and GATHER_TOKENS % GATHER_GROUP == 0


def _peer_gather_kernel(ids_hbm, tbl_hbm, x2_ref, gates_ref, fg_ref, og_ref, out_ref,
                        ids_smem, ids_sem, *scratch, final_norm, first_block):
    gbufs = scratch[:GATHER_BUFS]
    gsem, xn_ref, y_ref, yacc_ref = scratch[GATHER_BUFS:]
    i = pl.program_id(0)
    has_next = i + 1 < pl.num_programs(0)
    slot = i % 2
    other = 1 - slot
    tb = GATHER_TOKENS
    nchunk = D_MODEL // LANES
    nids = tb * PEER_SLOTS
    ahead = GATHER_BUFS - 1
    group = GATHER_GROUP

    def ids_copy(step, sl):
        return pltpu.make_async_copy(ids_hbm.at[pl.ds(pl.multiple_of((first_block + step) * nids, nids), nids)],
                                     ids_smem.at[pl.ds(pl.multiple_of(sl * nids, nids), nids)], ids_sem.at[sl])

    def row_copy(e, b, k):
        return pltpu.make_async_copy(tbl_hbm.at[e], gbufs[b].at[pl.ds(k, 1)], gsem.at[b])

    def issue(sl, tok, b):
        base = sl * nids + tok * PEER_SLOTS
        for k in range(PEER_SLOTS):
            row_copy(ids_smem[base + k], b, k).start(priority=k % 2)

    def wait_rows(b):
        for k in range(PEER_SLOTS):
            row_copy(0, b, k).wait()

    @pl.when(i == 0)
    def _():
        ids_copy(0, 0).start()
        ids_copy(0, 0).wait()
        for tok in range(ahead):
            issue(0, tok, tok)

    @pl.when(has_next)
    def _():
        ids_copy(i + 1, other).start()

    xn_ref[...] = _rms(x2_ref[...], fg_ref[...])

    lane = lax.broadcasted_iota(jnp.int32, (PEER_SLOTS, LANES), 1)
    hi_mask = jnp.uint32(0xFFFF0000)

    def compute(tok, r, b):
        gbuf = gbufs[b]
        xrow = xn_ref[pl.ds(tok, 1), :]
        acc = jnp.zeros((PEER_SLOTS, LANES), F32)
        for c in range(nchunk):
            w = gbuf[:, c * LANES:(c + 1) * LANES]
            u = lax.bitcast_convert_type(w << 16, F32)
            acc = acc + u * xrow[:, c * LANES:(c + 1) * LANES]
        a = jnp.sum(acc, axis=-1, keepdims=True)
        gate = jnp.sum(jnp.where(lane == tok, gates_ref[...], 0.0), axis=-1, keepdims=True)
        coeff = 0.5 * a * (1.0 + lax.erf(a * (2.0 ** -0.5))) * gate
        for c in range(nchunk):
            w = gbuf[:, c * LANES:(c + 1) * LANES]
            vv = lax.bitcast_convert_type(w & hi_mask, F32)
            yacc_ref[r:r + 1, c * LANES:(c + 1) * LANES] = jnp.sum(vv * coeff, axis=0, keepdims=True)

    def run_group(g, last):
        for r in range(group):
            b = r % GATHER_BUFS
            nb = (r + ahead) % GATHER_BUFS
            tok = g * group + r
            wait_rows(b)
            if not last or r + ahead < group:
                issue(slot, tok + ahead, nb)
            else:
                if r + ahead == group:
                    @pl.when(has_next)
                    def _():
                        ids_copy(i + 1, other).wait()

                @pl.when(has_next)
                def _():
                    issue(other, r + ahead - group, nb)
            compute(tok, r, b)
        y_ref[pl.ds(pl.multiple_of(g * group, group), group), :] = yacc_ref[...]

    def body(g, carry):
        run_group(g, last=False)
        return carry

    ngroups = tb // group
    lax.fori_loop(0, ngroups - 1, body, 0)
    run_group(ngroups - 1, last=True)
    x3 = x2_ref[...] + y_ref[...]
    out_ref[...] = _rms(x3, og_ref[...]) if final_norm else x3


def _peer_gather(ids_flat, table, x2, gates_t, ffn_gain, final_gain, final_norm, first_block=0):
    t, d = x2.shape
    tb = GATHER_TOKENS
    nsteps = t // tb - first_block
    full = lambda shape: pl.BlockSpec(shape, lambda i: (0, 0))
    return pl.pallas_call(
        functools.partial(_peer_gather_kernel, final_norm=final_norm, first_block=first_block),
        grid=(nsteps,),
        in_specs=[
            pl.BlockSpec(memory_space=pl.ANY),
            pl.BlockSpec(memory_space=pl.ANY),
            pl.BlockSpec((tb, d), lambda i: (i + first_block, 0)),
            pl.BlockSpec((PEER_SLOTS, tb), lambda i: (0, i + first_block)),
            full((1, d)),
            full((1, d)),
        ],
        out_specs=pl.BlockSpec((tb, d), lambda i: (i + first_block, 0)),
        out_shape=jax.ShapeDtypeStruct((t, d), F32),
        scratch_shapes=[
            pltpu.SMEM((2 * tb * PEER_SLOTS,), jnp.int32),
            pltpu.SemaphoreType.DMA((2,)),
            *[pltpu.VMEM((PEER_SLOTS, d), jnp.uint32) for _ in range(GATHER_BUFS)],
            pltpu.SemaphoreType.DMA((GATHER_BUFS,)),
            pltpu.VMEM((tb, d), F32),
            pltpu.VMEM((tb, d), F32),
            pltpu.VMEM((GATHER_GROUP, d), F32),
        ],
        compiler_params=_params("arbitrary"),
        name="peer_gather",
    )(ids_flat, table, x2, gates_t, ffn_gain.reshape(1, d), final_gain.reshape(1, d))


SC_CORES, SC_SUBCORES, SC_LANES = 2, 16, 16
SC_WORKERS = SC_CORES * SC_SUBCORES
SC_ROWS = SC_LANES
SC_COLS = 8
SC_DOT_ROWS = 8
SC_SHARE = (5, 16)


def _sc_erf(x):
    ax = jnp.abs(x)
    t = 1.0 / (1.0 + 0.3275911 * ax)
    poly = t * (0.254829592 + t * (-0.284496736 + t * (1.421413741 + t * (-1.453152027 + t * 1.061405429))))
    y = 1.0 - poly * jnp.exp(-ax * ax)
    return jnp.where(x < 0, -y, y)


def _peer_sc(ids_flat, gates_tok, xn, table, ts):
    d = D_MODEL
    lanes, rows = SC_LANES, SC_ROWS
    nchunk = PEER_SLOTS // rows
    block = lanes * SC_COLS
    per_worker = ts // SC_WORKERS
    assert ts % SC_WORKERS == 0 and d % block == 0
    mesh = plsc.VectorSubcoreMesh(core_axis_name="c", subcore_axis_name="s")

    @functools.partial(
        pl.kernel, mesh=mesh,
        out_type=jax.ShapeDtypeStruct((ts, d), F32),
        scratch_types=[
            pltpu.VMEM((PEER_SLOTS,), jnp.int32),
            pltpu.VMEM((PEER_SLOTS,), F32),
            pltpu.VMEM((d,), F32),
            pltpu.VMEM((d,), F32),
            pltpu.VMEM((2, rows, 1, d), jnp.uint32),
            pltpu.VMEM((rows, lanes), F32),
            pltpu.VMEM((lanes,), F32),
            pltpu.VMEM((rows, lanes), jnp.int32),
            pltpu.SemaphoreType.DMA((2,)),
        ],
        compiler_params=pltpu.CompilerParams(needs_layout_passes=False),
        name="peer_sparsecore",
    )
    def run(ids_hbm, gates_hbm, xn_hbm, tbl_hbm, y_hbm, ids_v, gates_v, x_v, y_v, rows_v, acc_v, coeff_v, ridx_v, sem):
        worker = lax.axis_index("s") * SC_CORES + lax.axis_index("c")
        lane = lax.iota(jnp.int32, lanes)
        zero = jnp.zeros((lanes,), F32)
        hi_mask = jnp.full((lanes,), 0xFFFF0000, jnp.uint32)
        for r in range(rows):
            ridx_v[r, :] = jnp.full((lanes,), r, jnp.int32)

        def gather(c, slot):
            return pltpu.make_async_copy(tbl_hbm.at[ids_v[pl.ds(c * rows, rows)]], rows_v.at[slot], sem.at[slot])

        @pl.loop(0, per_worker)
        def _(i):
            tok = worker * per_worker + i
            pltpu.sync_copy(ids_hbm.at[pl.ds(tok * PEER_SLOTS, PEER_SLOTS)], ids_v)
            pltpu.sync_copy(gates_hbm.at[pl.ds(tok * PEER_SLOTS, PEER_SLOTS)], gates_v)
            pltpu.sync_copy(xn_hbm.at[tok], x_v)

            @pl.loop(0, d // lanes)
            def _(j):
                y_v[pl.ds(j * lanes, lanes)] = zero

            gather(0, 0).start()
            for c in range(nchunk):
                slot = c % 2
                if c + 1 < nchunk:
                    gather(c + 1, 1 - slot).start()
                gather(c, slot).wait()
                for r0 in range(0, rows, SC_DOT_ROWS):
                    def dot_body(j, accs, r0=r0, slot=slot):
                        xj = x_v[pl.ds(j * lanes, lanes)]
                        return tuple(
                            accs[q] + plsc.bitcast(rows_v[slot, r0 + q, 0, pl.ds(j * lanes, lanes)] << 16, F32) * xj
                            for q in range(SC_DOT_ROWS))
                    accs = lax.fori_loop(0, d // lanes, dot_body, (zero,) * SC_DOT_ROWS, unroll=4)
                    for q in range(SC_DOT_ROWS):
                        acc_v[r0 + q, :] = accs[q]
                a = zero
                for col in range(lanes):
                    a = a + plsc.load_gather(acc_v, [lane, ridx_v[col, :]])
                gate = gates_v[pl.ds(c * rows, rows)]
                coeff_v[...] = 0.5 * a * (1.0 + _sc_erf(a * (2.0 ** -0.5))) * gate
                coeffs = [plsc.load_gather(coeff_v, [ridx_v[r, :]]) for r in range(rows)]

                @pl.loop(0, d // block)
                def _(jb, slot=slot, coeffs=coeffs):
                    base = jb * block
                    accs = [zero] * SC_COLS
                    for r in range(rows):
                        for q in range(SC_COLS):
                            w = rows_v[slot, r, 0, pl.ds(base + q * lanes, lanes)]
                            accs[q] = accs[q] + plsc.bitcast(w & hi_mask, F32) * coeffs[r]
                    for q in range(SC_COLS):
                        plsc.addupdate(y_v.at[pl.ds(base + q * lanes, lanes)], accs[q])

            pltpu.sync_copy(y_v, y_hbm.at[tok])

    return run(ids_flat, gates_tok, xn, table)


def _rms_rows_kernel(x_ref, g_ref, o_ref):
    o_ref[...] = _rms(x_ref[...], g_ref[...])


def _rms_rows(x, gain, rows, tm):
    d = x.shape[1]
    return pl.pallas_call(
        _rms_rows_kernel,
        grid=(rows // tm,),
        in_specs=[pl.BlockSpec((tm, d), lambda i: (i, 0)), pl.BlockSpec((1, d), lambda i: (0, 0))],
        out_specs=pl.BlockSpec((tm, d), lambda i: (i, 0)),
        out_shape=jax.ShapeDtypeStruct((rows, d), F32),
        compiler_params=_params("parallel"),
        name="rms_rows",
    )(x, gain.reshape(1, d))


def _peer_finish_kernel(x2_ref, y_ref, og_ref, prev_ref, out_ref, *, final_norm):
    del prev_ref
    x3 = x2_ref[...] + y_ref[...]
    out_ref[...] = _rms(x3, og_ref[...]) if final_norm else x3


def _peer_finish(x2, y_head, out_rest, final_gain, final_norm):
    rows, d = y_head.shape
    tm = GATHER_TOKENS
    blk = pl.BlockSpec((tm, d), lambda i: (i, 0))
    return pl.pallas_call(
        functools.partial(_peer_finish_kernel, final_norm=final_norm),
        grid=(rows // tm,),
        in_specs=[blk, blk, pl.BlockSpec((1, d), lambda i: (0, 0)), pl.BlockSpec(memory_space=pl.ANY)],
        out_specs=blk,
        out_shape=jax.ShapeDtypeStruct(out_rest.shape, F32),
        input_output_aliases={3: 0},
        compiler_params=_params("parallel"),
        name="peer_finish",
    )(x2, y_head, final_gain.reshape(1, d), out_rest)


def _rope_angles(positions, dh):
    inv_freq = ROPE_THETA ** (-jnp.arange(0, dh, 2, dtype=F32) / dh)
    ang = positions.astype(F32).reshape(-1, 1) * inv_freq
    return jnp.cos(ang), jnp.sin(ang)


def _rope_tables_att(positions):
    cos, sin = _rope_angles(positions, HEAD_DIM)
    copies = LANES // HEAD_DIM
    return (jnp.tile(jnp.concatenate([cos, cos], axis=1), (1, copies)),
            jnp.tile(jnp.concatenate([-sin, sin], axis=1), (1, copies)))


def _pack_table_kernel(u_ref, v_ref, o_ref):
    ub = lax.bitcast_convert_type(u_ref[...].astype(BF16).astype(F32), jnp.uint32)
    vb = lax.bitcast_convert_type(v_ref[...].astype(BF16).astype(F32), jnp.uint32)
    o_ref[:, 0, :] = (ub >> 16) | vb


def _pack_expert_table(u, v, rows=256):
    n, d = u.shape
    blk = pl.BlockSpec((rows, d), lambda i: (i, 0))
    return pl.pallas_call(
        _pack_table_kernel,
        grid=(n // rows,),
        in_specs=[blk, blk],
        out_specs=pl.BlockSpec((rows, 1, d), lambda i: (i, 0, 0)),
        out_shape=jax.ShapeDtypeStruct((n, 1, d), jnp.uint32),
        compiler_params=_params("parallel"),
        name="pack_table",
    )(u, v)


def kernel(x, mem, positions, mix_norm_gain, w_in, att_sinks, att_out_gain, ret_out_gain, w_out,
           cross_norm_gain, mem_norm_gain, w_xq, w_xk, w_xv, w_xo, ffn_norm_gain,
           w_peer_q, peer_sub_keys, peer_u, peer_v, final_norm_gain):
    batch, seq, d = x.shape
    mem_len = mem.shape[1]
    t = batch * seq
    depth = w_in.shape[0]
    assert d == D_MODEL and seq % BLOCK == 0 and t % GATHER_TOKENS == 0

    cos_a, sin_a = _rope_tables_att(positions)
    cos_r, sin_r = _rope_angles(positions, RET_HEAD_DIM)
    xf = x.reshape(t, d)
    memf = mem.reshape(batch * mem_len, d)
    for l in range(depth):
        h = _norm_matmul(xf, mix_norm_gain[l], _reorder_w_in(w_in[l]).astype(BF16), min(1024, t), 1792, BF16)
        oa = _swa(h, cos_a, sin_a, att_sinks[l], att_out_gain[l], batch, seq)
        orr = _retention(h, cos_r, sin_r, ret_out_gain[l], batch, seq)
        x1, qx = _outproj(xf, oa, orr, w_out[l].astype(BF16), cross_norm_gain[l], w_xq[l].astype(BF16),
                          min(256, seq))
        w_kv = jnp.concatenate([w_xk[l], w_xv[l]], axis=1).astype(BF16)
        kv_mem = _norm_matmul(memf, mem_norm_gain[l], w_kv, mem_len, 2 * XATT_WIDTH, BF16)
        x2, pq = _xattn(qx, kv_mem, x1, w_xo[l].astype(BF16), ffn_norm_gain[l], w_peer_q[l].astype(BF16),
                        batch, seq, mem_len, min(256, seq))
        ids_t, gates_t = _peer_topk(pq, peer_sub_keys[l].astype(BF16), LANES, PEER_HEADS)
        ids = ids_t.reshape(PEER_SLOTS, t).T.reshape(t * PEER_SLOTS)
        table = _pack_expert_table(peer_u[l], peer_v[l])
        gates_t = gates_t.reshape(PEER_SLOTS, t)
        last = l == depth - 1
        nblocks = t // GATHER_TOKENS
        sc_blocks = nblocks * SC_SHARE[0] // SC_SHARE[1] if nblocks % SC_SHARE[1] == 0 else 0
        ts = sc_blocks * GATHER_TOKENS
        if sc_blocks:
            xn_head = _rms_rows(x2, ffn_norm_gain[l], ts, GATHER_TOKENS)
            y_head = _peer_sc(ids, gates_t[:, :ts].T.reshape(ts * PEER_SLOTS), xn_head, table, ts)
        xf = _peer_gather(ids, table, x2, gates_t, ffn_norm_gain[l], final_norm_gain, last, sc_blocks)
        if sc_blocks:
            xf = _peer_finish(x2, y_head, xf, final_norm_gain, last)
    return xf.reshape(batch, seq, d)
```

```python
import functools
import math

import jax
import jax.numpy as jnp
from jax import lax
from jax.experimental import pallas as pl
from jax.experimental.pallas import tpu as pltpu
from jax.experimental.pallas import tpu_sc as plsc

F32 = jnp.float32
BF16 = jnp.bfloat16

EPS = 1e-6
GN_EPS = 1e-5
ROPE_THETA = 10000.0
D_MODEL = 2048
HEAD_DIM = 64
ATT_Q_HEADS = 16
ATT_KV_HEADS = 2
ATT_WIDTH = ATT_Q_HEADS * HEAD_DIM
ATT_KV_WIDTH = ATT_KV_HEADS * HEAD_DIM
BLOCK = 128
RET_HEADS = 4
RET_HEAD_DIM = 256
RET_WIDTH = RET_HEADS * RET_HEAD_DIM
RET_CHUNK = 128
IN_COLS = ATT_WIDTH + 2 * ATT_KV_WIDTH + 4 * RET_WIDTH
XATT_HEADS = 4
XATT_HEAD_DIM = 128
XATT_WIDTH = XATT_HEADS * XATT_HEAD_DIM
PEER_HEADS = 8
PEER_N_KEYS = 128
PEER_HALF_DIM = 128
PEER_TOPK = 16
PEER_SLOTS = PEER_HEADS * PEER_TOPK

LANES = 128
SUBLANES = 8
VMEM_LIMIT = 56 * 1024 * 1024

_REF_QR0 = ATT_WIDTH + 2 * ATT_KV_WIDTH
_QR0 = 0
_KR0 = _QR0 + RET_WIDTH
_VR0 = _KR0 + RET_WIDTH
_GR0 = _VR0 + RET_WIDTH
_QA0 = _GR0 + RET_WIDTH
_KA0 = _QA0 + ATT_WIDTH
_VA0 = _KA0 + ATT_KV_WIDTH


def _reorder_w_in(w):
    return jnp.concatenate([w[:, _REF_QR0:], w[:, :_REF_QR0]], axis=1)


def _params(*sem):
    return pltpu.CompilerParams(dimension_semantics=sem, vmem_limit_bytes=VMEM_LIMIT)


def _rms(xf, gain):
    ms = jnp.mean(xf * xf, axis=-1, keepdims=True)
    return xf * lax.rsqrt(ms + EPS) * gain


def _dot(a, b):
    return jnp.dot(a, b, preferred_element_type=F32)


def _dot_nt(a, b):
    return lax.dot_general(a, b, (((1,), (1,)), ((), ())), preferred_element_type=F32)


def _norm_matmul_kernel(x_ref, g_ref, w_ref, o_ref, xn_ref):
    @pl.when(pl.program_id(1) == 0)
    def _():
        xn_ref[...] = _rms(x_ref[...], g_ref[...]).astype(BF16)

    o_ref[...] = _dot(xn_ref[...], w_ref[...]).astype(o_ref.dtype)


def _norm_matmul(x, gain, w_bf16, tm, tn, out_dtype, row0=0, rows=None):
    d = x.shape[1]
    t = x.shape[0] if rows is None else rows
    n = w_bf16.shape[1]
    blk0 = row0 // tm
    return pl.pallas_call(
        _norm_matmul_kernel,
        grid=(t // tm, n // tn),
        in_specs=[
            pl.BlockSpec((tm, d), lambda i, j: (i + blk0, 0)),
            pl.BlockSpec((1, d), lambda i, j: (0, 0)),
            pl.BlockSpec((d, tn), lambda i, j: (0, j)),
        ],
        out_specs=pl.BlockSpec((tm, tn), lambda i, j: (i, j)),
        out_shape=jax.ShapeDtypeStruct((t, n), out_dtype),
        scratch_shapes=[pltpu.VMEM((tm, d), BF16)],
        compiler_params=_params("parallel", "arbitrary"),
        name="norm_matmul",
    )(x, gain.reshape(1, d), w_bf16)


def _swa_kernel(sink_ref, q_ref, kc_ref, kp_ref, vc_ref, vp_ref, cc_ref, sc_ref, cp_ref, sp_ref,
                gain_ref, o_ref):
    n = pl.program_id(1)
    lane = lax.broadcasted_iota(jnp.int32, (1, LANES), 1)
    first_half = (lane % HEAD_DIM) < (HEAD_DIM // 2)
    lo = lane < HEAD_DIM

    def rope(x, c, s):
        partner = jnp.where(first_half, pltpu.roll(x, LANES - HEAD_DIM // 2, 1),
                            pltpu.roll(x, HEAD_DIM // 2, 1))
        return x * c + partner * s

    cc, sc = cc_ref[...], sc_ref[...]
    k = jnp.concatenate([rope(kp_ref[...].astype(F32), cp_ref[...], sp_ref[...]),
                         rope(kc_ref[...].astype(F32), cc, sc)], axis=0)
    v = jnp.concatenate([vp_ref[...].astype(F32), vc_ref[...].astype(F32)], axis=0)
    k_sw = pltpu.roll(k, HEAD_DIM, 1)
    v_sw = pltpu.roll(v, HEAD_DIM, 1)

    def place(a, a_sw, c, half):
        src = a if c == half else a_sw
        keep = lo if half == 0 else jnp.logical_not(lo)
        return jnp.where(keep, src, 0.0).astype(BF16)

    kvar = {(c, h): place(k, k_sw, c, h) for c in range(ATT_KV_HEADS) for h in range(2)}
    vvar = {(c, h): place(v, v_sw, c, h) for c in range(ATT_KV_HEADS) for h in range(2)}

    qi = lax.broadcasted_iota(jnp.int32, (BLOCK, 2 * BLOCK), 0) + BLOCK
    ki = lax.broadcasted_iota(jnp.int32, (BLOCK, 2 * BLOCK), 1)
    dist = qi - ki
    kmin = jnp.where(n > 0, 0, BLOCK)
    valid = (dist >= 0) & (dist < BLOCK) & (ki >= kmin)

    group = ATT_Q_HEADS // ATT_KV_HEADS
    for j in range(ATT_WIDTH // LANES):
        c = (2 * j) // group
        cols = slice(j * LANES, (j + 1) * LANES)
        qg = (rope(q_ref[:, cols].astype(F32), cc, sc) * (HEAD_DIM ** -0.5)).astype(BF16)
        out = jnp.zeros((BLOCK, LANES), F32)
        for half in range(2):
            s = jnp.where(valid, _dot_nt(qg, kvar[(c, half)]), -jnp.inf)
            sink = sink_ref[2 * j + half]
            m = jnp.maximum(jnp.max(s, axis=-1, keepdims=True), sink)
            p = jnp.exp(s - m)
            denom = jnp.sum(p, axis=-1, keepdims=True) + jnp.exp(sink - m)
            out = out + _dot((p / denom).astype(BF16), vvar[(c, half)])
        sq = out * out
        ss_lo = jnp.sum(jnp.where(lo, sq, 0.0), axis=-1, keepdims=True)
        ss_hi = jnp.sum(jnp.where(lo, 0.0, sq), axis=-1, keepdims=True)
        ms = jnp.where(lo, ss_lo, ss_hi) * (1.0 / HEAD_DIM)
        o_ref[:, cols] = (out * lax.rsqrt(ms + EPS) * gain_ref[:, cols]).astype(o_ref.dtype)


def _swa(h, cos_a, sin_a, sinks, gain, batch, seq):
    nb = seq // BLOCK
    t = batch * seq
    kcol, vcol = _KA0 // LANES, _VA0 // LANES
    cur = lambda b, n: b * nb + n
    prev = lambda b, n: b * nb + jnp.maximum(n - 1, 0)
    row = lambda col, f: pl.BlockSpec((BLOCK, LANES), lambda b, n: (f(b, n), col))
    return pl.pallas_call(
        _swa_kernel,
        grid=(batch, nb),
        in_specs=[
            pl.BlockSpec(memory_space=pltpu.SMEM),
            pl.BlockSpec((BLOCK, ATT_WIDTH), lambda b, n: (cur(b, n), _QA0 // ATT_WIDTH)),
            row(kcol, cur), row(kcol, prev), row(vcol, cur), row(vcol, prev),
            row(0, cur), row(0, cur), row(0, prev), row(0, prev),
            pl.BlockSpec((1, ATT_WIDTH), lambda b, n: (0, 0)),
        ],
        out_specs=pl.BlockSpec((BLOCK, ATT_WIDTH), lambda b, n: (cur(b, n), 0)),
        out_shape=jax.ShapeDtypeStruct((t, ATT_WIDTH), BF16),
        compiler_params=_params("parallel", "parallel"),
        name="swa_attention",
    )(sinks, h, h, h, h, h, cos_a, sin_a, cos_a, sin_a, gain.reshape(1, ATT_WIDTH))


def _ret_kernel(lg_ref, cd_ref, q_ref, k_ref, v_ref, g_ref, c_ref, s_ref, gain_ref, o_ref, state_ref):
    n = pl.program_id(1)

    @pl.when(n == 0)
    def _():
        state_ref[...] = jnp.zeros_like(state_ref)

    c, s = c_ref[...], s_ref[...]
    hd = RET_HEAD_DIM
    half = hd // 2

    def rope(x):
        x1, x2 = x[:, :half], x[:, half:]
        return jnp.concatenate([x1 * c - x2 * s, x2 * c + x1 * s], axis=1)

    ri = lax.broadcasted_iota(jnp.int32, (RET_CHUNK, RET_CHUNK), 0).astype(F32)
    ci = lax.broadcasted_iota(jnp.int32, (RET_CHUNK, RET_CHUNK), 1).astype(F32)
    diff = ri - ci
    for hh in range(RET_HEADS):
        cols = slice(hh * hd, (hh + 1) * hd)
        lg = lg_ref[hh]
        q = rope(q_ref[:, cols].astype(F32))
        k = rope(k_ref[:, cols].astype(F32)) * (hd ** -0.5)
        v = v_ref[:, cols].astype(BF16)

        decay = jnp.where(diff >= 0, jnp.exp(jnp.maximum(diff, 0.0) * lg), 0.0)
        zeta = jnp.exp((RET_CHUNK - 1 - ri) * lg)
        xi = jnp.exp((ri + 1.0) * lg)
        zeta2 = jnp.concatenate([zeta, zeta], axis=1)
        xi2 = jnp.concatenate([xi, xi], axis=1)

        inner = _dot_nt(q.astype(BF16), k.astype(BF16)) * decay
        state = state_ref[hh]
        out = _dot(inner.astype(BF16), v) + _dot((q * xi2).astype(BF16), state.astype(BF16))
        kv = lax.dot_general((k * zeta2).astype(BF16), v, (((0,), (0,)), ((), ())), preferred_element_type=F32)
        state_ref[hh] = state * cd_ref[hh] + kv

        mu = jnp.mean(out, axis=-1, keepdims=True)
        cen = out - mu
        var = jnp.mean(cen * cen, axis=-1, keepdims=True)
        g = g_ref[:, cols].astype(F32)
        o = cen * lax.rsqrt(var + GN_EPS) * gain_ref[:, cols] * (g * jax.nn.sigmoid(g))
        o_ref[:, cols] = o.astype(o_ref.dtype)


def _retention(h, cos_r, sin_r, gain, batch, seq):
    nc = seq // RET_CHUNK
    t = batch * seq
    hd = RET_HEAD_DIM
    log_gamma = jnp.log1p(-jnp.exp2(-5.0 - jnp.arange(RET_HEADS, dtype=F32)))
    chunk_decay = jnp.exp(RET_CHUNK * log_gamma)
    col = lambda c0: pl.BlockSpec((RET_CHUNK, RET_WIDTH), lambda b, n: (b * nc + n, c0 // RET_WIDTH))
    tab = pl.BlockSpec((RET_CHUNK, hd // 2), lambda b, n: (b * nc + n, 0))
    return pl.pallas_call(
        _ret_kernel,
        grid=(batch, nc),
        in_specs=[
            pl.BlockSpec(memory_space=pltpu.SMEM),
            pl.BlockSpec(memory_space=pltpu.SMEM),
            col(_QR0), col(_KR0), col(_VR0), col(_GR0), tab, tab,
            pl.BlockSpec((1, RET_WIDTH), lambda b, n: (0, 0)),
        ],
        out_specs=pl.BlockSpec((RET_CHUNK, RET_WIDTH), lambda b, n: (b * nc + n, 0)),
        out_shape=jax.ShapeDtypeStruct((t, RET_WIDTH), BF16),
        scratch_shapes=[pltpu.VMEM((RET_HEADS, hd, hd), F32)],
        compiler_params=_params("parallel", "arbitrary"),
        name="retention",
    )(log_gamma, chunk_decay, h, h, h, h, cos_r, sin_r, gain.reshape(1, RET_WIDTH))


def _outproj_kernel(x_ref, oa_ref, or_ref, wa_ref, wr_ref, g_ref, wq_ref, x1_ref, qx_ref):
    x1 = x_ref[...] + _dot(oa_ref[...], wa_ref[...]) + _dot(or_ref[...], wr_ref[...])
    x1_ref[...] = x1
    qx_ref[...] = _dot(_rms(x1, g_ref[...]).astype(BF16), wq_ref[...]).astype(qx_ref.dtype)


def _outproj(x, oa, orr, w_out_bf16, gain, w_xq_bf16, tm, row0=0):
    t, d = oa.shape[0], x.shape[1]
    blk0 = row0 // tm
    full = lambda shape: pl.BlockSpec(shape, lambda i: (0, 0))
    return pl.pallas_call(
        _outproj_kernel,
        grid=(t // tm,),
        in_specs=[
            pl.BlockSpec((tm, d), lambda i: (i + blk0, 0)),
            pl.BlockSpec((tm, ATT_WIDTH), lambda i: (i, 0)),
            pl.BlockSpec((tm, RET_WIDTH), lambda i: (i, 0)),
            pl.BlockSpec((ATT_WIDTH, d), lambda i: (0, 0)),
            pl.BlockSpec((RET_WIDTH, d), lambda i: (1, 0)),
            full((1, d)),
            full((d, XATT_WIDTH)),
        ],
        out_specs=[pl.BlockSpec((tm, d), lambda i: (i, 0)), pl.BlockSpec((tm, XATT_WIDTH), lambda i: (i, 0))],
        out_shape=[jax.ShapeDtypeStruct((t, d), F32), jax.ShapeDtypeStruct((t, XATT_WIDTH), BF16)],
        compiler_params=_params("parallel"),
        name="out_proj",
    )(x, oa, orr, w_out_bf16, w_out_bf16, gain.reshape(1, d), w_xq_bf16)


def _xattn_kernel(qx_ref, k_ref, v_ref, x1_ref, wo_ref, g_ref, wpq_ref, x2_ref, pq_ref):
    heads = []
    for hh in range(XATT_HEADS):
        cols = slice(hh * XATT_HEAD_DIM, (hh + 1) * XATT_HEAD_DIM)
        s = _dot_nt(qx_ref[:, cols], k_ref[:, cols]) * (XATT_HEAD_DIM ** -0.5)
        p = jnp.exp(s - jnp.max(s, axis=-1, keepdims=True))
        p = p / jnp.sum(p, axis=-1, keepdims=True)
        heads.append(_dot(p.astype(BF16), v_ref[:, cols]))
    o = jnp.concatenate(heads, axis=1).astype(BF16)
    x2 = x1_ref[...] + _dot(o, wo_ref[...])
    x2_ref[...] = x2
    pq_ref[...] = _dot(_rms(x2, g_ref[...]).astype(BF16), wpq_ref[...])


def _xattn(qx, kv_mem, x1, w_xo_bf16, gain, w_pq_bf16, batch, seq, mem_len, tm, batch0=0):
    t, d = x1.shape
    npq = w_pq_bf16.shape[1]
    nt = seq // tm
    rows = lambda width: pl.BlockSpec((tm, width), lambda b, i: (b * nt + i, 0))
    full = lambda shape: pl.BlockSpec(shape, lambda b, i: (0, 0))
    return pl.pallas_call(
        _xattn_kernel,
        grid=(batch, nt),
        in_specs=[
            rows(XATT_WIDTH),
            pl.BlockSpec((mem_len, XATT_WIDTH), lambda b, i: (b + batch0, 0)),
            pl.BlockSpec((mem_len, XATT_WIDTH), lambda b, i: (b + batch0, 1)),
            rows(d),
            full((XATT_WIDTH, d)),
            full((1, d)),
            full((d, npq)),
        ],
        out_specs=[rows(d), rows(npq)],
        out_shape=[jax.ShapeDtypeStruct((t, d), F32), jax.ShapeDtypeStruct((t, npq), F32)],
        compiler_params=_params("parallel", "parallel"),
        name="cross_attention",
    )(qx, kv_mem, kv_mem, x1, w_xo_bf16, gain.reshape(1, d), w_pq_bf16)


def _topk_rows(s, k, payload=None):
    nrows = s.shape[0]
    rows = lax.broadcasted_iota(jnp.int32, s.shape, 0).astype(F32)
    vals, sel = [], []
    for _ in range(k):
        m = jnp.max(s, axis=0, keepdims=True)
        am = jnp.min(jnp.where(s == m, rows, float(nrows)), axis=0, keepdims=True)
        hit = rows == am
        vals.append(m)
        sel.append(am if payload is None else jnp.max(jnp.where(hit, payload, -1.0), axis=0, keepdims=True))
        s = jnp.where(hit, -jnp.inf, s)
    return jnp.concatenate(vals, axis=0), jnp.concatenate(sel, axis=0)


def _staircase(t1, t2):
    kk, half = PEER_TOPK, SUBLANES
    blocks = [(t1[0:1, :], t2, kk), (t1[1:2, :], t2[0:half, :], half)]
    blocks += [(t1[a:a + 1, :], t2[0:half, :], kk // (a + 1)) for a in range(2, half)]
    blocks.append((t1[half:kk, :], t2[0:1, :], half))
    return blocks


def _peer_topk_kernel(q_ref, keys_ref, ids_ref, gates_ref, *, heads):
    width = 2 * PEER_HALF_DIM
    row8 = lax.broadcasted_iota(jnp.int32, (SUBLANES, q_ref.shape[0]), 0)
    for hh in range(heads):
        tops = []
        for p in range(2):
            c0 = hh * width + p * PEER_HALF_DIM
            qp = q_ref[:, c0:c0 + PEER_HALF_DIM].astype(BF16)
            tops.append(_topk_rows(_dot_nt(keys_ref[hh, p], qp), PEER_TOPK))
        (s1, i1), (s2, i2) = tops
        cand_s, cand_i = [], []
        for (a_s, b_s, live), (a_i, b_i, _) in zip(_staircase(s1, s2), _staircase(i1 * float(PEER_N_KEYS), i2)):
            blk = a_s + b_s
            if live < blk.shape[0]:
                blk = jnp.where(row8 < live, blk, -jnp.inf)
            cand_s.append(blk)
            cand_i.append(a_i + b_i)
        top_s, top_e = _topk_rows(jnp.concatenate(cand_s, axis=0), PEER_TOPK,
                                  payload=jnp.concatenate(cand_i, axis=0))
        e = jnp.exp(top_s - top_s[0:1, :])
        ids_ref[hh] = top_e.astype(jnp.int32)
        gates_ref[hh] = e / jnp.sum(e, axis=0, keepdims=True)


def _peer_topk(pq, keys_bf16, tt, heads):
    t = pq.shape[0]
    width = 2 * PEER_HALF_DIM
    out = pl.BlockSpec((heads, PEER_TOPK, tt), lambda i, hh: (hh, 0, i))
    return pl.pallas_call(
        functools.partial(_peer_topk_kernel, heads=heads),
        grid=(t // tt, PEER_HEADS // heads),
        in_specs=[
            pl.BlockSpec((tt, heads * width), lambda i, hh: (i, hh)),
            pl.BlockSpec((heads, 2, PEER_N_KEYS, PEER_HALF_DIM), lambda i, hh: (hh, 0, 0, 0)),
        ],
        out_specs=[out, out],
        out_shape=[jax.ShapeDtypeStruct((PEER_HEADS, PEER_TOPK, t), jnp.int32),
                   jax.ShapeDtypeStruct((PEER_HEADS, PEER_TOPK, t), F32)],
        compiler_params=_params("parallel", "parallel"),
        name="peer_topk",
    )(pq, keys_bf16)


GATHER_TOKENS = 128
GATHER_BUFS = 8
GATHER_GROUP = 8
assert GATHER_GROUP % GATHER_BUFS == 0 and GATHER_GROUP % SUBLANES == 0 and GATHER_TOKENS % GATHER_GROUP == 0


def _peer_gather_kernel(ids_hbm, tbl_hbm, x2_ref, gates_ref, fg_ref, og_ref, *rest, final_norm, first_block,
                        has_prev):
    out_ref, ids_smem, ids_sem, *scratch = rest[1:] if has_prev else rest
    gbufs = scratch[:GATHER_BUFS]
    gsem, xn_ref, y_ref, yacc_ref = scratch[GATHER_BUFS:]
    i = pl.program_id(0)
    has_next = i + 1 < pl.num_programs(0)
    slot = i % 2
    other = 1 - slot
    tb = GATHER_TOKENS
    nchunk = D_MODEL // LANES
    nids = tb * PEER_SLOTS
    ahead = GATHER_BUFS - 1
    group = GATHER_GROUP

    def ids_copy(step, sl):
        return pltpu.make_async_copy(ids_hbm.at[pl.ds(pl.multiple_of((first_block + step) * nids, nids), nids)],
                                     ids_smem.at[pl.ds(pl.multiple_of(sl * nids, nids), nids)], ids_sem.at[sl])

    def row_copy(e, b, k):
        return pltpu.make_async_copy(tbl_hbm.at[e], gbufs[b].at[pl.ds(k, 1)], gsem.at[b])

    def issue(sl, tok, b):
        base = sl * nids + tok * PEER_SLOTS
        for k in range(PEER_SLOTS):
            row_copy(ids_smem[base + k], b, k).start(priority=k % 2)

    def wait_rows(b):
        for k in range(PEER_SLOTS):
            row_copy(0, b, k).wait()

    @pl.when(i == 0)
    def _():
        ids_copy(0, 0).start()
        ids_copy(0, 0).wait()
        for tok in range(ahead):
            issue(0, tok, tok)

    @pl.when(has_next)
    def _():
        ids_copy(i + 1, other).start()

    xn_ref[...] = _rms(x2_ref[...], fg_ref[...])

    lane = lax.broadcasted_iota(jnp.int32, (PEER_SLOTS, LANES), 1)
    hi_mask = jnp.uint32(0xFFFF0000)

    def compute(tok, r, b):
        gbuf = gbufs[b]
        xrow = xn_ref[pl.ds(tok, 1), :]
        acc = jnp.zeros((PEER_SLOTS, LANES), F32)
        for c in range(nchunk):
            w = gbuf[:, c * LANES:(c + 1) * LANES]
            u = lax.bitcast_convert_type(w << 16, F32)
            acc = acc + u * xrow[:, c * LANES:(c + 1) * LANES]
        a = jnp.sum(acc, axis=-1, keepdims=True)
        gate = jnp.sum(jnp.where(lane == tok, gates_ref[...], 0.0), axis=-1, keepdims=True)
        coeff = 0.5 * a * (1.0 + lax.erf(a * (2.0 ** -0.5))) * gate
        for c in range(nchunk):
            w = gbuf[:, c * LANES:(c + 1) * LANES]
            vv = lax.bitcast_convert_type(w & hi_mask, F32)
            yacc_ref[r:r + 1, c * LANES:(c + 1) * LANES] = jnp.sum(vv * coeff, axis=0, keepdims=True)

    def run_group(g, last):
        for r in range(group):
            b = r % GATHER_BUFS
            nb = (r + ahead) % GATHER_BUFS
            tok = g * group + r
            wait_rows(b)
            if not last or r + ahead < group:
                issue(slot, tok + ahead, nb)
            else:
                if r + ahead == group:
                    @pl.when(has_next)
                    def _():
                        ids_copy(i + 1, other).wait()

                @pl.when(has_next)
                def _():
                    issue(other, r + ahead - group, nb)
            compute(tok, r, b)
        y_ref[pl.ds(pl.multiple_of(g * group, group), group), :] = yacc_ref[...]

    def body(g, carry):
        run_group(g, last=False)
        return carry

    ngroups = tb // group
    lax.fori_loop(0, ngroups - 1, body, 0)
    run_group(ngroups - 1, last=True)
    x3 = x2_ref[...] + y_ref[...]
    out_ref[...] = _rms(x3, og_ref[...]) if final_norm else x3


def _peer_gather(ids_flat, table, x2, gates_t, ffn_gain, final_gain, final_norm, first_block, out_rows,
                 out_block0, prev):
    t, d = x2.shape
    tb = GATHER_TOKENS
    nsteps = t // tb - first_block
    full = lambda shape: pl.BlockSpec(shape, lambda i: (0, 0))
    has_prev = prev is not None
    return pl.pallas_call(
        functools.partial(_peer_gather_kernel, final_norm=final_norm, first_block=first_block, has_prev=has_prev),
        grid=(nsteps,),
        in_specs=[
            pl.BlockSpec(memory_space=pl.ANY),
            pl.BlockSpec(memory_space=pl.ANY),
            pl.BlockSpec((tb, d), lambda i: (i + first_block, 0)),
            pl.BlockSpec((PEER_SLOTS, tb), lambda i: (0, i + first_block)),
            full((1, d)),
            full((1, d)),
        ] + ([pl.BlockSpec(memory_space=pl.ANY)] if has_prev else []),
        out_specs=pl.BlockSpec((tb, d), lambda i: (i + first_block + out_block0, 0)),
        out_shape=jax.ShapeDtypeStruct((out_rows, d), F32),
        input_output_aliases={6: 0} if has_prev else {},
        scratch_shapes=[
            pltpu.SMEM((2 * tb * PEER_SLOTS,), jnp.int32),
            pltpu.SemaphoreType.DMA((2,)),
            *[pltpu.VMEM((PEER_SLOTS, d), jnp.uint32) for _ in range(GATHER_BUFS)],
            pltpu.SemaphoreType.DMA((GATHER_BUFS,)),
            pltpu.VMEM((tb, d), F32),
            pltpu.VMEM((tb, d), F32),
            pltpu.VMEM((GATHER_GROUP, d), F32),
        ],
        compiler_params=_params("arbitrary"),
        name="peer_gather",
    )(ids_flat, table, x2, gates_t, ffn_gain.reshape(1, d), final_gain.reshape(1, d), *([prev] if has_prev else []))


SC_CORES, SC_SUBCORES, SC_LANES = 2, 16, 16
SC_WORKERS = SC_CORES * SC_SUBCORES
SC_ROWS = SC_LANES
SC_COLS = 8
SC_DOT_ROWS = 8
SC_PARTS = 2
SC_SHARE = (25, 64)


def _sc_erf(x):
    ax = jnp.abs(x)
    t = 1.0 / (1.0 + 0.3275911 * ax)
    poly = t * (0.254829592 + t * (-0.284496736 + t * (1.421413741 + t * (-1.453152027 + t * 1.061405429))))
    y = 1.0 - poly * jnp.exp(-ax * ax)
    return jnp.where(x < 0, -y, y)


def _peer_sc(ids_flat, gates_tok, xn, table, ts):
    d = D_MODEL
    lanes, rows = SC_LANES, SC_ROWS
    nchunk = PEER_SLOTS // rows
    block = lanes * SC_COLS
    per_worker = ts // SC_WORKERS
    assert ts % SC_WORKERS == 0 and d % block == 0
    mesh = plsc.VectorSubcoreMesh(core_axis_name="c", subcore_axis_name="s")

    @functools.partial(
        pl.kernel, mesh=mesh,
        out_type=jax.ShapeDtypeStruct((ts, d), F32),
        scratch_types=[
            pltpu.VMEM((PEER_SLOTS,), jnp.int32),
            pltpu.VMEM((PEER_SLOTS,), F32),
            pltpu.VMEM((d,), F32),
            pltpu.VMEM((d,), F32),
            pltpu.VMEM((2, rows, 1, d), jnp.uint32),
            pltpu.VMEM((rows, lanes), F32),
            pltpu.VMEM((lanes,), F32),
            pltpu.VMEM((rows, lanes), jnp.int32),
            pltpu.SemaphoreType.DMA((2,)),
        ],
        compiler_params=pltpu.CompilerParams(needs_layout_passes=False),
        name="peer_sparsecore",
    )
    def run(ids_hbm, gates_hbm, xn_hbm, tbl_hbm, y_hbm, ids_v, gates_v, x_v, y_v, rows_v, acc_v, coeff_v, ridx_v, sem):
        worker = lax.axis_index("s") * SC_CORES + lax.axis_index("c")
        lane = lax.iota(jnp.int32, lanes)
        zero = jnp.zeros((lanes,), F32)
        hi_mask = jnp.full((lanes,), 0xFFFF0000, jnp.uint32)
        for r in range(rows):
            ridx_v[r, :] = jnp.full((lanes,), r, jnp.int32)

        def gather(c, slot):
            return pltpu.make_async_copy(tbl_hbm.at[ids_v[pl.ds(c * rows, rows)]], rows_v.at[slot], sem.at[slot])

        @pl.loop(0, per_worker)
        def _(i):
            tok = worker * per_worker + i
            pltpu.sync_copy(ids_hbm.at[pl.ds(tok * PEER_SLOTS, PEER_SLOTS)], ids_v)
            pltpu.sync_copy(gates_hbm.at[pl.ds(tok * PEER_SLOTS, PEER_SLOTS)], gates_v)
            pltpu.sync_copy(xn_hbm.at[tok], x_v)

            @pl.loop(0, d // lanes)
            def _(j):
                y_v[pl.ds(j * lanes, lanes)] = zero

            gather(0, 0).start()
            for c in range(nchunk):
                slot = c % 2
                if c + 1 < nchunk:
                    gather(c + 1, 1 - slot).start()
                gather(c, slot).wait()
                for r0 in range(0, rows, SC_DOT_ROWS):
                    def dot_body(j, accs, r0=r0, slot=slot):
                        xj = x_v[pl.ds(j * lanes, lanes)]
                        return tuple(
                            accs[q] + plsc.bitcast(rows_v[slot, r0 + q, 0, pl.ds(j * lanes, lanes)] << 16, F32) * xj
                            for q in range(SC_DOT_ROWS))
                    accs = lax.fori_loop(0, d // lanes, dot_body, (zero,) * SC_DOT_ROWS, unroll=4)
                    for q in range(SC_DOT_ROWS):
                        acc_v[r0 + q, :] = accs[q]
                a = zero
                for col in range(lanes):
                    a = a + plsc.load_gather(acc_v, [lane, ridx_v[col, :]])
                gate = gates_v[pl.ds(c * rows, rows)]
                coeff_v[...] = 0.5 * a * (1.0 + _sc_erf(a * (2.0 ** -0.5))) * gate
                coeffs = [plsc.load_gather(coeff_v, [ridx_v[r, :]]) for r in range(rows)]

                @pl.loop(0, d // block)
                def _(jb, slot=slot, coeffs=coeffs):
                    base = jb * block
                    accs = [zero] * SC_COLS
                    for r in range(rows):
                        for q in range(SC_COLS):
                            w = rows_v[slot, r, 0, pl.ds(base + q * lanes, lanes)]
                            accs[q] = accs[q] + plsc.bitcast(w & hi_mask, F32) * coeffs[r]
                    for q in range(SC_COLS):
                        plsc.addupdate(y_v.at[pl.ds(base + q * lanes, lanes)], accs[q])

            pltpu.sync_copy(y_v, y_hbm.at[tok])

    return run(ids_flat, gates_tok, xn, table)


def _rms_rows_kernel(x_ref, g_ref, o_ref):
    o_ref[...] = _rms(x_ref[...], g_ref[...])


def _rms_rows(x, gain, rows, tm):
    d = x.shape[1]
    return pl.pallas_call(
        _rms_rows_kernel,
        grid=(rows // tm,),
        in_specs=[pl.BlockSpec((tm, d), lambda i: (i, 0)), pl.BlockSpec((1, d), lambda i: (0, 0))],
        out_specs=pl.BlockSpec((tm, d), lambda i: (i, 0)),
        out_shape=jax.ShapeDtypeStruct((rows, d), F32),
        compiler_params=_params("parallel"),
        name="rms_rows",
    )(x, gain.reshape(1, d))


def _peer_finish_kernel(x2_ref, y_ref, og_ref, prev_ref, out_ref, *, final_norm):
    del prev_ref
    x3 = x2_ref[...] + y_ref[...]
    out_ref[...] = _rms(x3, og_ref[...]) if final_norm else x3


def _peer_finish(x2, y_head, out_rest, final_gain, final_norm, out_block0):
    rows, d = y_head.shape
    tm = GATHER_TOKENS
    blk = pl.BlockSpec((tm, d), lambda i: (i, 0))
    return pl.pallas_call(
        functools.partial(_peer_finish_kernel, final_norm=final_norm),
        grid=(rows // tm,),
        in_specs=[blk, blk, pl.BlockSpec((1, d), lambda i: (0, 0)), pl.BlockSpec(memory_space=pl.ANY)],
        out_specs=pl.BlockSpec((tm, d), lambda i: (i + out_block0, 0)),
        out_shape=jax.ShapeDtypeStruct(out_rest.shape, F32),
        input_output_aliases={3: 0},
        compiler_params=_params("parallel"),
        name="peer_finish",
    )(x2, y_head, final_gain.reshape(1, d), out_rest)


def _rope_angles(positions, dh):
    inv_freq = ROPE_THETA ** (-jnp.arange(0, dh, 2, dtype=F32) / dh)
    ang = positions.astype(F32).reshape(-1, 1) * inv_freq
    return jnp.cos(ang), jnp.sin(ang)


def _rope_tables_att(positions):
    cos, sin = _rope_angles(positions, HEAD_DIM)
    copies = LANES // HEAD_DIM
    return (jnp.tile(jnp.concatenate([cos, cos], axis=1), (1, copies)),
            jnp.tile(jnp.concatenate([-sin, sin], axis=1), (1, copies)))


def _pack_table_kernel(u_ref, v_ref, o_ref):
    ub = lax.bitcast_convert_type(u_ref[...].astype(BF16).astype(F32), jnp.uint32)
    vb = lax.bitcast_convert_type(v_ref[...].astype(BF16).astype(F32), jnp.uint32)
    o_ref[:, 0, :] = (ub >> 16) | vb


def _pack_expert_table(u, v, rows=256):
    n, d = u.shape
    blk = pl.BlockSpec((rows, d), lambda i: (i, 0))
    return pl.pallas_call(
        _pack_table_kernel,
        grid=(n // rows,),
        in_specs=[blk, blk],
        out_specs=pl.BlockSpec((rows, 1, d), lambda i: (i, 0, 0)),
        out_shape=jax.ShapeDtypeStruct((n, 1, d), jnp.uint32),
        compiler_params=_params("parallel"),
        name="pack_table",
    )(u, v)


def kernel(x, mem, positions, mix_norm_gain, w_in, att_sinks, att_out_gain, ret_out_gain, w_out,
           cross_norm_gain, mem_norm_gain, w_xq, w_xk, w_xv, w_xo, ffn_norm_gain,
           w_peer_q, peer_sub_keys, peer_u, peer_v, final_norm_gain):
    batch, seq, d = x.shape
    mem_len = mem.shape[1]
    t = batch * seq
    depth = w_in.shape[0]
    assert d == D_MODEL and seq % BLOCK == 0 and t % GATHER_TOKENS == 0

    cos_a, sin_a = _rope_tables_att(positions)
    cos_r, sin_r = _rope_angles(positions, RET_HEAD_DIM)
    xf = x.reshape(t, d)
    memf = mem.reshape(batch * mem_len, d)
    nblocks = t // GATHER_TOKENS
    parts = SC_PARTS if (batch % SC_PARTS == 0 and nblocks % SC_SHARE[1] == 0) else 1
    bp, tp = batch // parts, t // parts
    blocks_p = tp // GATHER_TOKENS
    sc_total = nblocks * SC_SHARE[0] // SC_SHARE[1] if parts > 1 else 0
    tm_in = min(1024, tp)
    tm_mid = min(256, seq)
    for l in range(depth):
        sc_left = sc_total
        w_in_b = _reorder_w_in(w_in[l]).astype(BF16)
        w_out_b, w_xq_b, w_xo_b, w_pq_b = (w.astype(BF16) for w in (w_out[l], w_xq[l], w_xo[l], w_peer_q[l]))
        keys_b = peer_sub_keys[l].astype(BF16)
        w_kv = jnp.concatenate([w_xk[l], w_xv[l]], axis=1).astype(BF16)
        kv_mem = _norm_matmul(memf, mem_norm_gain[l], w_kv, mem_len, 2 * XATT_WIDTH, BF16)
        table = _pack_expert_table(peer_u[l], peer_v[l])
        last = l == depth - 1
        out = None
        pending = []
        for p in range(parts):
            r0 = p * tp
            rows = slice(r0, r0 + tp)
            h = _norm_matmul(xf, mix_norm_gain[l], w_in_b, tm_in, 1792, BF16, row0=r0, rows=tp)
            oa = _swa(h, cos_a[rows], sin_a[rows], att_sinks[l], att_out_gain[l], bp, seq)
            orr = _retention(h, cos_r[rows], sin_r[rows], ret_out_gain[l], bp, seq)
            x1, qx = _outproj(xf, oa, orr, w_out_b, cross_norm_gain[l], w_xq_b, tm_mid, row0=r0)
            x2, pq = _xattn(qx, kv_mem, x1, w_xo_b, ffn_norm_gain[l], w_pq_b, bp, seq, mem_len, tm_mid,
                            batch0=p * bp)
            ids_t, gates_t = _peer_topk(pq, keys_b, LANES, PEER_HEADS)
            ids = ids_t.reshape(PEER_SLOTS, tp).T.reshape(tp * PEER_SLOTS)
            gates_t = gates_t.reshape(PEER_SLOTS, tp)
            sc_blocks = min(sc_left, blocks_p)
            sc_left -= sc_blocks
            if sc_blocks:
                ts = sc_blocks * GATHER_TOKENS
                xn_head = _rms_rows(x2, ffn_norm_gain[l], ts, GATHER_TOKENS)
                y_head = _peer_sc(ids, gates_t[:, :ts].T.reshape(ts * PEER_SLOTS), xn_head, table, ts)
                pending.append((x2, y_head, p * blocks_p))
            if sc_blocks < blocks_p:
                out = _peer_gather(ids, table, x2, gates_t, ffn_norm_gain[l], final_norm_gain, last, sc_blocks,
                                   t, p * blocks_p, out)
        for x2, y_head, block0 in pending:
            out = _peer_finish(x2, y_head, out, final_norm_gain, last, block0)
        xf = out
    return xf.reshape(batch, seq, d)
```

```python
import functools
import math

import jax
import jax.numpy as jnp
from jax import lax
from jax.experimental import pallas as pl
from jax.experimental.pallas import tpu as pltpu
from jax.experimental.pallas import tpu_sc as plsc

F32 = jnp.float32
BF16 = jnp.bfloat16

EPS = 1e-6
GN_EPS = 1e-5
ROPE_THETA = 10000.0
D_MODEL = 2048
HEAD_DIM = 64
ATT_Q_HEADS = 16
ATT_KV_HEADS = 2
ATT_WIDTH = ATT_Q_HEADS * HEAD_DIM
ATT_KV_WIDTH = ATT_KV_HEADS * HEAD_DIM
BLOCK = 128
RET_HEADS = 4
RET_HEAD_DIM = 256
RET_WIDTH = RET_HEADS * RET_HEAD_DIM
RET_CHUNK = 128
IN_COLS = ATT_WIDTH + 2 * ATT_KV_WIDTH + 4 * RET_WIDTH
XATT_HEADS = 4
XATT_HEAD_DIM = 128
XATT_WIDTH = XATT_HEADS * XATT_HEAD_DIM
PEER_HEADS = 8
PEER_N_KEYS = 128
PEER_HALF_DIM = 128
PEER_TOPK = 16
PEER_SLOTS = PEER_HEADS * PEER_TOPK

LANES = 128
SUBLANES = 8
VMEM_LIMIT = 56 * 1024 * 1024

_REF_QR0 = ATT_WIDTH + 2 * ATT_KV_WIDTH
_QR0 = 0
_KR0 = _QR0 + RET_WIDTH
_VR0 = _KR0 + RET_WIDTH
_GR0 = _VR0 + RET_WIDTH
_QA0 = _GR0 + RET_WIDTH
_KA0 = _QA0 + ATT_WIDTH
_VA0 = _KA0 + ATT_KV_WIDTH


def _reorder_w_in(w):
    return jnp.concatenate([w[:, _REF_QR0:], w[:, :_REF_QR0]], axis=1)


def _params(*sem):
    return pltpu.CompilerParams(dimension_semantics=sem, vmem_limit_bytes=VMEM_LIMIT)


def _rms(xf, gain):
    ms = jnp.mean(xf * xf, axis=-1, keepdims=True)
    return xf * lax.rsqrt(ms + EPS) * gain


def _dot(a, b):
    return jnp.dot(a, b, preferred_element_type=F32)


def _dot_nt(a, b):
    return lax.dot_general(a, b, (((1,), (1,)), ((), ())), preferred_element_type=F32)


def _norm_matmul_kernel(x_ref, g_ref, w_ref, o_ref, xn_ref):
    @pl.when(pl.program_id(1) == 0)
    def _():
        xn_ref[...] = _rms(x_ref[...], g_ref[...]).astype(BF16)

    o_ref[...] = _dot(xn_ref[...], w_ref[...]).astype(o_ref.dtype)


def _norm_matmul(x, gain, w_bf16, tm, tn, out_dtype, row0=0, rows=None):
    d = x.shape[1]
    t = x.shape[0] if rows is None else rows
    n = w_bf16.shape[1]
    blk0 = row0 // tm
    return pl.pallas_call(
        _norm_matmul_kernel,
        grid=(t // tm, n // tn),
        in_specs=[
            pl.BlockSpec((tm, d), lambda i, j: (i + blk0, 0)),
            pl.BlockSpec((1, d), lambda i, j: (0, 0)),
            pl.BlockSpec((d, tn), lambda i, j: (0, j)),
        ],
        out_specs=pl.BlockSpec((tm, tn), lambda i, j: (i, j)),
        out_shape=jax.ShapeDtypeStruct((t, n), out_dtype),
        scratch_shapes=[pltpu.VMEM((tm, d), BF16)],
        compiler_params=_params("parallel", "arbitrary"),
        name="norm_matmul",
    )(x, gain.reshape(1, d), w_bf16)


def _swa_kernel(sink_ref, q_ref, kc_ref, kp_ref, vc_ref, vp_ref, cc_ref, sc_ref, cp_ref, sp_ref,
                gain_ref, o_ref):
    n = pl.program_id(1)
    lane = lax.broadcasted_iota(jnp.int32, (1, LANES), 1)
    first_half = (lane % HEAD_DIM) < (HEAD_DIM // 2)
    lo = lane < HEAD_DIM

    def rope(x, c, s):
        partner = jnp.where(first_half, pltpu.roll(x, LANES - HEAD_DIM // 2, 1),
                            pltpu.roll(x, HEAD_DIM // 2, 1))
        return x * c + partner * s

    cc, sc = cc_ref[...], sc_ref[...]
    k = jnp.concatenate([rope(kp_ref[...].astype(F32), cp_ref[...], sp_ref[...]),
                         rope(kc_ref[...].astype(F32), cc, sc)], axis=0)
    v = jnp.concatenate([vp_ref[...].astype(F32), vc_ref[...].astype(F32)], axis=0)
    k_sw = pltpu.roll(k, HEAD_DIM, 1)
    v_sw = pltpu.roll(v, HEAD_DIM, 1)

    def place(a, a_sw, c, half):
        src = a if c == half else a_sw
        keep = lo if half == 0 else jnp.logical_not(lo)
        return jnp.where(keep, src, 0.0).astype(BF16)

    kvar = {(c, h): place(k, k_sw, c, h) for c in range(ATT_KV_HEADS) for h in range(2)}
    vvar = {(c, h): place(v, v_sw, c, h) for c in range(ATT_KV_HEADS) for h in range(2)}

    qi = lax.broadcasted_iota(jnp.int32, (BLOCK, 2 * BLOCK), 0) + BLOCK
    ki = lax.broadcasted_iota(jnp.int32, (BLOCK, 2 * BLOCK), 1)
    dist = qi - ki
    kmin = jnp.where(n > 0, 0, BLOCK)
    valid = (dist >= 0) & (dist < BLOCK) & (ki >= kmin)

    group = ATT_Q_HEADS // ATT_KV_HEADS
    for j in range(ATT_WIDTH // LANES):
        c = (2 * j) // group
        cols = slice(j * LANES, (j + 1) * LANES)
        qg = (rope(q_ref[:, cols].astype(F32), cc, sc) * (HEAD_DIM ** -0.5)).astype(BF16)
        out = jnp.zeros((BLOCK, LANES), F32)
        for half in range(2):
            s = jnp.where(valid, _dot_nt(qg, kvar[(c, half)]), -jnp.inf)
            sink = sink_ref[2 * j + half]
            m = jnp.maximum(jnp.max(s, axis=-1, keepdims=True), sink)
            p = jnp.exp(s - m)
            denom = jnp.sum(p, axis=-1, keepdims=True) + jnp.exp(sink - m)
            out = out + _dot((p / denom).astype(BF16), vvar[(c, half)])
        sq = out * out
        ss_lo = jnp.sum(jnp.where(lo, sq, 0.0), axis=-1, keepdims=True)
        ss_hi = jnp.sum(jnp.where(lo, 0.0, sq), axis=-1, keepdims=True)
        ms = jnp.where(lo, ss_lo, ss_hi) * (1.0 / HEAD_DIM)
        o_ref[:, cols] = (out * lax.rsqrt(ms + EPS) * gain_ref[:, cols]).astype(o_ref.dtype)


def _swa(h, cos_a, sin_a, sinks, gain, batch, seq):
    nb = seq // BLOCK
    t = batch * seq
    kcol, vcol = _KA0 // LANES, _VA0 // LANES
    cur = lambda b, n: b * nb + n
    prev = lambda b, n: b * nb + jnp.maximum(n - 1, 0)
    row = lambda col, f: pl.BlockSpec((BLOCK, LANES), lambda b, n: (f(b, n), col))
    return pl.pallas_call(
        _swa_kernel,
        grid=(batch, nb),
        in_specs=[
            pl.BlockSpec(memory_space=pltpu.SMEM),
            pl.BlockSpec((BLOCK, ATT_WIDTH), lambda b, n: (cur(b, n), _QA0 // ATT_WIDTH)),
            row(kcol, cur), row(kcol, prev), row(vcol, cur), row(vcol, prev),
            row(0, cur), row(0, cur), row(0, prev), row(0, prev),
            pl.BlockSpec((1, ATT_WIDTH), lambda b, n: (0, 0)),
        ],
        out_specs=pl.BlockSpec((BLOCK, ATT_WIDTH), lambda b, n: (cur(b, n), 0)),
        out_shape=jax.ShapeDtypeStruct((t, ATT_WIDTH), BF16),
        compiler_params=_params("parallel", "parallel"),
        name="swa_attention",
    )(sinks, h, h, h, h, h, cos_a, sin_a, cos_a, sin_a, gain.reshape(1, ATT_WIDTH))


def _ret_kernel(lg_ref, cd_ref, q_ref, k_ref, v_ref, g_ref, c_ref, s_ref, gain_ref, o_ref, state_ref):
    n = pl.program_id(1)

    @pl.when(n == 0)
    def _():
        state_ref[...] = jnp.zeros_like(state_ref)

    c, s = c_ref[...], s_ref[...]
    hd = RET_HEAD_DIM
    half = hd // 2

    def rope(x):
        x1, x2 = x[:, :half], x[:, half:]
        return jnp.concatenate([x1 * c - x2 * s, x2 * c + x1 * s], axis=1)

    ri = lax.broadcasted_iota(jnp.int32, (RET_CHUNK, RET_CHUNK), 0).astype(F32)
    ci = lax.broadcasted_iota(jnp.int32, (RET_CHUNK, RET_CHUNK), 1).astype(F32)
    diff = ri - ci
    for hh in range(RET_HEADS):
        cols = slice(hh * hd, (hh + 1) * hd)
        lg = lg_ref[hh]
        q = rope(q_ref[:, cols].astype(F32))
        k = rope(k_ref[:, cols].astype(F32)) * (hd ** -0.5)
        v = v_ref[:, cols].astype(BF16)

        decay = jnp.where(diff >= 0, jnp.exp(jnp.maximum(diff, 0.0) * lg), 0.0)
        zeta = jnp.exp((RET_CHUNK - 1 - ri) * lg)
        xi = jnp.exp((ri + 1.0) * lg)
        zeta2 = jnp.concatenate([zeta, zeta], axis=1)
        xi2 = jnp.concatenate([xi, xi], axis=1)

        inner = _dot_nt(q.astype(BF16), k.astype(BF16)) * decay
        state = state_ref[hh]
        out = _dot(inner.astype(BF16), v) + _dot((q * xi2).astype(BF16), state.astype(BF16))
        kv = lax.dot_general((k * zeta2).astype(BF16), v, (((0,), (0,)), ((), ())), preferred_element_type=F32)
        state_ref[hh] = state * cd_ref[hh] + kv

        mu = jnp.mean(out, axis=-1, keepdims=True)
        cen = out - mu
        var = jnp.mean(cen * cen, axis=-1, keepdims=True)
        g = g_ref[:, cols].astype(F32)
        o = cen * lax.rsqrt(var + GN_EPS) * gain_ref[:, cols] * (g * jax.nn.sigmoid(g))
        o_ref[:, cols] = o.astype(o_ref.dtype)


def _retention(h, cos_r, sin_r, gain, batch, seq):
    nc = seq // RET_CHUNK
    t = batch * seq
    hd = RET_HEAD_DIM
    log_gamma = jnp.log1p(-jnp.exp2(-5.0 - jnp.arange(RET_HEADS, dtype=F32)))
    chunk_decay = jnp.exp(RET_CHUNK * log_gamma)
    col = lambda c0: pl.BlockSpec((RET_CHUNK, RET_WIDTH), lambda b, n: (b * nc + n, c0 // RET_WIDTH))
    tab = pl.BlockSpec((RET_CHUNK, hd // 2), lambda b, n: (b * nc + n, 0))
    return pl.pallas_call(
        _ret_kernel,
        grid=(batch, nc),
        in_specs=[
            pl.BlockSpec(memory_space=pltpu.SMEM),
            pl.BlockSpec(memory_space=pltpu.SMEM),
            col(_QR0), col(_KR0), col(_VR0), col(_GR0), tab, tab,
            pl.BlockSpec((1, RET_WIDTH), lambda b, n: (0, 0)),
        ],
        out_specs=pl.BlockSpec((RET_CHUNK, RET_WIDTH), lambda b, n: (b * nc + n, 0)),
        out_shape=jax.ShapeDtypeStruct((t, RET_WIDTH), BF16),
        scratch_shapes=[pltpu.VMEM((RET_HEADS, hd, hd), F32)],
        compiler_params=_params("parallel", "arbitrary"),
        name="retention",
    )(log_gamma, chunk_decay, h, h, h, h, cos_r, sin_r, gain.reshape(1, RET_WIDTH))


def _outproj_kernel(x_ref, oa_ref, or_ref, wa_ref, wr_ref, g_ref, wq_ref, x1_ref, qx_ref):
    x1 = x_ref[...] + _dot(oa_ref[...], wa_ref[...]) + _dot(or_ref[...], wr_ref[...])
    x1_ref[...] = x1
    qx_ref[...] = _dot(_rms(x1, g_ref[...]).astype(BF16), wq_ref[...]).astype(qx_ref.dtype)


def _outproj(x, oa, orr, w_out_bf16, gain, w_xq_bf16, tm, row0=0):
    t, d = oa.shape[0], x.shape[1]
    blk0 = row0 // tm
    full = lambda shape: pl.BlockSpec(shape, lambda i: (0, 0))
    return pl.pallas_call(
        _outproj_kernel,
        grid=(t // tm,),
        in_specs=[
            pl.BlockSpec((tm, d), lambda i: (i + blk0, 0)),
            pl.BlockSpec((tm, ATT_WIDTH), lambda i: (i, 0)),
            pl.BlockSpec((tm, RET_WIDTH), lambda i: (i, 0)),
            pl.BlockSpec((ATT_WIDTH, d), lambda i: (0, 0)),
            pl.BlockSpec((RET_WIDTH, d), lambda i: (1, 0)),
            full((1, d)),
            full((d, XATT_WIDTH)),
        ],
        out_specs=[pl.BlockSpec((tm, d), lambda i: (i, 0)), pl.BlockSpec((tm, XATT_WIDTH), lambda i: (i, 0))],
        out_shape=[jax.ShapeDtypeStruct((t, d), F32), jax.ShapeDtypeStruct((t, XATT_WIDTH), BF16)],
        compiler_params=_params("parallel"),
        name="out_proj",
    )(x, oa, orr, w_out_bf16, w_out_bf16, gain.reshape(1, d), w_xq_bf16)


def _xattn_kernel(qx_ref, k_ref, v_ref, x1_ref, wo_ref, g_ref, wpq_ref, x2_ref, pq_ref):
    heads = []
    for hh in range(XATT_HEADS):
        cols = slice(hh * XATT_HEAD_DIM, (hh + 1) * XATT_HEAD_DIM)
        s = _dot_nt(qx_ref[:, cols], k_ref[:, cols]) * (XATT_HEAD_DIM ** -0.5)
        p = jnp.exp(s - jnp.max(s, axis=-1, keepdims=True))
        p = p / jnp.sum(p, axis=-1, keepdims=True)
        heads.append(_dot(p.astype(BF16), v_ref[:, cols]))
    o = jnp.concatenate(heads, axis=1).astype(BF16)
    x2 = x1_ref[...] + _dot(o, wo_ref[...])
    x2_ref[...] = x2
    pq_ref[...] = _dot(_rms(x2, g_ref[...]).astype(BF16), wpq_ref[...])


def _xattn(qx, kv_mem, x1, w_xo_bf16, gain, w_pq_bf16, batch, seq, mem_len, tm, batch0=0):
    t, d = x1.shape
    npq = w_pq_bf16.shape[1]
    nt = seq // tm
    rows = lambda width: pl.BlockSpec((tm, width), lambda b, i: (b * nt + i, 0))
    full = lambda shape: pl.BlockSpec(shape, lambda b, i: (0, 0))
    return pl.pallas_call(
        _xattn_kernel,
        grid=(batch, nt),
        in_specs=[
            rows(XATT_WIDTH),
            pl.BlockSpec((mem_len, XATT_WIDTH), lambda b, i: (b + batch0, 0)),
            pl.BlockSpec((mem_len, XATT_WIDTH), lambda b, i: (b + batch0, 1)),
            rows(d),
            full((XATT_WIDTH, d)),
            full((1, d)),
            full((d, npq)),
        ],
        out_specs=[rows(d), rows(npq)],
        out_shape=[jax.ShapeDtypeStruct((t, d), F32), jax.ShapeDtypeStruct((t, npq), F32)],
        compiler_params=_params("parallel", "parallel"),
        name="cross_attention",
    )(qx, kv_mem, kv_mem, x1, w_xo_bf16, gain.reshape(1, d), w_pq_bf16)


def _topk_rows(s, k, payload=None):
    nrows = s.shape[0]
    rows = lax.broadcasted_iota(jnp.int32, s.shape, 0).astype(F32)
    vals, sel = [], []
    for _ in range(k):
        m = jnp.max(s, axis=0, keepdims=True)
        am = jnp.min(jnp.where(s == m, rows, float(nrows)), axis=0, keepdims=True)
        hit = rows == am
        vals.append(m)
        sel.append(am if payload is None else jnp.max(jnp.where(hit, payload, -1.0), axis=0, keepdims=True))
        s = jnp.where(hit, -jnp.inf, s)
    return jnp.concatenate(vals, axis=0), jnp.concatenate(sel, axis=0)


def _staircase(t1, t2):
    kk, half = PEER_TOPK, SUBLANES
    blocks = [(t1[0:1, :], t2, kk), (t1[1:2, :], t2[0:half, :], half)]
    blocks += [(t1[a:a + 1, :], t2[0:half, :], kk // (a + 1)) for a in range(2, half)]
    blocks.append((t1[half:kk, :], t2[0:1, :], half))
    return blocks


def _peer_topk_kernel(q_ref, keys_ref, ids_ref, gates_ref, *, heads):
    width = 2 * PEER_HALF_DIM
    row8 = lax.broadcasted_iota(jnp.int32, (SUBLANES, q_ref.shape[0]), 0)
    for hh in range(heads):
        tops = []
        for p in range(2):
            c0 = hh * width + p * PEER_HALF_DIM
            qp = q_ref[:, c0:c0 + PEER_HALF_DIM].astype(BF16)
            tops.append(_topk_rows(_dot_nt(keys_ref[hh, p], qp), PEER_TOPK))
        (s1, i1), (s2, i2) = tops
        cand_s, cand_i = [], []
        for (a_s, b_s, live), (a_i, b_i, _) in zip(_staircase(s1, s2), _staircase(i1 * float(PEER_N_KEYS), i2)):
            blk = a_s + b_s
            if live < blk.shape[0]:
                blk = jnp.where(row8 < live, blk, -jnp.inf)
            cand_s.append(blk)
            cand_i.append(a_i + b_i)
        top_s, top_e = _topk_rows(jnp.concatenate(cand_s, axis=0), PEER_TOPK,
                                  payload=jnp.concatenate(cand_i, axis=0))
        e = jnp.exp(top_s - top_s[0:1, :])
        ids_ref[hh] = top_e.astype(jnp.int32)
        gates_ref[hh] = e / jnp.sum(e, axis=0, keepdims=True)


def _peer_topk(pq, keys_bf16, tt, heads):
    t = pq.shape[0]
    width = 2 * PEER_HALF_DIM
    out = pl.BlockSpec((heads, PEER_TOPK, tt), lambda i, hh: (hh, 0, i))
    return pl.pallas_call(
        functools.partial(_peer_topk_kernel, heads=heads),
        grid=(t // tt, PEER_HEADS // heads),
        in_specs=[
            pl.BlockSpec((tt, heads * width), lambda i, hh: (i, hh)),
            pl.BlockSpec((heads, 2, PEER_N_KEYS, PEER_HALF_DIM), lambda i, hh: (hh, 0, 0, 0)),
        ],
        out_specs=[out, out],
        out_shape=[jax.ShapeDtypeStruct((PEER_HEADS, PEER_TOPK, t), jnp.int32),
                   jax.ShapeDtypeStruct((PEER_HEADS, PEER_TOPK, t), F32)],
        compiler_params=_params("parallel", "parallel"),
        name="peer_topk",
    )(pq, keys_bf16)


GATHER_TOKENS = 128
GATHER_BUFS = 8
GATHER_GROUP = 8
assert GATHER_GROUP % GATHER_BUFS == 0 and GATHER_GROUP % SUBLANES == 0 and GATHER_TOKENS % GATHER_GROUP == 0


def _peer_gather_kernel(ids_hbm, tbl_hbm, x2_ref, gates_ref, fg_ref, og_ref, *rest, final_norm, first_block,
                        has_prev):
    out_ref, ids_smem, ids_sem, *scratch = rest[1:] if has_prev else rest
    gbufs = scratch[:GATHER_BUFS]
    gsem, xn_ref, y_ref, yacc_ref = scratch[GATHER_BUFS:]
    i = pl.program_id(0)
    has_next = i + 1 < pl.num_programs(0)
    slot = i % 2
    other = 1 - slot
    tb = GATHER_TOKENS
    nchunk = D_MODEL // LANES
    nids = tb * PEER_SLOTS
    ahead = GATHER_BUFS - 1
    group = GATHER_GROUP

    def ids_copy(step, sl):
        return pltpu.make_async_copy(ids_hbm.at[pl.ds(pl.multiple_of((first_block + step) * nids, nids), nids)],
                                     ids_smem.at[pl.ds(pl.multiple_of(sl * nids, nids), nids)], ids_sem.at[sl])

    def row_copy(e, b, k):
        return pltpu.make_async_copy(tbl_hbm.at[e], gbufs[b].at[pl.ds(k, 1)], gsem.at[b])

    def issue(sl, tok, b):
        base = sl * nids + tok * PEER_SLOTS
        for k in range(PEER_SLOTS):
            row_copy(ids_smem[base + k], b, k).start(priority=k % 2)

    def wait_rows(b):
        for k in range(PEER_SLOTS):
            row_copy(0, b, k).wait()

    @pl.when(i == 0)
    def _():
        ids_copy(0, 0).start()
        ids_copy(0, 0).wait()
        for tok in range(ahead):
            issue(0, tok, tok)

    @pl.when(has_next)
    def _():
        ids_copy(i + 1, other).start()

    xn_ref[...] = _rms(x2_ref[...], fg_ref[...])

    lane = lax.broadcasted_iota(jnp.int32, (PEER_SLOTS, LANES), 1)
    hi_mask = jnp.uint32(0xFFFF0000)

    def compute(tok, r, b):
        gbuf = gbufs[b]
        xrow = xn_ref[pl.ds(tok, 1), :]
        acc = jnp.zeros((PEER_SLOTS, LANES), F32)
        for c in range(nchunk):
            w = gbuf[:, c * LANES:(c + 1) * LANES]
            u = lax.bitcast_convert_type(w << 16, F32)
            acc = acc + u * xrow[:, c * LANES:(c + 1) * LANES]
        a = jnp.sum(acc, axis=-1, keepdims=True)
        gate = jnp.sum(jnp.where(lane == tok, gates_ref[...], 0.0), axis=-1, keepdims=True)
        coeff = 0.5 * a * (1.0 + lax.erf(a * (2.0 ** -0.5))) * gate
        for c in range(nchunk):
            w = gbuf[:, c * LANES:(c + 1) * LANES]
            vv = lax.bitcast_convert_type(w & hi_mask, F32)
            yacc_ref[r:r + 1, c * LANES:(c + 1) * LANES] = jnp.sum(vv * coeff, axis=0, keepdims=True)

    def run_group(g, last):
        for r in range(group):
            b = r % GATHER_BUFS
            nb = (r + ahead) % GATHER_BUFS
            tok = g * group + r
            wait_rows(b)
            if not last or r + ahead < group:
                issue(slot, tok + ahead, nb)
            else:
                if r + ahead == group:
                    @pl.when(has_next)
                    def _():
                        ids_copy(i + 1, other).wait()

                @pl.when(has_next)
                def _():
                    issue(other, r + ahead - group, nb)
            compute(tok, r, b)
        y_ref[pl.ds(pl.multiple_of(g * group, group), group), :] = yacc_ref[...]

    def body(g, carry):
        run_group(g, last=False)
        return carry

    ngroups = tb // group
    lax.fori_loop(0, ngroups - 1, body, 0)
    run_group(ngroups - 1, last=True)
    x3 = x2_ref[...] + y_ref[...]
    out_ref[...] = _rms(x3, og_ref[...]) if final_norm else x3


def _peer_gather(ids_flat, table, x2, gates_t, ffn_gain, final_gain, final_norm, first_block, out_rows,
                 out_block0, prev):
    t, d = x2.shape
    tb = GATHER_TOKENS
    nsteps = t // tb - first_block
    full = lambda shape: pl.BlockSpec(shape, lambda i: (0, 0))
    has_prev = prev is not None
    return pl.pallas_call(
        functools.partial(_peer_gather_kernel, final_norm=final_norm, first_block=first_block, has_prev=has_prev),
        grid=(nsteps,),
        in_specs=[
            pl.BlockSpec(memory_space=pl.ANY),
            pl.BlockSpec(memory_space=pl.ANY),
            pl.BlockSpec((tb, d), lambda i: (i + first_block, 0)),
            pl.BlockSpec((PEER_SLOTS, tb), lambda i: (0, i + first_block)),
            full((1, d)),
            full((1, d)),
        ] + ([pl.BlockSpec(memory_space=pl.ANY)] if has_prev else []),
        out_specs=pl.BlockSpec((tb, d), lambda i: (i + first_block + out_block0, 0)),
        out_shape=jax.ShapeDtypeStruct((out_rows, d), F32),
        input_output_aliases={6: 0} if has_prev else {},
        scratch_shapes=[
            pltpu.SMEM((2 * tb * PEER_SLOTS,), jnp.int32),
            pltpu.SemaphoreType.DMA((2,)),
            *[pltpu.VMEM((PEER_SLOTS, d), jnp.uint32) for _ in range(GATHER_BUFS)],
            pltpu.SemaphoreType.DMA((GATHER_BUFS,)),
            pltpu.VMEM((tb, d), F32),
            pltpu.VMEM((tb, d), F32),
            pltpu.VMEM((GATHER_GROUP, d), F32),
        ],
        compiler_params=_params("arbitrary"),
        name="peer_gather",
    )(ids_flat, table, x2, gates_t, ffn_gain.reshape(1, d), final_gain.reshape(1, d), *([prev] if has_prev else []))


SC_CORES, SC_SUBCORES, SC_LANES = 2, 16, 16
SC_WORKERS = SC_CORES * SC_SUBCORES
SC_ROWS = SC_LANES
SC_COLS = 8
SC_DOT_ROWS = 8
SC_PARTS = 4
SC_SHARE = (55, 128)


def _sc_erf(x):
    ax = jnp.abs(x)
    t = 1.0 / (1.0 + 0.3275911 * ax)
    poly = t * (0.254829592 + t * (-0.284496736 + t * (1.421413741 + t * (-1.453152027 + t * 1.061405429))))
    y = 1.0 - poly * jnp.exp(-ax * ax)
    return jnp.where(x < 0, -y, y)


def _peer_sc(ids_flat, gates_tok, xn, table, ts):
    d = D_MODEL
    lanes, rows = SC_LANES, SC_ROWS
    nchunk = PEER_SLOTS // rows
    block = lanes * SC_COLS
    per_worker = ts // SC_WORKERS
    assert ts % SC_WORKERS == 0 and d % block == 0
    mesh = plsc.VectorSubcoreMesh(core_axis_name="c", subcore_axis_name="s")

    @functools.partial(
        pl.kernel, mesh=mesh,
        out_type=jax.ShapeDtypeStruct((ts, d), F32),
        scratch_types=[
            pltpu.VMEM((PEER_SLOTS,), jnp.int32),
            pltpu.VMEM((PEER_SLOTS,), F32),
            pltpu.VMEM((d,), F32),
            pltpu.VMEM((d,), F32),
            pltpu.VMEM((2, rows, 1, d), jnp.uint32),
            pltpu.VMEM((rows, lanes), F32),
            pltpu.VMEM((lanes,), F32),
            pltpu.VMEM((rows, lanes), jnp.int32),
            pltpu.SemaphoreType.DMA((2,)),
        ],
        compiler_params=pltpu.CompilerParams(needs_layout_passes=False),
        name="peer_sparsecore",
    )
    def run(ids_hbm, gates_hbm, xn_hbm, tbl_hbm, y_hbm, ids_v, gates_v, x_v, y_v, rows_v, acc_v, coeff_v, ridx_v, sem):
        worker = lax.axis_index("s") * SC_CORES + lax.axis_index("c")
        lane = lax.iota(jnp.int32, lanes)
        zero = jnp.zeros((lanes,), F32)
        hi_mask = jnp.full((lanes,), 0xFFFF0000, jnp.uint32)
        for r in range(rows):
            ridx_v[r, :] = jnp.full((lanes,), r, jnp.int32)

        def gather(c, slot):
            return pltpu.make_async_copy(tbl_hbm.at[ids_v[pl.ds(c * rows, rows)]], rows_v.at[slot], sem.at[slot])

        @pl.loop(0, per_worker)
        def _(i):
            tok = worker * per_worker + i
            pltpu.sync_copy(ids_hbm.at[pl.ds(tok * PEER_SLOTS, PEER_SLOTS)], ids_v)
            pltpu.sync_copy(gates_hbm.at[pl.ds(tok * PEER_SLOTS, PEER_SLOTS)], gates_v)
            pltpu.sync_copy(xn_hbm.at[tok], x_v)

            @pl.loop(0, d // lanes)
            def _(j):
                y_v[pl.ds(j * lanes, lanes)] = zero

            gather(0, 0).start()
            for c in range(nchunk):
                slot = c % 2
                if c + 1 < nchunk:
                    gather(c + 1, 1 - slot).start()
                gather(c, slot).wait()
                for r0 in range(0, rows, SC_DOT_ROWS):
                    def dot_body(j, accs, r0=r0, slot=slot):
                        xj = x_v[pl.ds(j * lanes, lanes)]
                        return tuple(
                            accs[q] + plsc.bitcast(rows_v[slot, r0 + q, 0, pl.ds(j * lanes, lanes)] << 16, F32) * xj
                            for q in range(SC_DOT_ROWS))
                    accs = lax.fori_loop(0, d // lanes, dot_body, (zero,) * SC_DOT_ROWS, unroll=4)
                    for q in range(SC_DOT_ROWS):
                        acc_v[r0 + q, :] = accs[q]
                a = zero
                for col in range(lanes):
                    a = a + plsc.load_gather(acc_v, [lane, ridx_v[col, :]])
                gate = gates_v[pl.ds(c * rows, rows)]
                coeff_v[...] = 0.5 * a * (1.0 + _sc_erf(a * (2.0 ** -0.5))) * gate
                coeffs = [plsc.load_gather(coeff_v, [ridx_v[r, :]]) for r in range(rows)]

                @pl.loop(0, d // block)
                def _(jb, slot=slot, coeffs=coeffs):
                    base = jb * block
                    accs = [zero] * SC_COLS
                    for r in range(rows):
                        for q in range(SC_COLS):
                            w = rows_v[slot, r, 0, pl.ds(base + q * lanes, lanes)]
                            accs[q] = accs[q] + plsc.bitcast(w & hi_mask, F32) * coeffs[r]
                    for q in range(SC_COLS):
                        plsc.addupdate(y_v.at[pl.ds(base + q * lanes, lanes)], accs[q])

            pltpu.sync_copy(y_v, y_hbm.at[tok])

    return run(ids_flat, gates_tok, xn, table)


def _rms_rows_kernel(x_ref, g_ref, o_ref):
    o_ref[...] = _rms(x_ref[...], g_ref[...])


def _rms_rows(x, gain, rows, tm):
    d = x.shape[1]
    return pl.pallas_call(
        _rms_rows_kernel,
        grid=(rows // tm,),
        in_specs=[pl.BlockSpec((tm, d), lambda i: (i, 0)), pl.BlockSpec((1, d), lambda i: (0, 0))],
        out_specs=pl.BlockSpec((tm, d), lambda i: (i, 0)),
        out_shape=jax.ShapeDtypeStruct((rows, d), F32),
        compiler_params=_params("parallel"),
        name="rms_rows",
    )(x, gain.reshape(1, d))


def _peer_finish_kernel(x2_ref, y_ref, og_ref, prev_ref, out_ref, *, final_norm):
    del prev_ref
    x3 = x2_ref[...] + y_ref[...]
    out_ref[...] = _rms(x3, og_ref[...]) if final_norm else x3


def _peer_finish(x2, y_head, out_rest, final_gain, final_norm, out_block0):
    rows, d = y_head.shape
    tm = GATHER_TOKENS
    blk = pl.BlockSpec((tm, d), lambda i: (i, 0))
    return pl.pallas_call(
        functools.partial(_peer_finish_kernel, final_norm=final_norm),
        grid=(rows // tm,),
        in_specs=[blk, blk, pl.BlockSpec((1, d), lambda i: (0, 0)), pl.BlockSpec(memory_space=pl.ANY)],
        out_specs=pl.BlockSpec((tm, d), lambda i: (i + out_block0, 0)),
        out_shape=jax.ShapeDtypeStruct(out_rest.shape, F32),
        input_output_aliases={3: 0},
        compiler_params=_params("parallel"),
        name="peer_finish",
    )(x2, y_head, final_gain.reshape(1, d), out_rest)


def _rope_angles(positions, dh):
    inv_freq = ROPE_THETA ** (-jnp.arange(0, dh, 2, dtype=F32) / dh)
    ang = positions.astype(F32).reshape(-1, 1) * inv_freq
    return jnp.cos(ang), jnp.sin(ang)


def _rope_tables_att(positions):
    cos, sin = _rope_angles(positions, HEAD_DIM)
    copies = LANES // HEAD_DIM
    return (jnp.tile(jnp.concatenate([cos, cos], axis=1), (1, copies)),
            jnp.tile(jnp.concatenate([-sin, sin], axis=1), (1, copies)))


def _pack_table_kernel(u_ref, v_ref, o_ref):
    ub = lax.bitcast_convert_type(u_ref[...].astype(BF16).astype(F32), jnp.uint32)
    vb = lax.bitcast_convert_type(v_ref[...].astype(BF16).astype(F32), jnp.uint32)
    o_ref[:, 0, :] = (ub >> 16) | vb


def _pack_expert_table(u, v, rows=256):
    n, d = u.shape
    blk = pl.BlockSpec((rows, d), lambda i: (i, 0))
    return pl.pallas_call(
        _pack_table_kernel,
        grid=(n // rows,),
        in_specs=[blk, blk],
        out_specs=pl.BlockSpec((rows, 1, d), lambda i: (i, 0, 0)),
        out_shape=jax.ShapeDtypeStruct((n, 1, d), jnp.uint32),
        compiler_params=_params("parallel"),
        name="pack_table",
    )(u, v)


def kernel(x, mem, positions, mix_norm_gain, w_in, att_sinks, att_out_gain, ret_out_gain, w_out,
           cross_norm_gain, mem_norm_gain, w_xq, w_xk, w_xv, w_xo, ffn_norm_gain,
           w_peer_q, peer_sub_keys, peer_u, peer_v, final_norm_gain):
    batch, seq, d = x.shape
    mem_len = mem.shape[1]
    t = batch * seq
    depth = w_in.shape[0]
    assert d == D_MODEL and seq % BLOCK == 0 and t % GATHER_TOKENS == 0

    cos_a, sin_a = _rope_tables_att(positions)
    cos_r, sin_r = _rope_angles(positions, RET_HEAD_DIM)
    xf = x.reshape(t, d)
    memf = mem.reshape(batch * mem_len, d)
    nblocks = t // GATHER_TOKENS
    parts = SC_PARTS if (batch % SC_PARTS == 0 and nblocks % SC_SHARE[1] == 0) else 1
    bp, tp = batch // parts, t // parts
    blocks_p = tp // GATHER_TOKENS
    sc_total = nblocks * SC_SHARE[0] // SC_SHARE[1] if parts > 1 else 0
    tm_in = min(1024, tp)
    tm_mid = min(256, seq)
    for l in range(depth):
        sc_left = sc_total
        w_in_b = _reorder_w_in(w_in[l]).astype(BF16)
        w_out_b, w_xq_b, w_xo_b, w_pq_b = (w.astype(BF16) for w in (w_out[l], w_xq[l], w_xo[l], w_peer_q[l]))
        keys_b = peer_sub_keys[l].astype(BF16)
        w_kv = jnp.concatenate([w_xk[l], w_xv[l]], axis=1).astype(BF16)
        kv_mem = _norm_matmul(memf, mem_norm_gain[l], w_kv, mem_len, 2 * XATT_WIDTH, BF16)
        table = _pack_expert_table(peer_u[l], peer_v[l])
        last = l == depth - 1
        out = None
        pending = []
        for p in range(parts):
            r0 = p * tp
            rows = slice(r0, r0 + tp)
            h = _norm_matmul(xf, mix_norm_gain[l], w_in_b, tm_in, 1792, BF16, row0=r0, rows=tp)
            oa = _swa(h, cos_a[rows], sin_a[rows], att_sinks[l], att_out_gain[l], bp, seq)
            orr = _retention(h, cos_r[rows], sin_r[rows], ret_out_gain[l], bp, seq)
            x1, qx = _outproj(xf, oa, orr, w_out_b, cross_norm_gain[l], w_xq_b, tm_mid, row0=r0)
            x2, pq = _xattn(qx, kv_mem, x1, w_xo_b, ffn_norm_gain[l], w_pq_b, bp, seq, mem_len, tm_mid,
                            batch0=p * bp)
            ids_t, gates_t = _peer_topk(pq, keys_b, LANES, PEER_HEADS)
            ids = ids_t.reshape(PEER_SLOTS, tp).T.reshape(tp * PEER_SLOTS)
            gates_t = gates_t.reshape(PEER_SLOTS, tp)
            sc_blocks = min(sc_left, blocks_p)
            sc_left -= sc_blocks
            if sc_blocks:
                ts = sc_blocks * GATHER_TOKENS
                xn_head = _rms_rows(x2, ffn_norm_gain[l], ts, GATHER_TOKENS)
                y_head = _peer_sc(ids, gates_t[:, :ts].T.reshape(ts * PEER_SLOTS), xn_head, table, ts)
                pending.append((x2, y_head, p * blocks_p))
            if sc_blocks < blocks_p:
                out = _peer_gather(ids, table, x2, gates_t, ffn_norm_gain[l], final_norm_gain, last, sc_blocks,
                                   t, p * blocks_p, out)
        for x2, y_head, block0 in pending:
            out = _peer_finish(x2, y_head, out, final_norm_gain, last, block0)
        xf = out
    return xf.reshape(batch, seq, d)
```

```python
import functools
import math

import jax
import jax.numpy as jnp
from jax import lax
from jax.experimental import pallas as pl
from jax.experimental.pallas import tpu as pltpu
from jax.experimental.pallas import tpu_sc as plsc

F32 = jnp.float32
BF16 = jnp.bfloat16

EPS = 1e-6
GN_EPS = 1e-5
ROPE_THETA = 10000.0
D_MODEL = 2048
HEAD_DIM = 64
ATT_Q_HEADS = 16
ATT_KV_HEADS = 2
ATT_WIDTH = ATT_Q_HEADS * HEAD_DIM
ATT_KV_WIDTH = ATT_KV_HEADS * HEAD_DIM
BLOCK = 128
RET_HEADS = 4
RET_HEAD_DIM = 256
RET_WIDTH = RET_HEADS * RET_HEAD_DIM
RET_CHUNK = 128
IN_COLS = ATT_WIDTH + 2 * ATT_KV_WIDTH + 4 * RET_WIDTH
XATT_HEADS = 4
XATT_HEAD_DIM = 128
XATT_WIDTH = XATT_HEADS * XATT_HEAD_DIM
PEER_HEADS = 8
PEER_N_KEYS = 128
PEER_HALF_DIM = 128
PEER_TOPK = 16
PEER_SLOTS = PEER_HEADS * PEER_TOPK

LANES = 128
SUBLANES = 8
VMEM_LIMIT = 56 * 1024 * 1024

_REF_QR0 = ATT_WIDTH + 2 * ATT_KV_WIDTH
_QR0 = 0
_KR0 = _QR0 + RET_WIDTH
_VR0 = _KR0 + RET_WIDTH
_GR0 = _VR0 + RET_WIDTH
_QA0 = _GR0 + RET_WIDTH
_KA0 = _QA0 + ATT_WIDTH
_VA0 = _KA0 + ATT_KV_WIDTH


def _reorder_w_in(w):
    return jnp.concatenate([w[:, _REF_QR0:], w[:, :_REF_QR0]], axis=1)


def _params(*sem):
    return pltpu.CompilerParams(dimension_semantics=sem, vmem_limit_bytes=VMEM_LIMIT)


def _rms(xf, gain):
    ms = jnp.mean(xf * xf, axis=-1, keepdims=True)
    return xf * lax.rsqrt(ms + EPS) * gain


def _dot(a, b):
    return jnp.dot(a, b, preferred_element_type=F32)


def _dot_nt(a, b):
    return lax.dot_general(a, b, (((1,), (1,)), ((), ())), preferred_element_type=F32)


def _norm_matmul_kernel(x_ref, g_ref, w_ref, o_ref, xn_ref):
    @pl.when(pl.program_id(1) == 0)
    def _():
        xn_ref[...] = _rms(x_ref[...], g_ref[...]).astype(BF16)

    o_ref[...] = _dot(xn_ref[...], w_ref[...]).astype(o_ref.dtype)


def _norm_matmul(x, gain, w_bf16, tm, tn, out_dtype, row0=0, rows=None):
    d = x.shape[1]
    t = x.shape[0] if rows is None else rows
    n = w_bf16.shape[1]
    blk0 = row0 // tm
    return pl.pallas_call(
        _norm_matmul_kernel,
        grid=(t // tm, n // tn),
        in_specs=[
            pl.BlockSpec((tm, d), lambda i, j: (i + blk0, 0)),
            pl.BlockSpec((1, d), lambda i, j: (0, 0)),
            pl.BlockSpec((d, tn), lambda i, j: (0, j)),
        ],
        out_specs=pl.BlockSpec((tm, tn), lambda i, j: (i, j)),
        out_shape=jax.ShapeDtypeStruct((t, n), out_dtype),
        scratch_shapes=[pltpu.VMEM((tm, d), BF16)],
        compiler_params=_params("parallel", "arbitrary"),
        name="norm_matmul",
    )(x, gain.reshape(1, d), w_bf16)


def _swa_kernel(sink_ref, q_ref, kc_ref, kp_ref, vc_ref, vp_ref, cc_ref, sc_ref, cp_ref, sp_ref,
                gain_ref, o_ref):
    n = pl.program_id(1)
    lane = lax.broadcasted_iota(jnp.int32, (1, LANES), 1)
    first_half = (lane % HEAD_DIM) < (HEAD_DIM // 2)
    lo = lane < HEAD_DIM

    def rope(x, c, s):
        partner = jnp.where(first_half, pltpu.roll(x, LANES - HEAD_DIM // 2, 1),
                            pltpu.roll(x, HEAD_DIM // 2, 1))
        return x * c + partner * s

    cc, sc = cc_ref[...], sc_ref[...]
    k = jnp.concatenate([rope(kp_ref[...].astype(F32), cp_ref[...], sp_ref[...]),
                         rope(kc_ref[...].astype(F32), cc, sc)], axis=0)
    v = jnp.concatenate([vp_ref[...].astype(F32), vc_ref[...].astype(F32)], axis=0)
    k_sw = pltpu.roll(k, HEAD_DIM, 1)
    v_sw = pltpu.roll(v, HEAD_DIM, 1)

    def place(a, a_sw, c, half):
        src = a if c == half else a_sw
        keep = lo if half == 0 else jnp.logical_not(lo)
        return jnp.where(keep, src, 0.0).astype(BF16)

    kvar = {(c, h): place(k, k_sw, c, h) for c in range(ATT_KV_HEADS) for h in range(2)}
    vvar = {(c, h): place(v, v_sw, c, h) for c in range(ATT_KV_HEADS) for h in range(2)}

    qi = lax.broadcasted_iota(jnp.int32, (BLOCK, 2 * BLOCK), 0) + BLOCK
    ki = lax.broadcasted_iota(jnp.int32, (BLOCK, 2 * BLOCK), 1)
    dist = qi - ki
    kmin = jnp.where(n > 0, 0, BLOCK)
    valid = (dist >= 0) & (dist < BLOCK) & (ki >= kmin)

    group = ATT_Q_HEADS // ATT_KV_HEADS
    for j in range(ATT_WIDTH // LANES):
        c = (2 * j) // group
        cols = slice(j * LANES, (j + 1) * LANES)
        qg = (rope(q_ref[:, cols].astype(F32), cc, sc) * (HEAD_DIM ** -0.5)).astype(BF16)
        out = jnp.zeros((BLOCK, LANES), F32)
        for half in range(2):
            s = jnp.where(valid, _dot_nt(qg, kvar[(c, half)]), -jnp.inf)
            sink = sink_ref[2 * j + half]
            m = jnp.maximum(jnp.max(s, axis=-1, keepdims=True), sink)
            p = jnp.exp(s - m)
            denom = jnp.sum(p, axis=-1, keepdims=True) + jnp.exp(sink - m)
            out = out + _dot((p / denom).astype(BF16), vvar[(c, half)])
        sq = out * out
        ss_lo = jnp.sum(jnp.where(lo, sq, 0.0), axis=-1, keepdims=True)
        ss_hi = jnp.sum(jnp.where(lo, 0.0, sq), axis=-1, keepdims=True)
        ms = jnp.where(lo, ss_lo, ss_hi) * (1.0 / HEAD_DIM)
        o_ref[:, cols] = (out * lax.rsqrt(ms + EPS) * gain_ref[:, cols]).astype(o_ref.dtype)


def _swa(h, cos_a, sin_a, sinks, gain, batch, seq):
    nb = seq // BLOCK
    t = batch * seq
    kcol, vcol = _KA0 // LANES, _VA0 // LANES
    cur = lambda b, n: b * nb + n
    prev = lambda b, n: b * nb + jnp.maximum(n - 1, 0)
    row = lambda col, f: pl.BlockSpec((BLOCK, LANES), lambda b, n: (f(b, n), col))
    return pl.pallas_call(
        _swa_kernel,
        grid=(batch, nb),
        in_specs=[
            pl.BlockSpec(memory_space=pltpu.SMEM),
            pl.BlockSpec((BLOCK, ATT_WIDTH), lambda b, n: (cur(b, n), _QA0 // ATT_WIDTH)),
            row(kcol, cur), row(kcol, prev), row(vcol, cur), row(vcol, prev),
            row(0, cur), row(0, cur), row(0, prev), row(0, prev),
            pl.BlockSpec((1, ATT_WIDTH), lambda b, n: (0, 0)),
        ],
        out_specs=pl.BlockSpec((BLOCK, ATT_WIDTH), lambda b, n: (cur(b, n), 0)),
        out_shape=jax.ShapeDtypeStruct((t, ATT_WIDTH), BF16),
        compiler_params=_params("parallel", "parallel"),
        name="swa_attention",
    )(sinks, h, h, h, h, h, cos_a, sin_a, cos_a, sin_a, gain.reshape(1, ATT_WIDTH))


def _ret_kernel(lg_ref, cd_ref, q_ref, k_ref, v_ref, g_ref, c_ref, s_ref, gain_ref, o_ref, state_ref):
    n = pl.program_id(1)

    @pl.when(n == 0)
    def _():
        state_ref[...] = jnp.zeros_like(state_ref)

    c, s = c_ref[...], s_ref[...]
    hd = RET_HEAD_DIM
    half = hd // 2

    def rope(x):
        x1, x2 = x[:, :half], x[:, half:]
        return jnp.concatenate([x1 * c - x2 * s, x2 * c + x1 * s], axis=1)

    ri = lax.broadcasted_iota(jnp.int32, (RET_CHUNK, RET_CHUNK), 0).astype(F32)
    ci = lax.broadcasted_iota(jnp.int32, (RET_CHUNK, RET_CHUNK), 1).astype(F32)
    diff = ri - ci
    for hh in range(RET_HEADS):
        cols = slice(hh * hd, (hh + 1) * hd)
        lg = lg_ref[hh]
        q = rope(q_ref[:, cols].astype(F32))
        k = rope(k_ref[:, cols].astype(F32)) * (hd ** -0.5)
        v = v_ref[:, cols].astype(BF16)

        decay = jnp.where(diff >= 0, jnp.exp(jnp.maximum(diff, 0.0) * lg), 0.0)
        zeta = jnp.exp((RET_CHUNK - 1 - ri) * lg)
        xi = jnp.exp((ri + 1.0) * lg)
        zeta2 = jnp.concatenate([zeta, zeta], axis=1)
        xi2 = jnp.concatenate([xi, xi], axis=1)

        inner = _dot_nt(q.astype(BF16), k.astype(BF16)) * decay
        state = state_ref[hh]
        out = _dot(inner.astype(BF16), v) + _dot((q * xi2).astype(BF16), state.astype(BF16))
        kv = lax.dot_general((k * zeta2).astype(BF16), v, (((0,), (0,)), ((), ())), preferred_element_type=F32)
        state_ref[hh] = state * cd_ref[hh] + kv

        mu = jnp.mean(out, axis=-1, keepdims=True)
        cen = out - mu
        var = jnp.mean(cen * cen, axis=-1, keepdims=True)
        g = g_ref[:, cols].astype(F32)
        o = cen * lax.rsqrt(var + GN_EPS) * gain_ref[:, cols] * (g * jax.nn.sigmoid(g))
        o_ref[:, cols] = o.astype(o_ref.dtype)


def _retention(h, cos_r, sin_r, gain, batch, seq):
    nc = seq // RET_CHUNK
    t = batch * seq
    hd = RET_HEAD_DIM
    log_gamma = jnp.log1p(-jnp.exp2(-5.0 - jnp.arange(RET_HEADS, dtype=F32)))
    chunk_decay = jnp.exp(RET_CHUNK * log_gamma)
    col = lambda c0: pl.BlockSpec((RET_CHUNK, RET_WIDTH), lambda b, n: (b * nc + n, c0 // RET_WIDTH))
    tab = pl.BlockSpec((RET_CHUNK, hd // 2), lambda b, n: (b * nc + n, 0))
    return pl.pallas_call(
        _ret_kernel,
        grid=(batch, nc),
        in_specs=[
            pl.BlockSpec(memory_space=pltpu.SMEM),
            pl.BlockSpec(memory_space=pltpu.SMEM),
            col(_QR0), col(_KR0), col(_VR0), col(_GR0), tab, tab,
            pl.BlockSpec((1, RET_WIDTH), lambda b, n: (0, 0)),
        ],
        out_specs=pl.BlockSpec((RET_CHUNK, RET_WIDTH), lambda b, n: (b * nc + n, 0)),
        out_shape=jax.ShapeDtypeStruct((t, RET_WIDTH), BF16),
        scratch_shapes=[pltpu.VMEM((RET_HEADS, hd, hd), F32)],
        compiler_params=_params("parallel", "arbitrary"),
        name="retention",
    )(log_gamma, chunk_decay, h, h, h, h, cos_r, sin_r, gain.reshape(1, RET_WIDTH))


def _outproj_kernel(x_ref, oa_ref, or_ref, wa_ref, wr_ref, g_ref, wq_ref, x1_ref, qx_ref):
    x1 = x_ref[...] + _dot(oa_ref[...], wa_ref[...]) + _dot(or_ref[...], wr_ref[...])
    x1_ref[...] = x1
    qx_ref[...] = _dot(_rms(x1, g_ref[...]).astype(BF16), wq_ref[...]).astype(qx_ref.dtype)


def _outproj(x, oa, orr, w_out_bf16, gain, w_xq_bf16, tm, row0=0):
    t, d = oa.shape[0], x.shape[1]
    blk0 = row0 // tm
    full = lambda shape: pl.BlockSpec(shape, lambda i: (0, 0))
    return pl.pallas_call(
        _outproj_kernel,
        grid=(t // tm,),
        in_specs=[
            pl.BlockSpec((tm, d), lambda i: (i + blk0, 0)),
            pl.BlockSpec((tm, ATT_WIDTH), lambda i: (i, 0)),
            pl.BlockSpec((tm, RET_WIDTH), lambda i: (i, 0)),
            pl.BlockSpec((ATT_WIDTH, d), lambda i: (0, 0)),
            pl.BlockSpec((RET_WIDTH, d), lambda i: (1, 0)),
            full((1, d)),
            full((d, XATT_WIDTH)),
        ],
        out_specs=[pl.BlockSpec((tm, d), lambda i: (i, 0)), pl.BlockSpec((tm, XATT_WIDTH), lambda i: (i, 0))],
        out_shape=[jax.ShapeDtypeStruct((t, d), F32), jax.ShapeDtypeStruct((t, XATT_WIDTH), BF16)],
        compiler_params=_params("parallel"),
        name="out_proj",
    )(x, oa, orr, w_out_bf16, w_out_bf16, gain.reshape(1, d), w_xq_bf16)


def _xattn_kernel(qx_ref, k_ref, v_ref, x1_ref, wo_ref, g_ref, wpq_ref, x2_ref, pq_ref):
    heads = []
    for hh in range(XATT_HEADS):
        cols = slice(hh * XATT_HEAD_DIM, (hh + 1) * XATT_HEAD_DIM)
        s = _dot_nt(qx_ref[:, cols], k_ref[:, cols]) * (XATT_HEAD_DIM ** -0.5)
        p = jnp.exp(s - jnp.max(s, axis=-1, keepdims=True))
        p = p / jnp.sum(p, axis=-1, keepdims=True)
        heads.append(_dot(p.astype(BF16), v_ref[:, cols]))
    o = jnp.concatenate(heads, axis=1).astype(BF16)
    x2 = x1_ref[...] + _dot(o, wo_ref[...])
    x2_ref[...] = x2
    pq_ref[...] = _dot(_rms(x2, g_ref[...]).astype(BF16), wpq_ref[...])


def _xattn(qx, kv_mem, x1, w_xo_bf16, gain, w_pq_bf16, batch, seq, mem_len, tm, batch0=0):
    t, d = x1.shape
    npq = w_pq_bf16.shape[1]
    nt = seq // tm
    rows = lambda width: pl.BlockSpec((tm, width), lambda b, i: (b * nt + i, 0))
    full = lambda shape: pl.BlockSpec(shape, lambda b, i: (0, 0))
    return pl.pallas_call(
        _xattn_kernel,
        grid=(batch, nt),
        in_specs=[
            rows(XATT_WIDTH),
            pl.BlockSpec((mem_len, XATT_WIDTH), lambda b, i: (b + batch0, 0)),
            pl.BlockSpec((mem_len, XATT_WIDTH), lambda b, i: (b + batch0, 1)),
            rows(d),
            full((XATT_WIDTH, d)),
            full((1, d)),
            full((d, npq)),
        ],
        out_specs=[rows(d), rows(npq)],
        out_shape=[jax.ShapeDtypeStruct((t, d), F32), jax.ShapeDtypeStruct((t, npq), F32)],
        compiler_params=_params("parallel", "parallel"),
        name="cross_attention",
    )(qx, kv_mem, kv_mem, x1, w_xo_bf16, gain.reshape(1, d), w_pq_bf16)


def _topk_rows(s, k, payload=None):
    nrows = s.shape[0]
    rows = lax.broadcasted_iota(jnp.int32, s.shape, 0).astype(F32)
    vals, sel = [], []
    for _ in range(k):
        m = jnp.max(s, axis=0, keepdims=True)
        am = jnp.min(jnp.where(s == m, rows, float(nrows)), axis=0, keepdims=True)
        hit = rows == am
        vals.append(m)
        sel.append(am if payload is None else jnp.max(jnp.where(hit, payload, -1.0), axis=0, keepdims=True))
        s = jnp.where(hit, -jnp.inf, s)
    return jnp.concatenate(vals, axis=0), jnp.concatenate(sel, axis=0)


def _staircase(t1, t2):
    kk, half = PEER_TOPK, SUBLANES
    blocks = [(t1[0:1, :], t2, kk), (t1[1:2, :], t2[0:half, :], half)]
    blocks += [(t1[a:a + 1, :], t2[0:half, :], kk // (a + 1)) for a in range(2, half)]
    blocks.append((t1[half:kk, :], t2[0:1, :], half))
    return blocks


def _peer_topk_kernel(q_ref, keys_ref, ids_ref, gates_ref, *, heads):
    width = 2 * PEER_HALF_DIM
    row8 = lax.broadcasted_iota(jnp.int32, (SUBLANES, q_ref.shape[0]), 0)
    for hh in range(heads):
        tops = []
        for p in range(2):
            c0 = hh * width + p * PEER_HALF_DIM
            qp = q_ref[:, c0:c0 + PEER_HALF_DIM].astype(BF16)
            tops.append(_topk_rows(_dot_nt(keys_ref[hh, p], qp), PEER_TOPK))
        (s1, i1), (s2, i2) = tops
        cand_s, cand_i = [], []
        for (a_s, b_s, live), (a_i, b_i, _) in zip(_staircase(s1, s2), _staircase(i1 * float(PEER_N_KEYS), i2)):
            blk = a_s + b_s
            if live < blk.shape[0]:
                blk = jnp.where(row8 < live, blk, -jnp.inf)
            cand_s.append(blk)
            cand_i.append(a_i + b_i)
        top_s, top_e = _topk_rows(jnp.concatenate(cand_s, axis=0), PEER_TOPK,
                                  payload=jnp.concatenate(cand_i, axis=0))
        e = jnp.exp(top_s - top_s[0:1, :])
        ids_ref[hh] = top_e.astype(jnp.int32)
        gates_ref[hh] = e / jnp.sum(e, axis=0, keepdims=True)


def _peer_topk(pq, keys_bf16, tt, heads):
    t = pq.shape[0]
    width = 2 * PEER_HALF_DIM
    out = pl.BlockSpec((heads, PEER_TOPK, tt), lambda i, hh: (hh, 0, i))
    return pl.pallas_call(
        functools.partial(_peer_topk_kernel, heads=heads),
        grid=(t // tt, PEER_HEADS // heads),
        in_specs=[
            pl.BlockSpec((tt, heads * width), lambda i, hh: (i, hh)),
            pl.BlockSpec((heads, 2, PEER_N_KEYS, PEER_HALF_DIM), lambda i, hh: (hh, 0, 0, 0)),
        ],
        out_specs=[out, out],
        out_shape=[jax.ShapeDtypeStruct((PEER_HEADS, PEER_TOPK, t), jnp.int32),
                   jax.ShapeDtypeStruct((PEER_HEADS, PEER_TOPK, t), F32)],
        compiler_params=_params("parallel", "parallel"),
        name="peer_topk",
    )(pq, keys_bf16)


GATHER_TOKENS = 128
GATHER_BUFS = 8
GATHER_GROUP = 8
assert GATHER_GROUP % GATHER_BUFS == 0 and GATHER_GROUP % SUBLANES == 0 and GATHER_TOKENS % GATHER_GROUP == 0


def _peer_gather_kernel(ids_hbm, tbl_hbm, x2_ref, gates_ref, fg_ref, og_ref, *rest, final_norm, first_block,
                        has_prev):
    out_ref, ids_smem, ids_sem, *scratch = rest[1:] if has_prev else rest
    gbufs = scratch[:GATHER_BUFS]
    gsem, xn_ref, y_ref, yacc_ref = scratch[GATHER_BUFS:]
    i = pl.program_id(0)
    has_next = i + 1 < pl.num_programs(0)
    slot = i % 2
    other = 1 - slot
    tb = GATHER_TOKENS
    nchunk = D_MODEL // LANES
    nids = tb * PEER_SLOTS
    ahead = GATHER_BUFS - 1
    group = GATHER_GROUP

    def ids_copy(step, sl):
        return pltpu.make_async_copy(ids_hbm.at[pl.ds(pl.multiple_of((first_block + step) * nids, nids), nids)],
                                     ids_smem.at[pl.ds(pl.multiple_of(sl * nids, nids), nids)], ids_sem.at[sl])

    def row_copy(e, b, k):
        return pltpu.make_async_copy(tbl_hbm.at[e], gbufs[b].at[pl.ds(k, 1)], gsem.at[b])

    def issue(sl, tok, b):
        base = sl * nids + tok * PEER_SLOTS
        for k in range(PEER_SLOTS):
            row_copy(ids_smem[base + k], b, k).start(priority=k % 2)

    def wait_rows(b):
        for k in range(PEER_SLOTS):
            row_copy(0, b, k).wait()

    @pl.when(i == 0)
    def _():
        ids_copy(0, 0).start()
        ids_copy(0, 0).wait()
        for tok in range(ahead):
            issue(0, tok, tok)

    @pl.when(has_next)
    def _():
        ids_copy(i + 1, other).start()

    xn_ref[...] = _rms(x2_ref[...], fg_ref[...])

    lane = lax.broadcasted_iota(jnp.int32, (PEER_SLOTS, LANES), 1)
    hi_mask = jnp.uint32(0xFFFF0000)

    def compute(tok, r, b):
        gbuf = gbufs[b]
        xrow = xn_ref[pl.ds(tok, 1), :]
        acc = jnp.zeros((PEER_SLOTS, LANES), F32)
        for c in range(nchunk):
            w = gbuf[:, c * LANES:(c + 1) * LANES]
            u = lax.bitcast_convert_type(w << 16, F32)
            acc = acc + u * xrow[:, c * LANES:(c + 1) * LANES]
        a = jnp.sum(acc, axis=-1, keepdims=True)
        gate = jnp.sum(jnp.where(lane == tok, gates_ref[...], 0.0), axis=-1, keepdims=True)
        coeff = 0.5 * a * (1.0 + lax.erf(a * (2.0 ** -0.5))) * gate
        for c in range(nchunk):
            w = gbuf[:, c * LANES:(c + 1) * LANES]
            vv = lax.bitcast_convert_type(w & hi_mask, F32)
            yacc_ref[r:r + 1, c * LANES:(c + 1) * LANES] = jnp.sum(vv * coeff, axis=0, keepdims=True)

    def run_group(g, last):
        for r in range(group):
            b = r % GATHER_BUFS
            nb = (r + ahead) % GATHER_BUFS
            tok = g * group + r
            wait_rows(b)
            if not last or r + ahead < group:
                issue(slot, tok + ahead, nb)
            else:
                if r + ahead == group:
                    @pl.when(has_next)
                    def _():
                        ids_copy(i + 1, other).wait()

                @pl.when(has_next)
                def _():
                    issue(other, r + ahead - group, nb)
            compute(tok, r, b)
        y_ref[pl.ds(pl.multiple_of(g * group, group), group), :] = yacc_ref[...]

    def body(g, carry):
        run_group(g, last=False)
        return carry

    ngroups = tb // group
    lax.fori_loop(0, ngroups - 1, body, 0)
    run_group(ngroups - 1, last=True)
    x3 = x2_ref[...] + y_ref[...]
    out_ref[...] = _rms(x3, og_ref[...]) if final_norm else x3


def _peer_gather(ids_flat, table, x2, gates_t, ffn_gain, final_gain, final_norm, first_block, out_rows,
                 out_block0, prev):
    t, d = x2.shape
    tb = GATHER_TOKENS
    nsteps = t // tb - first_block
    full = lambda shape: pl.BlockSpec(shape, lambda i: (0, 0))
    has_prev = prev is not None
    return pl.pallas_call(
        functools.partial(_peer_gather_kernel, final_norm=final_norm, first_block=first_block, has_prev=has_prev),
        grid=(nsteps,),
        in_specs=[
            pl.BlockSpec(memory_space=pl.ANY),
            pl.BlockSpec(memory_space=pl.ANY),
            pl.BlockSpec((tb, d), lambda i: (i + first_block, 0)),
            pl.BlockSpec((PEER_SLOTS, tb), lambda i: (0, i + first_block)),
            full((1, d)),
            full((1, d)),
        ] + ([pl.BlockSpec(memory_space=pl.ANY)] if has_prev else []),
        out_specs=pl.BlockSpec((tb, d), lambda i: (i + first_block + out_block0, 0)),
        out_shape=jax.ShapeDtypeStruct((out_rows, d), F32),
        input_output_aliases={6: 0} if has_prev else {},
        scratch_shapes=[
            pltpu.SMEM((2 * tb * PEER_SLOTS,), jnp.int32),
            pltpu.SemaphoreType.DMA((2,)),
            *[pltpu.VMEM((PEER_SLOTS, d), jnp.uint32) for _ in range(GATHER_BUFS)],
            pltpu.SemaphoreType.DMA((GATHER_BUFS,)),
            pltpu.VMEM((tb, d), F32),
            pltpu.VMEM((tb, d), F32),
            pltpu.VMEM((GATHER_GROUP, d), F32),
        ],
        compiler_params=_params("arbitrary"),
        name="peer_gather",
    )(ids_flat, table, x2, gates_t, ffn_gain.reshape(1, d), final_gain.reshape(1, d), *([prev] if has_prev else []))


SC_CORES, SC_SUBCORES, SC_LANES = 2, 16, 16
SC_WORKERS = SC_CORES * SC_SUBCORES
SC_ROWS = SC_LANES
SC_COLS = 8
SC_DOT_ROWS = 8
SC_PARTS = 8
SC_SHARE = (57, 128)


def _sc_erf(x):
    ax = jnp.abs(x)
    t = 1.0 / (1.0 + 0.3275911 * ax)
    poly = t * (0.254829592 + t * (-0.284496736 + t * (1.421413741 + t * (-1.453152027 + t * 1.061405429))))
    y = 1.0 - poly * jnp.exp(-ax * ax)
    return jnp.where(x < 0, -y, y)


def _peer_sc(ids_flat, gates_tok, xn, table, ts):
    d = D_MODEL
    lanes, rows = SC_LANES, SC_ROWS
    nchunk = PEER_SLOTS // rows
    block = lanes * SC_COLS
    per_worker = ts // SC_WORKERS
    assert ts % SC_WORKERS == 0 and d % block == 0
    mesh = plsc.VectorSubcoreMesh(core_axis_name="c", subcore_axis_name="s")

    @functools.partial(
        pl.kernel, mesh=mesh,
        out_type=jax.ShapeDtypeStruct((ts, d), F32),
        scratch_types=[
            pltpu.VMEM((PEER_SLOTS,), jnp.int32),
            pltpu.VMEM((PEER_SLOTS,), F32),
            pltpu.VMEM((d,), F32),
            pltpu.VMEM((d,), F32),
            pltpu.VMEM((2, rows, 1, d), jnp.uint32),
            pltpu.VMEM((rows, lanes), F32),
            pltpu.VMEM((lanes,), F32),
            pltpu.VMEM((rows, lanes), jnp.int32),
            pltpu.SemaphoreType.DMA((2,)),
        ],
        compiler_params=pltpu.CompilerParams(needs_layout_passes=False),
        name="peer_sparsecore",
    )
    def run(ids_hbm, gates_hbm, xn_hbm, tbl_hbm, y_hbm, ids_v, gates_v, x_v, y_v, rows_v, acc_v, coeff_v, ridx_v, sem):
        worker = lax.axis_index("s") * SC_CORES + lax.axis_index("c")
        lane = lax.iota(jnp.int32, lanes)
        zero = jnp.zeros((lanes,), F32)
        hi_mask = jnp.full((lanes,), 0xFFFF0000, jnp.uint32)
        for r in range(rows):
            ridx_v[r, :] = jnp.full((lanes,), r, jnp.int32)

        def gather(c, slot):
            return pltpu.make_async_copy(tbl_hbm.at[ids_v[pl.ds(c * rows, rows)]], rows_v.at[slot], sem.at[slot])

        @pl.loop(0, per_worker)
        def _(i):
            tok = worker * per_worker + i
            pltpu.sync_copy(ids_hbm.at[pl.ds(tok * PEER_SLOTS, PEER_SLOTS)], ids_v)
            pltpu.sync_copy(gates_hbm.at[pl.ds(tok * PEER_SLOTS, PEER_SLOTS)], gates_v)
            pltpu.sync_copy(xn_hbm.at[tok], x_v)

            @pl.loop(0, d // lanes)
            def _(j):
                y_v[pl.ds(j * lanes, lanes)] = zero

            gather(0, 0).start()
            for c in range(nchunk):
                slot = c % 2
                if c + 1 < nchunk:
                    gather(c + 1, 1 - slot).start()
                gather(c, slot).wait()
                for r0 in range(0, rows, SC_DOT_ROWS):
                    def dot_body(j, accs, r0=r0, slot=slot):
                        xj = x_v[pl.ds(j * lanes, lanes)]
                        return tuple(
                            accs[q] + plsc.bitcast(rows_v[slot, r0 + q, 0, pl.ds(j * lanes, lanes)] << 16, F32) * xj
                            for q in range(SC_DOT_ROWS))
                    accs = lax.fori_loop(0, d // lanes, dot_body, (zero,) * SC_DOT_ROWS, unroll=4)
                    for q in range(SC_DOT_ROWS):
                        acc_v[r0 + q, :] = accs[q]
                a = zero
                for col in range(lanes):
                    a = a + plsc.load_gather(acc_v, [lane, ridx_v[col, :]])
                gate = gates_v[pl.ds(c * rows, rows)]
                coeff_v[...] = 0.5 * a * (1.0 + _sc_erf(a * (2.0 ** -0.5))) * gate
                coeffs = [plsc.load_gather(coeff_v, [ridx_v[r, :]]) for r in range(rows)]

                @pl.loop(0, d // block)
                def _(jb, slot=slot, coeffs=coeffs):
                    base = jb * block
                    accs = [zero] * SC_COLS
                    for r in range(rows):
                        for q in range(SC_COLS):
                            w = rows_v[slot, r, 0, pl.ds(base + q * lanes, lanes)]
                            accs[q] = accs[q] + plsc.bitcast(w & hi_mask, F32) * coeffs[r]
                    for q in range(SC_COLS):
                        plsc.addupdate(y_v.at[pl.ds(base + q * lanes, lanes)], accs[q])

            pltpu.sync_copy(y_v, y_hbm.at[tok])

    return run(ids_flat, gates_tok, xn, table)


def _rms_rows_kernel(x_ref, g_ref, o_ref):
    o_ref[...] = _rms(x_ref[...], g_ref[...])


def _rms_rows(x, gain, rows, tm):
    d = x.shape[1]
    return pl.pallas_call(
        _rms_rows_kernel,
        grid=(rows // tm,),
        in_specs=[pl.BlockSpec((tm, d), lambda i: (i, 0)), pl.BlockSpec((1, d), lambda i: (0, 0))],
        out_specs=pl.BlockSpec((tm, d), lambda i: (i, 0)),
        out_shape=jax.ShapeDtypeStruct((rows, d), F32),
        compiler_params=_params("parallel"),
        name="rms_rows",
    )(x, gain.reshape(1, d))


def _peer_finish_kernel(x2_ref, y_ref, og_ref, prev_ref, out_ref, *, final_norm):
    del prev_ref
    x3 = x2_ref[...] + y_ref[...]
    out_ref[...] = _rms(x3, og_ref[...]) if final_norm else x3


def _peer_finish(x2, y_head, out_rest, final_gain, final_norm, out_block0):
    rows, d = y_head.shape
    tm = GATHER_TOKENS
    blk = pl.BlockSpec((tm, d), lambda i: (i, 0))
    return pl.pallas_call(
        functools.partial(_peer_finish_kernel, final_norm=final_norm),
        grid=(rows // tm,),
        in_specs=[blk, blk, pl.BlockSpec((1, d), lambda i: (0, 0)), pl.BlockSpec(memory_space=pl.ANY)],
        out_specs=pl.BlockSpec((tm, d), lambda i: (i + out_block0, 0)),
        out_shape=jax.ShapeDtypeStruct(out_rest.shape, F32),
        input_output_aliases={3: 0},
        compiler_params=_params("parallel"),
        name="peer_finish",
    )(x2, y_head, final_gain.reshape(1, d), out_rest)


def _rope_angles(positions, dh):
    inv_freq = ROPE_THETA ** (-jnp.arange(0, dh, 2, dtype=F32) / dh)
    ang = positions.astype(F32).reshape(-1, 1) * inv_freq
    return jnp.cos(ang), jnp.sin(ang)


def _rope_tables_att(positions):
    cos, sin = _rope_angles(positions, HEAD_DIM)
    copies = LANES // HEAD_DIM
    return (jnp.tile(jnp.concatenate([cos, cos], axis=1), (1, copies)),
            jnp.tile(jnp.concatenate([-sin, sin], axis=1), (1, copies)))


def _pack_table_kernel(u_ref, v_ref, o_ref):
    ub = lax.bitcast_convert_type(u_ref[...].astype(BF16).astype(F32), jnp.uint32)
    vb = lax.bitcast_convert_type(v_ref[...].astype(BF16).astype(F32), jnp.uint32)
    o_ref[:, 0, :] = (ub >> 16) | vb


def _pack_expert_table(u, v, rows=256):
    n, d = u.shape
    blk = pl.BlockSpec((rows, d), lambda i: (i, 0))
    return pl.pallas_call(
        _pack_table_kernel,
        grid=(n // rows,),
        in_specs=[blk, blk],
        out_specs=pl.BlockSpec((rows, 1, d), lambda i: (i, 0, 0)),
        out_shape=jax.ShapeDtypeStruct((n, 1, d), jnp.uint32),
        compiler_params=_params("parallel"),
        name="pack_table",
    )(u, v)


def kernel(x, mem, positions, mix_norm_gain, w_in, att_sinks, att_out_gain, ret_out_gain, w_out,
           cross_norm_gain, mem_norm_gain, w_xq, w_xk, w_xv, w_xo, ffn_norm_gain,
           w_peer_q, peer_sub_keys, peer_u, peer_v, final_norm_gain):
    batch, seq, d = x.shape
    mem_len = mem.shape[1]
    t = batch * seq
    depth = w_in.shape[0]
    assert d == D_MODEL and seq % BLOCK == 0 and t % GATHER_TOKENS == 0

    cos_a, sin_a = _rope_tables_att(positions)
    cos_r, sin_r = _rope_angles(positions, RET_HEAD_DIM)
    xf = x.reshape(t, d)
    memf = mem.reshape(batch * mem_len, d)
    nblocks = t // GATHER_TOKENS
    parts = SC_PARTS if (batch % SC_PARTS == 0 and nblocks % SC_SHARE[1] == 0) else 1
    bp, tp = batch // parts, t // parts
    blocks_p = tp // GATHER_TOKENS
    sc_total = nblocks * SC_SHARE[0] // SC_SHARE[1] if parts > 1 else 0
    tm_in = min(1024, tp)
    tm_mid = min(256, seq)
    for l in range(depth):
        sc_left = sc_total
        w_in_b = _reorder_w_in(w_in[l]).astype(BF16)
        w_out_b, w_xq_b, w_xo_b, w_pq_b = (w.astype(BF16) for w in (w_out[l], w_xq[l], w_xo[l], w_peer_q[l]))
        keys_b = peer_sub_keys[l].astype(BF16)
        w_kv = jnp.concatenate([w_xk[l], w_xv[l]], axis=1).astype(BF16)
        kv_mem = _norm_matmul(memf, mem_norm_gain[l], w_kv, mem_len, 2 * XATT_WIDTH, BF16)
        table = _pack_expert_table(peer_u[l], peer_v[l])
        last = l == depth - 1
        out = None
        pending = []
        for p in range(parts):
            r0 = p * tp
            rows = slice(r0, r0 + tp)
            h = _norm_matmul(xf, mix_norm_gain[l], w_in_b, tm_in, 1792, BF16, row0=r0, rows=tp)
            oa = _swa(h, cos_a[rows], sin_a[rows], att_sinks[l], att_out_gain[l], bp, seq)
            orr = _retention(h, cos_r[rows], sin_r[rows], ret_out_gain[l], bp, seq)
            x1, qx = _outproj(xf, oa, orr, w_out_b, cross_norm_gain[l], w_xq_b, tm_mid, row0=r0)
            x2, pq = _xattn(qx, kv_mem, x1, w_xo_b, ffn_norm_gain[l], w_pq_b, bp, seq, mem_len, tm_mid,
                            batch0=p * bp)
            ids_t, gates_t = _peer_topk(pq, keys_b, LANES, PEER_HEADS)
            ids = ids_t.reshape(PEER_SLOTS, tp).T.reshape(tp * PEER_SLOTS)
            gates_t = gates_t.reshape(PEER_SLOTS, tp)
            sc_blocks = min(sc_left, blocks_p)
            sc_left -= sc_blocks
            if sc_blocks:
                ts = sc_blocks * GATHER_TOKENS
                xn_head = _rms_rows(x2, ffn_norm_gain[l], ts, GATHER_TOKENS)
                y_head = _peer_sc(ids, gates_t[:, :ts].T.reshape(ts * PEER_SLOTS), xn_head, table, ts)
                pending.append((x2, y_head, p * blocks_p))
            if sc_blocks < blocks_p:
                out = _peer_gather(ids, table, x2, gates_t, ffn_norm_gain[l], final_norm_gain, last, sc_blocks,
                                   t, p * blocks_p, out)
        for x2, y_head, block0 in pending:
            out = _peer_finish(x2, y_head, out, final_norm_gain, last, block0)
        xf = out
    return xf.reshape(batch, seq, d)
```

```python
import functools
import math

import jax
import jax.numpy as jnp
from jax import lax
from jax.experimental import pallas as pl
from jax.experimental.pallas import tpu as pltpu
from jax.experimental.pallas import tpu_sc as plsc

F32 = jnp.float32
BF16 = jnp.bfloat16

EPS = 1e-6
GN_EPS = 1e-5
ROPE_THETA = 10000.0
D_MODEL = 2048
HEAD_DIM = 64
ATT_Q_HEADS = 16
ATT_KV_HEADS = 2
ATT_WIDTH = ATT_Q_HEADS * HEAD_DIM
ATT_KV_WIDTH = ATT_KV_HEADS * HEAD_DIM
BLOCK = 128
RET_HEADS = 4
RET_HEAD_DIM = 256
RET_WIDTH = RET_HEADS * RET_HEAD_DIM
RET_CHUNK = 128
IN_COLS = ATT_WIDTH + 2 * ATT_KV_WIDTH + 4 * RET_WIDTH
XATT_HEADS = 4
XATT_HEAD_DIM = 128
XATT_WIDTH = XATT_HEADS * XATT_HEAD_DIM
PEER_HEADS = 8
PEER_N_KEYS = 128
PEER_HALF_DIM = 128
PEER_TOPK = 16
PEER_SLOTS = PEER_HEADS * PEER_TOPK

LANES = 128
SUBLANES = 8
VMEM_LIMIT = 56 * 1024 * 1024

_REF_QR0 = ATT_WIDTH + 2 * ATT_KV_WIDTH
_QR0 = 0
_KR0 = _QR0 + RET_WIDTH
_VR0 = _KR0 + RET_WIDTH
_GR0 = _VR0 + RET_WIDTH
_QA0 = _GR0 + RET_WIDTH
_KA0 = _QA0 + ATT_WIDTH
_VA0 = _KA0 + ATT_KV_WIDTH


def _reorder_w_in(w):
    return jnp.concatenate([w[:, _REF_QR0:], w[:, :_REF_QR0]], axis=1)


def _params(*sem):
    return pltpu.CompilerParams(dimension_semantics=sem, vmem_limit_bytes=VMEM_LIMIT)


def _rms(xf, gain):
    ms = jnp.mean(xf * xf, axis=-1, keepdims=True)
    return xf * lax.rsqrt(ms + EPS) * gain


def _dot(a, b):
    return jnp.dot(a, b, preferred_element_type=F32)


def _dot_nt(a, b):
    return lax.dot_general(a, b, (((1,), (1,)), ((), ())), preferred_element_type=F32)


def _norm_matmul_kernel(x_ref, g_ref, w_ref, o_ref, xn_ref):
    @pl.when(pl.program_id(1) == 0)
    def _():
        xn_ref[...] = _rms(x_ref[...], g_ref[...]).astype(BF16)

    o_ref[...] = _dot(xn_ref[...], w_ref[...]).astype(o_ref.dtype)


def _norm_matmul(x, gain, w_bf16, tm, tn, out_dtype, row0=0, rows=None):
    d = x.shape[1]
    t = x.shape[0] if rows is None else rows
    n = w_bf16.shape[1]
    blk0 = row0 // tm
    return pl.pallas_call(
        _norm_matmul_kernel,
        grid=(t // tm, n // tn),
        in_specs=[
            pl.BlockSpec((tm, d), lambda i, j: (i + blk0, 0)),
            pl.BlockSpec((1, d), lambda i, j: (0, 0)),
            pl.BlockSpec((d, tn), lambda i, j: (0, j)),
        ],
        out_specs=pl.BlockSpec((tm, tn), lambda i, j: (i, j)),
        out_shape=jax.ShapeDtypeStruct((t, n), out_dtype),
        scratch_shapes=[pltpu.VMEM((tm, d), BF16)],
        compiler_params=_params("parallel", "arbitrary"),
        name="norm_matmul",
    )(x, gain.reshape(1, d), w_bf16)


def _swa_kernel(sink_ref, q_ref, kc_ref, kp_ref, vc_ref, vp_ref, cc_ref, sc_ref, cp_ref, sp_ref,
                gain_ref, o_ref):
    n = pl.program_id(1)
    lane = lax.broadcasted_iota(jnp.int32, (1, LANES), 1)
    first_half = (lane % HEAD_DIM) < (HEAD_DIM // 2)
    lo = lane < HEAD_DIM

    def rope(x, c, s):
        partner = jnp.where(first_half, pltpu.roll(x, LANES - HEAD_DIM // 2, 1),
                            pltpu.roll(x, HEAD_DIM // 2, 1))
        return x * c + partner * s

    cc, sc = cc_ref[...], sc_ref[...]
    k = jnp.concatenate([rope(kp_ref[...].astype(F32), cp_ref[...], sp_ref[...]),
                         rope(kc_ref[...].astype(F32), cc, sc)], axis=0)
    v = jnp.concatenate([vp_ref[...].astype(F32), vc_ref[...].astype(F32)], axis=0)
    k_sw = pltpu.roll(k, HEAD_DIM, 1)
    v_sw = pltpu.roll(v, HEAD_DIM, 1)

    def place(a, a_sw, c, half):
        src = a if c == half else a_sw
        keep = lo if half == 0 else jnp.logical_not(lo)
        return jnp.where(keep, src, 0.0).astype(BF16)

    kvar = {(c, h): place(k, k_sw, c, h) for c in range(ATT_KV_HEADS) for h in range(2)}
    vvar = {(c, h): place(v, v_sw, c, h) for c in range(ATT_KV_HEADS) for h in range(2)}

    qi = lax.broadcasted_iota(jnp.int32, (BLOCK, 2 * BLOCK), 0) + BLOCK
    ki = lax.broadcasted_iota(jnp.int32, (BLOCK, 2 * BLOCK), 1)
    dist = qi - ki
    kmin = jnp.where(n > 0, 0, BLOCK)
    valid = (dist >= 0) & (dist < BLOCK) & (ki >= kmin)

    group = ATT_Q_HEADS // ATT_KV_HEADS
    for j in range(ATT_WIDTH // LANES):
        c = (2 * j) // group
        cols = slice(j * LANES, (j + 1) * LANES)
        qg = (rope(q_ref[:, cols].astype(F32), cc, sc) * (HEAD_DIM ** -0.5)).astype(BF16)
        out = jnp.zeros((BLOCK, LANES), F32)
        for half in range(2):
            s = jnp.where(valid, _dot_nt(qg, kvar[(c, half)]), -jnp.inf)
            sink = sink_ref[2 * j + half]
            m = jnp.maximum(jnp.max(s, axis=-1, keepdims=True), sink)
            p = jnp.exp(s - m)
            denom = jnp.sum(p, axis=-1, keepdims=True) + jnp.exp(sink - m)
            out = out + _dot((p / denom).astype(BF16), vvar[(c, half)])
        sq = out * out
        ss_lo = jnp.sum(jnp.where(lo, sq, 0.0), axis=-1, keepdims=True)
        ss_hi = jnp.sum(jnp.where(lo, 0.0, sq), axis=-1, keepdims=True)
        ms = jnp.where(lo, ss_lo, ss_hi) * (1.0 / HEAD_DIM)
        o_ref[:, cols] = (out * lax.rsqrt(ms + EPS) * gain_ref[:, cols]).astype(o_ref.dtype)


def _swa(h, cos_a, sin_a, sinks, gain, batch, seq):
    nb = seq // BLOCK
    t = batch * seq
    kcol, vcol = _KA0 // LANES, _VA0 // LANES
    cur = lambda b, n: b * nb + n
    prev = lambda b, n: b * nb + jnp.maximum(n - 1, 0)
    row = lambda col, f: pl.BlockSpec((BLOCK, LANES), lambda b, n: (f(b, n), col))
    return pl.pallas_call(
        _swa_kernel,
        grid=(batch, nb),
        in_specs=[
            pl.BlockSpec(memory_space=pltpu.SMEM),
            pl.BlockSpec((BLOCK, ATT_WIDTH), lambda b, n: (cur(b, n), _QA0 // ATT_WIDTH)),
            row(kcol, cur), row(kcol, prev), row(vcol, cur), row(vcol, prev),
            row(0, cur), row(0, cur), row(0, prev), row(0, prev),
            pl.BlockSpec((1, ATT_WIDTH), lambda b, n: (0, 0)),
        ],
        out_specs=pl.BlockSpec((BLOCK, ATT_WIDTH), lambda b, n: (cur(b, n), 0)),
        out_shape=jax.ShapeDtypeStruct((t, ATT_WIDTH), BF16),
        compiler_params=_params("parallel", "parallel"),
        name="swa_attention",
    )(sinks, h, h, h, h, h, cos_a, sin_a, cos_a, sin_a, gain.reshape(1, ATT_WIDTH))


def _ret_kernel(lg_ref, cd_ref, q_ref, k_ref, v_ref, g_ref, c_ref, s_ref, gain_ref, o_ref, state_ref):
    n = pl.program_id(1)

    @pl.when(n == 0)
    def _():
        state_ref[...] = jnp.zeros_like(state_ref)

    c, s = c_ref[...], s_ref[...]
    hd = RET_HEAD_DIM
    half = hd // 2

    def rope(x):
        x1, x2 = x[:, :half], x[:, half:]
        return jnp.concatenate([x1 * c - x2 * s, x2 * c + x1 * s], axis=1)

    ri = lax.broadcasted_iota(jnp.int32, (RET_CHUNK, RET_CHUNK), 0).astype(F32)
    ci = lax.broadcasted_iota(jnp.int32, (RET_CHUNK, RET_CHUNK), 1).astype(F32)
    diff = ri - ci
    for hh in range(RET_HEADS):
        cols = slice(hh * hd, (hh + 1) * hd)
        lg = lg_ref[hh]
        q = rope(q_ref[:, cols].astype(F32))
        k = rope(k_ref[:, cols].astype(F32)) * (hd ** -0.5)
        v = v_ref[:, cols].astype(BF16)

        decay = jnp.where(diff >= 0, jnp.exp(jnp.maximum(diff, 0.0) * lg), 0.0)
        zeta = jnp.exp((RET_CHUNK - 1 - ri) * lg)
        xi = jnp.exp((ri + 1.0) * lg)
        zeta2 = jnp.concatenate([zeta, zeta], axis=1)
        xi2 = jnp.concatenate([xi, xi], axis=1)

        inner = _dot_nt(q.astype(BF16), k.astype(BF16)) * decay
        state = state_ref[hh]
        out = _dot(inner.astype(BF16), v) + _dot((q * xi2).astype(BF16), state.astype(BF16))
        kv = lax.dot_general((k * zeta2).astype(BF16), v, (((0,), (0,)), ((), ())), preferred_element_type=F32)
        state_ref[hh] = state * cd_ref[hh] + kv

        mu = jnp.mean(out, axis=-1, keepdims=True)
        cen = out - mu
        var = jnp.mean(cen * cen, axis=-1, keepdims=True)
        g = g_ref[:, cols].astype(F32)
        o = cen * lax.rsqrt(var + GN_EPS) * gain_ref[:, cols] * (g * jax.nn.sigmoid(g))
        o_ref[:, cols] = o.astype(o_ref.dtype)


def _retention(h, cos_r, sin_r, gain, batch, seq):
    nc = seq // RET_CHUNK
    t = batch * seq
    hd = RET_HEAD_DIM
    log_gamma = jnp.log1p(-jnp.exp2(-5.0 - jnp.arange(RET_HEADS, dtype=F32)))
    chunk_decay = jnp.exp(RET_CHUNK * log_gamma)
    col = lambda c0: pl.BlockSpec((RET_CHUNK, RET_WIDTH), lambda b, n: (b * nc + n, c0 // RET_WIDTH))
    tab = pl.BlockSpec((RET_CHUNK, hd // 2), lambda b, n: (b * nc + n, 0))
    return pl.pallas_call(
        _ret_kernel,
        grid=(batch, nc),
        in_specs=[
            pl.BlockSpec(memory_space=pltpu.SMEM),
            pl.BlockSpec(memory_space=pltpu.SMEM),
            col(_QR0), col(_KR0), col(_VR0), col(_GR0), tab, tab,
            pl.BlockSpec((1, RET_WIDTH), lambda b, n: (0, 0)),
        ],
        out_specs=pl.BlockSpec((RET_CHUNK, RET_WIDTH), lambda b, n: (b * nc + n, 0)),
        out_shape=jax.ShapeDtypeStruct((t, RET_WIDTH), BF16),
        scratch_shapes=[pltpu.VMEM((RET_HEADS, hd, hd), F32)],
        compiler_params=_params("parallel", "arbitrary"),
        name="retention",
    )(log_gamma, chunk_decay, h, h, h, h, cos_r, sin_r, gain.reshape(1, RET_WIDTH))


def _outproj_kernel(x_ref, oa_ref, or_ref, wa_ref, wr_ref, g_ref, wq_ref, x1_ref, qx_ref):
    x1 = x_ref[...] + _dot(oa_ref[...], wa_ref[...]) + _dot(or_ref[...], wr_ref[...])
    x1_ref[...] = x1
    qx_ref[...] = _dot(_rms(x1, g_ref[...]).astype(BF16), wq_ref[...]).astype(qx_ref.dtype)


def _outproj(x, oa, orr, w_out_bf16, gain, w_xq_bf16, tm, row0=0):
    t, d = oa.shape[0], x.shape[1]
    blk0 = row0 // tm
    full = lambda shape: pl.BlockSpec(shape, lambda i: (0, 0))
    return pl.pallas_call(
        _outproj_kernel,
        grid=(t // tm,),
        in_specs=[
            pl.BlockSpec((tm, d), lambda i: (i + blk0, 0)),
            pl.BlockSpec((tm, ATT_WIDTH), lambda i: (i, 0)),
            pl.BlockSpec((tm, RET_WIDTH), lambda i: (i, 0)),
            pl.BlockSpec((ATT_WIDTH, d), lambda i: (0, 0)),
            pl.BlockSpec((RET_WIDTH, d), lambda i: (1, 0)),
            full((1, d)),
            full((d, XATT_WIDTH)),
        ],
        out_specs=[pl.BlockSpec((tm, d), lambda i: (i, 0)), pl.BlockSpec((tm, XATT_WIDTH), lambda i: (i, 0))],
        out_shape=[jax.ShapeDtypeStruct((t, d), F32), jax.ShapeDtypeStruct((t, XATT_WIDTH), BF16)],
        compiler_params=_params("parallel"),
        name="out_proj",
    )(x, oa, orr, w_out_bf16, w_out_bf16, gain.reshape(1, d), w_xq_bf16)


def _xattn_kernel(qx_ref, k_ref, v_ref, x1_ref, wo_ref, g_ref, wpq_ref, x2_ref, pq_ref):
    heads = []
    for hh in range(XATT_HEADS):
        cols = slice(hh * XATT_HEAD_DIM, (hh + 1) * XATT_HEAD_DIM)
        s = _dot_nt(qx_ref[:, cols], k_ref[:, cols]) * (XATT_HEAD_DIM ** -0.5)
        p = jnp.exp(s - jnp.max(s, axis=-1, keepdims=True))
        p = p / jnp.sum(p, axis=-1, keepdims=True)
        heads.append(_dot(p.astype(BF16), v_ref[:, cols]))
    o = jnp.concatenate(heads, axis=1).astype(BF16)
    x2 = x1_ref[...] + _dot(o, wo_ref[...])
    x2_ref[...] = x2
    pq_ref[...] = _dot(_rms(x2, g_ref[...]).astype(BF16), wpq_ref[...])


def _xattn(qx, kv_mem, x1, w_xo_bf16, gain, w_pq_bf16, batch, seq, mem_len, tm, batch0=0):
    t, d = x1.shape
    npq = w_pq_bf16.shape[1]
    nt = seq // tm
    rows = lambda width: pl.BlockSpec((tm, width), lambda b, i: (b * nt + i, 0))
    full = lambda shape: pl.BlockSpec(shape, lambda b, i: (0, 0))
    return pl.pallas_call(
        _xattn_kernel,
        grid=(batch, nt),
        in_specs=[
            rows(XATT_WIDTH),
            pl.BlockSpec((mem_len, XATT_WIDTH), lambda b, i: (b + batch0, 0)),
            pl.BlockSpec((mem_len, XATT_WIDTH), lambda b, i: (b + batch0, 1)),
            rows(d),
            full((XATT_WIDTH, d)),
            full((1, d)),
            full((d, npq)),
        ],
        out_specs=[rows(d), rows(npq)],
        out_shape=[jax.ShapeDtypeStruct((t, d), F32), jax.ShapeDtypeStruct((t, npq), F32)],
        compiler_params=_params("parallel", "parallel"),
        name="cross_attention",
    )(qx, kv_mem, kv_mem, x1, w_xo_bf16, gain.reshape(1, d), w_pq_bf16)


def _topk_rows(s, k, payload=None):
    nrows = s.shape[0]
    rows = lax.broadcasted_iota(jnp.int32, s.shape, 0).astype(F32)
    vals, sel = [], []
    for _ in range(k):
        m = jnp.max(s, axis=0, keepdims=True)
        am = jnp.min(jnp.where(s == m, rows, float(nrows)), axis=0, keepdims=True)
        hit = rows == am
        vals.append(m)
        sel.append(am if payload is None else jnp.max(jnp.where(hit, payload, -1.0), axis=0, keepdims=True))
        s = jnp.where(hit, -jnp.inf, s)
    return jnp.concatenate(vals, axis=0), jnp.concatenate(sel, axis=0)


def _staircase(t1, t2):
    kk, half = PEER_TOPK, SUBLANES
    blocks = [(t1[0:1, :], t2, kk), (t1[1:2, :], t2[0:half, :], half)]
    blocks += [(t1[a:a + 1, :], t2[0:half, :], kk // (a + 1)) for a in range(2, half)]
    blocks.append((t1[half:kk, :], t2[0:1, :], half))
    return blocks


def _peer_topk_kernel(q_ref, keys_ref, ids_ref, gates_ref, *, heads):
    width = 2 * PEER_HALF_DIM
    row8 = lax.broadcasted_iota(jnp.int32, (SUBLANES, q_ref.shape[0]), 0)
    for hh in range(heads):
        tops = []
        for p in range(2):
            c0 = hh * width + p * PEER_HALF_DIM
            qp = q_ref[:, c0:c0 + PEER_HALF_DIM].astype(BF16)
            tops.append(_topk_rows(_dot_nt(keys_ref[hh, p], qp), PEER_TOPK))
        (s1, i1), (s2, i2) = tops
        cand_s, cand_i = [], []
        for (a_s, b_s, live), (a_i, b_i, _) in zip(_staircase(s1, s2), _staircase(i1 * float(PEER_N_KEYS), i2)):
            blk = a_s + b_s
            if live < blk.shape[0]:
                blk = jnp.where(row8 < live, blk, -jnp.inf)
            cand_s.append(blk)
            cand_i.append(a_i + b_i)
        top_s, top_e = _topk_rows(jnp.concatenate(cand_s, axis=0), PEER_TOPK,
                                  payload=jnp.concatenate(cand_i, axis=0))
        e = jnp.exp(top_s - top_s[0:1, :])
        ids_ref[hh] = top_e.astype(jnp.int32)
        gates_ref[hh] = e / jnp.sum(e, axis=0, keepdims=True)


def _peer_topk(pq, keys_bf16, tt, heads):
    t = pq.shape[0]
    width = 2 * PEER_HALF_DIM
    out = pl.BlockSpec((heads, PEER_TOPK, tt), lambda i, hh: (hh, 0, i))
    return pl.pallas_call(
        functools.partial(_peer_topk_kernel, heads=heads),
        grid=(t // tt, PEER_HEADS // heads),
        in_specs=[
            pl.BlockSpec((tt, heads * width), lambda i, hh: (i, hh)),
            pl.BlockSpec((heads, 2, PEER_N_KEYS, PEER_HALF_DIM), lambda i, hh: (hh, 0, 0, 0)),
        ],
        out_specs=[out, out],
        out_shape=[jax.ShapeDtypeStruct((PEER_HEADS, PEER_TOPK, t), jnp.int32),
                   jax.ShapeDtypeStruct((PEER_HEADS, PEER_TOPK, t), F32)],
        compiler_params=_params("parallel", "parallel"),
        name="peer_topk",
    )(pq, keys_bf16)


GATHER_TOKENS = 128
GATHER_BUFS = 8
GATHER_GROUP = 8
assert GATHER_GROUP % GATHER_BUFS == 0 and GATHER_GROUP % SUBLANES == 0 and GATHER_TOKENS % GATHER_GROUP == 0


def _peer_gather_kernel(ids_hbm, tbl_hbm, x2_ref, gates_ref, fg_ref, og_ref, *rest, final_norm, first_block,
                        has_prev):
    out_ref, ids_smem, ids_sem, *scratch = rest[1:] if has_prev else rest
    gbufs = scratch[:GATHER_BUFS]
    gsem, xn_ref, y_ref, yacc_ref = scratch[GATHER_BUFS:]
    i = pl.program_id(0)
    has_next = i + 1 < pl.num_programs(0)
    slot = i % 2
    other = 1 - slot
    tb = GATHER_TOKENS
    nchunk = D_MODEL // LANES
    nids = tb * PEER_SLOTS
    ahead = GATHER_BUFS - 1
    group = GATHER_GROUP

    def ids_copy(step, sl):
        return pltpu.make_async_copy(ids_hbm.at[pl.ds(pl.multiple_of((first_block + step) * nids, nids), nids)],
                                     ids_smem.at[pl.ds(pl.multiple_of(sl * nids, nids), nids)], ids_sem.at[sl])

    def row_copy(e, b, k):
        return pltpu.make_async_copy(tbl_hbm.at[e], gbufs[b].at[pl.ds(k, 1)], gsem.at[b])

    def issue(sl, tok, b):
        base = sl * nids + tok * PEER_SLOTS
        for k in range(PEER_SLOTS):
            row_copy(ids_smem[base + k], b, k).start(priority=k % 2)

    def wait_rows(b):
        for k in range(PEER_SLOTS):
            row_copy(0, b, k).wait()

    @pl.when(i == 0)
    def _():
        ids_copy(0, 0).start()
        ids_copy(0, 0).wait()
        for tok in range(ahead):
            issue(0, tok, tok)

    @pl.when(has_next)
    def _():
        ids_copy(i + 1, other).start()

    xn_ref[...] = _rms(x2_ref[...], fg_ref[...])

    lane = lax.broadcasted_iota(jnp.int32, (PEER_SLOTS, LANES), 1)
    hi_mask = jnp.uint32(0xFFFF0000)

    def compute(tok, r, b):
        gbuf = gbufs[b]
        xrow = xn_ref[pl.ds(tok, 1), :]
        acc = jnp.zeros((PEER_SLOTS, LANES), F32)
        for c in range(nchunk):
            w = gbuf[:, c * LANES:(c + 1) * LANES]
            u = lax.bitcast_convert_type(w << 16, F32)
            acc = acc + u * xrow[:, c * LANES:(c + 1) * LANES]
        a = jnp.sum(acc, axis=-1, keepdims=True)
        gate = jnp.sum(jnp.where(lane == tok, gates_ref[...], 0.0), axis=-1, keepdims=True)
        coeff = 0.5 * a * (1.0 + lax.erf(a * (2.0 ** -0.5))) * gate
        for c in range(nchunk):
            w = gbuf[:, c * LANES:(c + 1) * LANES]
            vv = lax.bitcast_convert_type(w & hi_mask, F32)
            yacc_ref[r:r + 1, c * LANES:(c + 1) * LANES] = jnp.sum(vv * coeff, axis=0, keepdims=True)

    def run_group(g, last):
        for r in range(group):
            b = r % GATHER_BUFS
            nb = (r + ahead) % GATHER_BUFS
            tok = g * group + r
            wait_rows(b)
            if not last or r + ahead < group:
                issue(slot, tok + ahead, nb)
            else:
                if r + ahead == group:
                    @pl.when(has_next)
                    def _():
                        ids_copy(i + 1, other).wait()

                @pl.when(has_next)
                def _():
                    issue(other, r + ahead - group, nb)
            compute(tok, r, b)
        y_ref[pl.ds(pl.multiple_of(g * group, group), group), :] = yacc_ref[...]

    def body(g, carry):
        run_group(g, last=False)
        return carry

    ngroups = tb // group
    lax.fori_loop(0, ngroups - 1, body, 0)
    run_group(ngroups - 1, last=True)
    x3 = x2_ref[...] + y_ref[...]
    out_ref[...] = _rms(x3, og_ref[...]) if final_norm else x3


def _peer_gather(ids_flat, table, x2, gates_t, ffn_gain, final_gain, final_norm, first_block, out_rows,
                 out_block0, prev):
    t, d = x2.shape
    tb = GATHER_TOKENS
    nsteps = t // tb - first_block
    full = lambda shape: pl.BlockSpec(shape, lambda i: (0, 0))
    has_prev = prev is not None
    return pl.pallas_call(
        functools.partial(_peer_gather_kernel, final_norm=final_norm, first_block=first_block, has_prev=has_prev),
        grid=(nsteps,),
        in_specs=[
            pl.BlockSpec(memory_space=pl.ANY),
            pl.BlockSpec(memory_space=pl.ANY),
            pl.BlockSpec((tb, d), lambda i: (i + first_block, 0)),
            pl.BlockSpec((PEER_SLOTS, tb), lambda i: (0, i + first_block)),
            full((1, d)),
            full((1, d)),
        ] + ([pl.BlockSpec(memory_space=pl.ANY)] if has_prev else []),
        out_specs=pl.BlockSpec((tb, d), lambda i: (i + first_block + out_block0, 0)),
        out_shape=jax.ShapeDtypeStruct((out_rows, d), F32),
        input_output_aliases={6: 0} if has_prev else {},
        scratch_shapes=[
            pltpu.SMEM((2 * tb * PEER_SLOTS,), jnp.int32),
            pltpu.SemaphoreType.DMA((2,)),
            *[pltpu.VMEM((PEER_SLOTS, d), jnp.uint32) for _ in range(GATHER_BUFS)],
            pltpu.SemaphoreType.DMA((GATHER_BUFS,)),
            pltpu.VMEM((tb, d), F32),
            pltpu.VMEM((tb, d), F32),
            pltpu.VMEM((GATHER_GROUP, d), F32),
        ],
        compiler_params=_params("arbitrary"),
        name="peer_gather",
    )(ids_flat, table, x2, gates_t, ffn_gain.reshape(1, d), final_gain.reshape(1, d), *([prev] if has_prev else []))


SC_CORES, SC_SUBCORES, SC_LANES = 2, 16, 16
SC_WORKERS = SC_CORES * SC_SUBCORES
SC_ROWS = SC_LANES
SC_COLS = 8
SC_DOT_ROWS = 8
SC_GROUPS, SC_GROUP_DENOM = (1, 1, 2, 4), 8
SC_SHARE = (57, 128)


def _sc_erf(x):
    ax = jnp.abs(x)
    t = 1.0 / (1.0 + 0.3275911 * ax)
    poly = t * (0.254829592 + t * (-0.284496736 + t * (1.421413741 + t * (-1.453152027 + t * 1.061405429))))
    y = 1.0 - poly * jnp.exp(-ax * ax)
    return jnp.where(x < 0, -y, y)


def _peer_sc(ids_flat, gates_tok, xn, table, ts):
    d = D_MODEL
    lanes, rows = SC_LANES, SC_ROWS
    nchunk = PEER_SLOTS // rows
    block = lanes * SC_COLS
    per_worker = ts // SC_WORKERS
    assert ts % SC_WORKERS == 0 and d % block == 0
    mesh = plsc.VectorSubcoreMesh(core_axis_name="c", subcore_axis_name="s")

    @functools.partial(
        pl.kernel, mesh=mesh,
        out_type=jax.ShapeDtypeStruct((ts, d), F32),
        scratch_types=[
            pltpu.VMEM((PEER_SLOTS,), jnp.int32),
            pltpu.VMEM((PEER_SLOTS,), F32),
            pltpu.VMEM((d,), F32),
            pltpu.VMEM((d,), F32),
            pltpu.VMEM((2, rows, 1, d), jnp.uint32),
            pltpu.VMEM((rows, lanes), F32),
            pltpu.VMEM((lanes,), F32),
            pltpu.VMEM((rows, lanes), jnp.int32),
            pltpu.SemaphoreType.DMA((2,)),
        ],
        compiler_params=pltpu.CompilerParams(needs_layout_passes=False),
        name="peer_sparsecore",
    )
    def run(ids_hbm, gates_hbm, xn_hbm, tbl_hbm, y_hbm, ids_v, gates_v, x_v, y_v, rows_v, acc_v, coeff_v, ridx_v, sem):
        worker = lax.axis_index("s") * SC_CORES + lax.axis_index("c")
        lane = lax.iota(jnp.int32, lanes)
        zero = jnp.zeros((lanes,), F32)
        hi_mask = jnp.full((lanes,), 0xFFFF0000, jnp.uint32)
        for r in range(rows):
            ridx_v[r, :] = jnp.full((lanes,), r, jnp.int32)

        def gather(c, slot):
            return pltpu.make_async_copy(tbl_hbm.at[ids_v[pl.ds(c * rows, rows)]], rows_v.at[slot], sem.at[slot])

        @pl.loop(0, per_worker)
        def _(i):
            tok = worker * per_worker + i
            pltpu.sync_copy(ids_hbm.at[pl.ds(tok * PEER_SLOTS, PEER_SLOTS)], ids_v)
            pltpu.sync_copy(gates_hbm.at[pl.ds(tok * PEER_SLOTS, PEER_SLOTS)], gates_v)
            pltpu.sync_copy(xn_hbm.at[tok], x_v)

            @pl.loop(0, d // lanes)
            def _(j):
                y_v[pl.ds(j * lanes, lanes)] = zero

            gather(0, 0).start()
            for c in range(nchunk):
                slot = c % 2
                if c + 1 < nchunk:
                    gather(c + 1, 1 - slot).start()
                gather(c, slot).wait()
                for r0 in range(0, rows, SC_DOT_ROWS):
                    def dot_body(j, accs, r0=r0, slot=slot):
                        xj = x_v[pl.ds(j * lanes, lanes)]
                        return tuple(
                            accs[q] + plsc.bitcast(rows_v[slot, r0 + q, 0, pl.ds(j * lanes, lanes)] << 16, F32) * xj
                            for q in range(SC_DOT_ROWS))
                    accs = lax.fori_loop(0, d // lanes, dot_body, (zero,) * SC_DOT_ROWS, unroll=4)
                    for q in range(SC_DOT_ROWS):
                        acc_v[r0 + q, :] = accs[q]
                a = zero
                for col in range(lanes):
                    a = a + plsc.load_gather(acc_v, [lane, ridx_v[col, :]])
                gate = gates_v[pl.ds(c * rows, rows)]
                coeff_v[...] = 0.5 * a * (1.0 + _sc_erf(a * (2.0 ** -0.5))) * gate
                coeffs = [plsc.load_gather(coeff_v, [ridx_v[r, :]]) for r in range(rows)]

                @pl.loop(0, d // block)
                def _(jb, slot=slot, coeffs=coeffs):
                    base = jb * block
                    accs = [zero] * SC_COLS
                    for r in range(rows):
                        for q in range(SC_COLS):
                            w = rows_v[slot, r, 0, pl.ds(base + q * lanes, lanes)]
                            accs[q] = accs[q] + plsc.bitcast(w & hi_mask, F32) * coeffs[r]
                    for q in range(SC_COLS):
                        plsc.addupdate(y_v.at[pl.ds(base + q * lanes, lanes)], accs[q])

            pltpu.sync_copy(y_v, y_hbm.at[tok])

    return run(ids_flat, gates_tok, xn, table)


def _rms_rows_kernel(x_ref, g_ref, o_ref):
    o_ref[...] = _rms(x_ref[...], g_ref[...])


def _rms_rows(x, gain, rows, tm):
    d = x.shape[1]
    return pl.pallas_call(
        _rms_rows_kernel,
        grid=(rows // tm,),
        in_specs=[pl.BlockSpec((tm, d), lambda i: (i, 0)), pl.BlockSpec((1, d), lambda i: (0, 0))],
        out_specs=pl.BlockSpec((tm, d), lambda i: (i, 0)),
        out_shape=jax.ShapeDtypeStruct((rows, d), F32),
        compiler_params=_params("parallel"),
        name="rms_rows",
    )(x, gain.reshape(1, d))


def _peer_finish_kernel(x2_ref, y_ref, og_ref, prev_ref, out_ref, *, final_norm):
    del prev_ref
    x3 = x2_ref[...] + y_ref[...]
    out_ref[...] = _rms(x3, og_ref[...]) if final_norm else x3


def _peer_finish(x2, y_head, out_rest, final_gain, final_norm, out_block0):
    rows, d = y_head.shape
    tm = GATHER_TOKENS
    blk = pl.BlockSpec((tm, d), lambda i: (i, 0))
    return pl.pallas_call(
        functools.partial(_peer_finish_kernel, final_norm=final_norm),
        grid=(rows // tm,),
        in_specs=[blk, blk, pl.BlockSpec((1, d), lambda i: (0, 0)), pl.BlockSpec(memory_space=pl.ANY)],
        out_specs=pl.BlockSpec((tm, d), lambda i: (i + out_block0, 0)),
        out_shape=jax.ShapeDtypeStruct(out_rest.shape, F32),
        input_output_aliases={3: 0},
        compiler_params=_params("parallel"),
        name="peer_finish",
    )(x2, y_head, final_gain.reshape(1, d), out_rest)


def _rope_angles(positions, dh):
    inv_freq = ROPE_THETA ** (-jnp.arange(0, dh, 2, dtype=F32) / dh)
    ang = positions.astype(F32).reshape(-1, 1) * inv_freq
    return jnp.cos(ang), jnp.sin(ang)


def _rope_tables_att(positions):
    cos, sin = _rope_angles(positions, HEAD_DIM)
    copies = LANES // HEAD_DIM
    return (jnp.tile(jnp.concatenate([cos, cos], axis=1), (1, copies)),
            jnp.tile(jnp.concatenate([-sin, sin], axis=1), (1, copies)))


def _pack_table_kernel(u_ref, v_ref, o_ref):
    ub = lax.bitcast_convert_type(u_ref[...].astype(BF16).astype(F32), jnp.uint32)
    vb = lax.bitcast_convert_type(v_ref[...].astype(BF16).astype(F32), jnp.uint32)
    o_ref[:, 0, :] = (ub >> 16) | vb


def _pack_expert_table(u, v, rows=256):
    n, d = u.shape
    blk = pl.BlockSpec((rows, d), lambda i: (i, 0))
    return pl.pallas_call(
        _pack_table_kernel,
        grid=(n // rows,),
        in_specs=[blk, blk],
        out_specs=pl.BlockSpec((rows, 1, d), lambda i: (i, 0, 0)),
        out_shape=jax.ShapeDtypeStruct((n, 1, d), jnp.uint32),
        compiler_params=_params("parallel"),
        name="pack_table",
    )(u, v)


def kernel(x, mem, positions, mix_norm_gain, w_in, att_sinks, att_out_gain, ret_out_gain, w_out,
           cross_norm_gain, mem_norm_gain, w_xq, w_xk, w_xv, w_xo, ffn_norm_gain,
           w_peer_q, peer_sub_keys, peer_u, peer_v, final_norm_gain):
    batch, seq, d = x.shape
    mem_len = mem.shape[1]
    t = batch * seq
    depth = w_in.shape[0]
    assert d == D_MODEL and seq % BLOCK == 0 and t % GATHER_TOKENS == 0

    cos_a, sin_a = _rope_tables_att(positions)
    cos_r, sin_r = _rope_angles(positions, RET_HEAD_DIM)
    xf = x.reshape(t, d)
    memf = mem.reshape(batch * mem_len, d)
    nblocks = t // GATHER_TOKENS
    split = batch % SC_GROUP_DENOM == 0 and nblocks % SC_SHARE[1] == 0
    groups = [batch * g // SC_GROUP_DENOM for g in SC_GROUPS] if split else [batch]
    sc_total = nblocks * SC_SHARE[0] // SC_SHARE[1] if split else 0
    tm_mid = min(256, seq)
    for l in range(depth):
        sc_left = sc_total
        w_in_b = _reorder_w_in(w_in[l]).astype(BF16)
        w_out_b, w_xq_b, w_xo_b, w_pq_b = (w.astype(BF16) for w in (w_out[l], w_xq[l], w_xo[l], w_peer_q[l]))
        keys_b = peer_sub_keys[l].astype(BF16)
        w_kv = jnp.concatenate([w_xk[l], w_xv[l]], axis=1).astype(BF16)
        kv_mem = _norm_matmul(memf, mem_norm_gain[l], w_kv, mem_len, 2 * XATT_WIDTH, BF16)
        table = _pack_expert_table(peer_u[l], peer_v[l])
        last = l == depth - 1
        out = None
        pending = []
        b0 = 0
        for bp in groups:
            r0, tp = b0 * seq, bp * seq
            block0, blocks_p = r0 // GATHER_TOKENS, tp // GATHER_TOKENS
            rows = slice(r0, r0 + tp)
            h = _norm_matmul(xf, mix_norm_gain[l], w_in_b, min(1024, tp), 1792, BF16, row0=r0, rows=tp)
            oa = _swa(h, cos_a[rows], sin_a[rows], att_sinks[l], att_out_gain[l], bp, seq)
            orr = _retention(h, cos_r[rows], sin_r[rows], ret_out_gain[l], bp, seq)
            x1, qx = _outproj(xf, oa, orr, w_out_b, cross_norm_gain[l], w_xq_b, tm_mid, row0=r0)
            x2, pq = _xattn(qx, kv_mem, x1, w_xo_b, ffn_norm_gain[l], w_pq_b, bp, seq, mem_len, tm_mid, batch0=b0)
            ids_t, gates_t = _peer_topk(pq, keys_b, LANES, PEER_HEADS)
            ids = ids_t.reshape(PEER_SLOTS, tp).T.reshape(tp * PEER_SLOTS)
            gates_t = gates_t.reshape(PEER_SLOTS, tp)
            sc_blocks = min(sc_left, blocks_p)
            sc_left -= sc_blocks
            if sc_blocks:
                ts = sc_blocks * GATHER_TOKENS
                xn_head = _rms_rows(x2, ffn_norm_gain[l], ts, GATHER_TOKENS)
                y_head = _peer_sc(ids, gates_t[:, :ts].T.reshape(ts * PEER_SLOTS), xn_head, table, ts)
                pending.append((x2, y_head, block0))
            if sc_blocks < blocks_p:
                out = _peer_gather(ids, table, x2, gates_t, ffn_norm_gain[l], final_norm_gain, last, sc_blocks,
                                   t, block0, out)
            b0 += bp
        for x2, y_head, block0 in pending:
            out = _peer_finish(x2, y_head, out, final_norm_gain, last, block0)
        xf = out
    return xf.reshape(batch, seq, d)
```

```python
import functools
import math

import jax
import jax.numpy as jnp
from jax import lax
from jax.experimental import pallas as pl
from jax.experimental.pallas import tpu as pltpu
from jax.experimental.pallas import tpu_sc as plsc

F32 = jnp.float32
BF16 = jnp.bfloat16

EPS = 1e-6
GN_EPS = 1e-5
ROPE_THETA = 10000.0
D_MODEL = 2048
HEAD_DIM = 64
ATT_Q_HEADS = 16
ATT_KV_HEADS = 2
ATT_WIDTH = ATT_Q_HEADS * HEAD_DIM
ATT_KV_WIDTH = ATT_KV_HEADS * HEAD_DIM
BLOCK = 128
RET_HEADS = 4
RET_HEAD_DIM = 256
RET_WIDTH = RET_HEADS * RET_HEAD_DIM
RET_CHUNK = 128
IN_COLS = ATT_WIDTH + 2 * ATT_KV_WIDTH + 4 * RET_WIDTH
XATT_HEADS = 4
XATT_HEAD_DIM = 128
XATT_WIDTH = XATT_HEADS * XATT_HEAD_DIM
PEER_HEADS = 8
PEER_N_KEYS = 128
PEER_HALF_DIM = 128
PEER_TOPK = 16
PEER_SLOTS = PEER_HEADS * PEER_TOPK

LANES = 128
SUBLANES = 8
VMEM_LIMIT = 56 * 1024 * 1024
IN_PROJ_TM, IN_PROJ_TN = 1024, 1792
ROW_TM = 256
PACK_ROWS = 256

_REF_QR0 = ATT_WIDTH + 2 * ATT_KV_WIDTH
_QR0 = 0
_KR0 = _QR0 + RET_WIDTH
_VR0 = _KR0 + RET_WIDTH
_GR0 = _VR0 + RET_WIDTH
_QA0 = _GR0 + RET_WIDTH
_KA0 = _QA0 + ATT_WIDTH
_VA0 = _KA0 + ATT_KV_WIDTH


def _reorder_w_in(w):
    return jnp.concatenate([w[:, _REF_QR0:], w[:, :_REF_QR0]], axis=1)


def _params(*sem):
    return pltpu.CompilerParams(dimension_semantics=sem, vmem_limit_bytes=VMEM_LIMIT)


def _rms(xf, gain):
    ms = jnp.mean(xf * xf, axis=-1, keepdims=True)
    return xf * lax.rsqrt(ms + EPS) * gain


def _dot(a, b):
    return jnp.dot(a, b, preferred_element_type=F32)


def _dot_nt(a, b):
    return lax.dot_general(a, b, (((1,), (1,)), ((), ())), preferred_element_type=F32)


def _norm_matmul_kernel(x_ref, g_ref, w_ref, o_ref, xn_ref):
    @pl.when(pl.program_id(1) == 0)
    def _():
        xn_ref[...] = _rms(x_ref[...], g_ref[...]).astype(BF16)

    o_ref[...] = _dot(xn_ref[...], w_ref[...]).astype(o_ref.dtype)


def _norm_matmul(x, gain, w_bf16, tm, tn, out_dtype, row0=0, rows=None):
    d = x.shape[1]
    t = x.shape[0] if rows is None else rows
    n = w_bf16.shape[1]
    blk0 = row0 // tm
    return pl.pallas_call(
        _norm_matmul_kernel,
        grid=(t // tm, n // tn),
        in_specs=[
            pl.BlockSpec((tm, d), lambda i, j: (i + blk0, 0)),
            pl.BlockSpec((1, d), lambda i, j: (0, 0)),
            pl.BlockSpec((d, tn), lambda i, j: (0, j)),
        ],
        out_specs=pl.BlockSpec((tm, tn), lambda i, j: (i, j)),
        out_shape=jax.ShapeDtypeStruct((t, n), out_dtype),
        scratch_shapes=[pltpu.VMEM((tm, d), BF16)],
        compiler_params=_params("parallel", "arbitrary"),
        name="norm_matmul",
    )(x, gain.reshape(1, d), w_bf16)


def _swa_kernel(sink_ref, q_ref, kc_ref, kp_ref, vc_ref, vp_ref, cc_ref, sc_ref, cp_ref, sp_ref,
                gain_ref, o_ref):
    n = pl.program_id(1)
    lane = lax.broadcasted_iota(jnp.int32, (1, LANES), 1)
    first_half = (lane % HEAD_DIM) < (HEAD_DIM // 2)
    lo = lane < HEAD_DIM

    def rope(x, c, s):
        partner = jnp.where(first_half, pltpu.roll(x, LANES - HEAD_DIM // 2, 1),
                            pltpu.roll(x, HEAD_DIM // 2, 1))
        return x * c + partner * s

    cc, sc = cc_ref[...], sc_ref[...]
    k = jnp.concatenate([rope(kp_ref[...].astype(F32), cp_ref[...], sp_ref[...]),
                         rope(kc_ref[...].astype(F32), cc, sc)], axis=0)
    v = jnp.concatenate([vp_ref[...].astype(F32), vc_ref[...].astype(F32)], axis=0)
    k_sw = pltpu.roll(k, HEAD_DIM, 1)
    v_sw = pltpu.roll(v, HEAD_DIM, 1)

    def place(a, a_sw, c, half):
        src = a if c == half else a_sw
        keep = lo if half == 0 else jnp.logical_not(lo)
        return jnp.where(keep, src, 0.0).astype(BF16)

    kvar = {(c, h): place(k, k_sw, c, h) for c in range(ATT_KV_HEADS) for h in range(2)}
    vvar = {(c, h): place(v, v_sw, c, h) for c in range(ATT_KV_HEADS) for h in range(2)}

    qi = lax.broadcasted_iota(jnp.int32, (BLOCK, 2 * BLOCK), 0) + BLOCK
    ki = lax.broadcasted_iota(jnp.int32, (BLOCK, 2 * BLOCK), 1)
    dist = qi - ki
    kmin = jnp.where(n > 0, 0, BLOCK)
    valid = (dist >= 0) & (dist < BLOCK) & (ki >= kmin)

    group = ATT_Q_HEADS // ATT_KV_HEADS
    for j in range(ATT_WIDTH // LANES):
        c = (2 * j) // group
        cols = slice(j * LANES, (j + 1) * LANES)
        qg = (rope(q_ref[:, cols].astype(F32), cc, sc) * (HEAD_DIM ** -0.5)).astype(BF16)
        out = jnp.zeros((BLOCK, LANES), F32)
        for half in range(2):
            s = jnp.where(valid, _dot_nt(qg, kvar[(c, half)]), -jnp.inf)
            sink = sink_ref[2 * j + half]
            m = jnp.maximum(jnp.max(s, axis=-1, keepdims=True), sink)
            p = jnp.exp(s - m)
            denom = jnp.sum(p, axis=-1, keepdims=True) + jnp.exp(sink - m)
            out = out + _dot((p / denom).astype(BF16), vvar[(c, half)])
        sq = out * out
        ss_lo = jnp.sum(jnp.where(lo, sq, 0.0), axis=-1, keepdims=True)
        ss_hi = jnp.sum(jnp.where(lo, 0.0, sq), axis=-1, keepdims=True)
        ms = jnp.where(lo, ss_lo, ss_hi) * (1.0 / HEAD_DIM)
        o_ref[:, cols] = (out * lax.rsqrt(ms + EPS) * gain_ref[:, cols]).astype(o_ref.dtype)


def _swa(h, cos_a, sin_a, sinks, gain, batch, seq):
    nb = seq // BLOCK
    t = batch * seq
    kcol, vcol = _KA0 // LANES, _VA0 // LANES
    cur = lambda b, n: b * nb + n
    prev = lambda b, n: b * nb + jnp.maximum(n - 1, 0)
    row = lambda col, f: pl.BlockSpec((BLOCK, LANES), lambda b, n: (f(b, n), col))
    return pl.pallas_call(
        _swa_kernel,
        grid=(batch, nb),
        in_specs=[
            pl.BlockSpec(memory_space=pltpu.SMEM),
            pl.BlockSpec((BLOCK, ATT_WIDTH), lambda b, n: (cur(b, n), _QA0 // ATT_WIDTH)),
            row(kcol, cur), row(kcol, prev), row(vcol, cur), row(vcol, prev),
            row(0, cur), row(0, cur), row(0, prev), row(0, prev),
            pl.BlockSpec((1, ATT_WIDTH), lambda b, n: (0, 0)),
        ],
        out_specs=pl.BlockSpec((BLOCK, ATT_WIDTH), lambda b, n: (cur(b, n), 0)),
        out_shape=jax.ShapeDtypeStruct((t, ATT_WIDTH), BF16),
        compiler_params=_params("parallel", "parallel"),
        name="swa_attention",
    )(sinks, h, h, h, h, h, cos_a, sin_a, cos_a, sin_a, gain.reshape(1, ATT_WIDTH))


def _ret_kernel(lg_ref, cd_ref, q_ref, k_ref, v_ref, g_ref, c_ref, s_ref, gain_ref, o_ref, state_ref):
    n = pl.program_id(1)

    @pl.when(n == 0)
    def _():
        state_ref[...] = jnp.zeros_like(state_ref)

    c, s = c_ref[...], s_ref[...]
    hd = RET_HEAD_DIM
    half = hd // 2

    def rope(x):
        x1, x2 = x[:, :half], x[:, half:]
        return jnp.concatenate([x1 * c - x2 * s, x2 * c + x1 * s], axis=1)

    ri = lax.broadcasted_iota(jnp.int32, (RET_CHUNK, RET_CHUNK), 0).astype(F32)
    ci = lax.broadcasted_iota(jnp.int32, (RET_CHUNK, RET_CHUNK), 1).astype(F32)
    diff = ri - ci
    for hh in range(RET_HEADS):
        cols = slice(hh * hd, (hh + 1) * hd)
        lg = lg_ref[hh]
        q = rope(q_ref[:, cols].astype(F32))
        k = rope(k_ref[:, cols].astype(F32)) * (hd ** -0.5)
        v = v_ref[:, cols].astype(BF16)

        decay = jnp.where(diff >= 0, jnp.exp(jnp.maximum(diff, 0.0) * lg), 0.0)
        zeta = jnp.exp((RET_CHUNK - 1 - ri) * lg)
        xi = jnp.exp((ri + 1.0) * lg)
        zeta2 = jnp.concatenate([zeta, zeta], axis=1)
        xi2 = jnp.concatenate([xi, xi], axis=1)

        inner = _dot_nt(q.astype(BF16), k.astype(BF16)) * decay
        state = state_ref[hh]
        out = _dot(inner.astype(BF16), v) + _dot((q * xi2).astype(BF16), state.astype(BF16))
        kv = lax.dot_general((k * zeta2).astype(BF16), v, (((0,), (0,)), ((), ())), preferred_element_type=F32)
        state_ref[hh] = state * cd_ref[hh] + kv

        mu = jnp.mean(out, axis=-1, keepdims=True)
        cen = out - mu
        var = jnp.mean(cen * cen, axis=-1, keepdims=True)
        g = g_ref[:, cols].astype(F32)
        o = cen * lax.rsqrt(var + GN_EPS) * gain_ref[:, cols] * (g * jax.nn.sigmoid(g))
        o_ref[:, cols] = o.astype(o_ref.dtype)


def _retention(h, cos_r, sin_r, gain, batch, seq):
    nc = seq // RET_CHUNK
    t = batch * seq
    hd = RET_HEAD_DIM
    log_gamma = jnp.log1p(-jnp.exp2(-5.0 - jnp.arange(RET_HEADS, dtype=F32)))
    chunk_decay = jnp.exp(RET_CHUNK * log_gamma)
    col = lambda c0: pl.BlockSpec((RET_CHUNK, RET_WIDTH), lambda b, n: (b * nc + n, c0 // RET_WIDTH))
    tab = pl.BlockSpec((RET_CHUNK, hd // 2), lambda b, n: (b * nc + n, 0))
    return pl.pallas_call(
        _ret_kernel,
        grid=(batch, nc),
        in_specs=[
            pl.BlockSpec(memory_space=pltpu.SMEM),
            pl.BlockSpec(memory_space=pltpu.SMEM),
            col(_QR0), col(_KR0), col(_VR0), col(_GR0), tab, tab,
            pl.BlockSpec((1, RET_WIDTH), lambda b, n: (0, 0)),
        ],
        out_specs=pl.BlockSpec((RET_CHUNK, RET_WIDTH), lambda b, n: (b * nc + n, 0)),
        out_shape=jax.ShapeDtypeStruct((t, RET_WIDTH), BF16),
        scratch_shapes=[pltpu.VMEM((RET_HEADS, hd, hd), F32)],
        compiler_params=_params("parallel", "arbitrary"),
        name="retention",
    )(log_gamma, chunk_decay, h, h, h, h, cos_r, sin_r, gain.reshape(1, RET_WIDTH))


def _outproj_kernel(x_ref, oa_ref, or_ref, wa_ref, wr_ref, g_ref, wq_ref, x1_ref, qx_ref):
    x1 = x_ref[...] + _dot(oa_ref[...], wa_ref[...]) + _dot(or_ref[...], wr_ref[...])
    x1_ref[...] = x1
    qx_ref[...] = _dot(_rms(x1, g_ref[...]).astype(BF16), wq_ref[...]).astype(qx_ref.dtype)


def _outproj(x, oa, orr, w_out_bf16, gain, w_xq_bf16, tm, row0=0):
    t, d = oa.shape[0], x.shape[1]
    blk0 = row0 // tm
    full = lambda shape: pl.BlockSpec(shape, lambda i: (0, 0))
    return pl.pallas_call(
        _outproj_kernel,
        grid=(t // tm,),
        in_specs=[
            pl.BlockSpec((tm, d), lambda i: (i + blk0, 0)),
            pl.BlockSpec((tm, ATT_WIDTH), lambda i: (i, 0)),
            pl.BlockSpec((tm, RET_WIDTH), lambda i: (i, 0)),
            pl.BlockSpec((ATT_WIDTH, d), lambda i: (0, 0)),
            pl.BlockSpec((RET_WIDTH, d), lambda i: (1, 0)),
            full((1, d)),
            full((d, XATT_WIDTH)),
        ],
        out_specs=[pl.BlockSpec((tm, d), lambda i: (i, 0)), pl.BlockSpec((tm, XATT_WIDTH), lambda i: (i, 0))],
        out_shape=[jax.ShapeDtypeStruct((t, d), F32), jax.ShapeDtypeStruct((t, XATT_WIDTH), BF16)],
        compiler_params=_params("parallel"),
        name="out_proj",
    )(x, oa, orr, w_out_bf16, w_out_bf16, gain.reshape(1, d), w_xq_bf16)


def _xattn_kernel(qx_ref, k_ref, v_ref, x1_ref, wo_ref, g_ref, wpq_ref, x2_ref, pq_ref):
    heads = []
    for hh in range(XATT_HEADS):
        cols = slice(hh * XATT_HEAD_DIM, (hh + 1) * XATT_HEAD_DIM)
        s = _dot_nt(qx_ref[:, cols], k_ref[:, cols]) * (XATT_HEAD_DIM ** -0.5)
        p = jnp.exp(s - jnp.max(s, axis=-1, keepdims=True))
        p = p / jnp.sum(p, axis=-1, keepdims=True)
        heads.append(_dot(p.astype(BF16), v_ref[:, cols]))
    o = jnp.concatenate(heads, axis=1).astype(BF16)
    x2 = x1_ref[...] + _dot(o, wo_ref[...])
    x2_ref[...] = x2
    pq_ref[...] = _dot(_rms(x2, g_ref[...]).astype(BF16), wpq_ref[...])


def _xattn(qx, kv_mem, x1, w_xo_bf16, gain, w_pq_bf16, batch, seq, mem_len, tm, batch0=0):
    t, d = x1.shape
    npq = w_pq_bf16.shape[1]
    nt = seq // tm
    rows = lambda width: pl.BlockSpec((tm, width), lambda b, i: (b * nt + i, 0))
    full = lambda shape: pl.BlockSpec(shape, lambda b, i: (0, 0))
    return pl.pallas_call(
        _xattn_kernel,
        grid=(batch, nt),
        in_specs=[
            rows(XATT_WIDTH),
            pl.BlockSpec((mem_len, XATT_WIDTH), lambda b, i: (b + batch0, 0)),
            pl.BlockSpec((mem_len, XATT_WIDTH), lambda b, i: (b + batch0, 1)),
            rows(d),
            full((XATT_WIDTH, d)),
            full((1, d)),
            full((d, npq)),
        ],
        out_specs=[rows(d), rows(npq)],
        out_shape=[jax.ShapeDtypeStruct((t, d), F32), jax.ShapeDtypeStruct((t, npq), F32)],
        compiler_params=_params("parallel", "parallel"),
        name="cross_attention",
    )(qx, kv_mem, kv_mem, x1, w_xo_bf16, gain.reshape(1, d), w_pq_bf16)


def _topk_rows(s, k, payload=None):
    nrows = s.shape[0]
    rows = lax.broadcasted_iota(jnp.int32, s.shape, 0).astype(F32)
    vals, sel = [], []
    for _ in range(k):
        m = jnp.max(s, axis=0, keepdims=True)
        am = jnp.min(jnp.where(s == m, rows, float(nrows)), axis=0, keepdims=True)
        hit = rows == am
        vals.append(m)
        sel.append(am if payload is None else jnp.max(jnp.where(hit, payload, -1.0), axis=0, keepdims=True))
        s = jnp.where(hit, -jnp.inf, s)
    return jnp.concatenate(vals, axis=0), jnp.concatenate(sel, axis=0)


def _staircase(t1, t2):
    kk, half = PEER_TOPK, SUBLANES
    blocks = [(t1[0:1, :], t2, kk), (t1[1:2, :], t2[0:half, :], half)]
    blocks += [(t1[a:a + 1, :], t2[0:half, :], kk // (a + 1)) for a in range(2, half)]
    blocks.append((t1[half:kk, :], t2[0:1, :], half))
    return blocks


def _peer_topk_kernel(q_ref, keys_ref, ids_ref, gates_ref, *, heads):
    width = 2 * PEER_HALF_DIM
    row8 = lax.broadcasted_iota(jnp.int32, (SUBLANES, q_ref.shape[0]), 0)
    for hh in range(heads):
        tops = []
        for p in range(2):
            c0 = hh * width + p * PEER_HALF_DIM
            qp = q_ref[:, c0:c0 + PEER_HALF_DIM].astype(BF16)
            tops.append(_topk_rows(_dot_nt(keys_ref[hh, p], qp), PEER_TOPK))
        (s1, i1), (s2, i2) = tops
        cand_s, cand_i = [], []
        for (a_s, b_s, live), (a_i, b_i, _) in zip(_staircase(s1, s2), _staircase(i1 * float(PEER_N_KEYS), i2)):
            blk = a_s + b_s
            if live < blk.shape[0]:
                blk = jnp.where(row8 < live, blk, -jnp.inf)
            cand_s.append(blk)
            cand_i.append(a_i + b_i)
        top_s, top_e = _topk_rows(jnp.concatenate(cand_s, axis=0), PEER_TOPK,
                                  payload=jnp.concatenate(cand_i, axis=0))
        e = jnp.exp(top_s - top_s[0:1, :])
        ids_ref[hh] = top_e.astype(jnp.int32)
        gates_ref[hh] = e / jnp.sum(e, axis=0, keepdims=True)


def _peer_topk(pq, keys_bf16, tt, heads):
    t = pq.shape[0]
    width = 2 * PEER_HALF_DIM
    out = pl.BlockSpec((heads, PEER_TOPK, tt), lambda i, hh: (hh, 0, i))
    return pl.pallas_call(
        functools.partial(_peer_topk_kernel, heads=heads),
        grid=(t // tt, PEER_HEADS // heads),
        in_specs=[
            pl.BlockSpec((tt, heads * width), lambda i, hh: (i, hh)),
            pl.BlockSpec((heads, 2, PEER_N_KEYS, PEER_HALF_DIM), lambda i, hh: (hh, 0, 0, 0)),
        ],
        out_specs=[out, out],
        out_shape=[jax.ShapeDtypeStruct((PEER_HEADS, PEER_TOPK, t), jnp.int32),
                   jax.ShapeDtypeStruct((PEER_HEADS, PEER_TOPK, t), F32)],
        compiler_params=_params("parallel", "parallel"),
        name="peer_topk",
    )(pq, keys_bf16)


GATHER_TOKENS = 128
GATHER_BUFS = 8
GATHER_GROUP = 8
assert GATHER_GROUP % GATHER_BUFS == 0 and GATHER_GROUP % SUBLANES == 0 and GATHER_TOKENS % GATHER_GROUP == 0


def _peer_gather_kernel(ids_hbm, tbl_hbm, x2_ref, gates_ref, fg_ref, og_ref, *rest, final_norm, first_block,
                        has_prev):
    out_ref, ids_smem, ids_sem, *scratch = rest[1:] if has_prev else rest
    gbufs = scratch[:GATHER_BUFS]
    gsem, xn_ref, y_ref, yacc_ref = scratch[GATHER_BUFS:]
    i = pl.program_id(0)
    has_next = i + 1 < pl.num_programs(0)
    slot = i % 2
    other = 1 - slot
    tb = GATHER_TOKENS
    nchunk = D_MODEL // LANES
    nids = tb * PEER_SLOTS
    ahead = GATHER_BUFS - 1
    group = GATHER_GROUP

    def ids_copy(step, sl):
        return pltpu.make_async_copy(ids_hbm.at[pl.ds(pl.multiple_of((first_block + step) * nids, nids), nids)],
                                     ids_smem.at[pl.ds(pl.multiple_of(sl * nids, nids), nids)], ids_sem.at[sl])

    def row_copy(e, b, k):
        return pltpu.make_async_copy(tbl_hbm.at[e], gbufs[b].at[pl.ds(k, 1)], gsem.at[b])

    def issue(sl, tok, b):
        base = sl * nids + tok * PEER_SLOTS
        for k in range(PEER_SLOTS):
            row_copy(ids_smem[base + k], b, k).start(priority=k % 2)

    def wait_rows(b):
        for k in range(PEER_SLOTS):
            row_copy(0, b, k).wait()

    @pl.when(i == 0)
    def _():
        ids_copy(0, 0).start()
        ids_copy(0, 0).wait()
        for tok in range(ahead):
            issue(0, tok, tok)

    @pl.when(has_next)
    def _():
        ids_copy(i + 1, other).start()

    xn_ref[...] = _rms(x2_ref[...], fg_ref[...])

    lane = lax.broadcasted_iota(jnp.int32, (PEER_SLOTS, LANES), 1)
    hi_mask = jnp.uint32(0xFFFF0000)

    def compute(tok, r, b):
        gbuf = gbufs[b]
        xrow = xn_ref[pl.ds(tok, 1), :]
        acc = jnp.zeros((PEER_SLOTS, LANES), F32)
        for c in range(nchunk):
            w = gbuf[:, c * LANES:(c + 1) * LANES]
            u = lax.bitcast_convert_type(w << 16, F32)
            acc = acc + u * xrow[:, c * LANES:(c + 1) * LANES]
        a = jnp.sum(acc, axis=-1, keepdims=True)
        gate = jnp.sum(jnp.where(lane == tok, gates_ref[...], 0.0), axis=-1, keepdims=True)
        coeff = 0.5 * a * (1.0 + lax.erf(a * (2.0 ** -0.5))) * gate
        for c in range(nchunk):
            w = gbuf[:, c * LANES:(c + 1) * LANES]
            vv = lax.bitcast_convert_type(w & hi_mask, F32)
            yacc_ref[r:r + 1, c * LANES:(c + 1) * LANES] = jnp.sum(vv * coeff, axis=0, keepdims=True)

    def run_group(g, last):
        for r in range(group):
            b = r % GATHER_BUFS
            nb = (r + ahead) % GATHER_BUFS
            tok = g * group + r
            wait_rows(b)
            if not last or r + ahead < group:
                issue(slot, tok + ahead, nb)
            else:
                if r + ahead == group:
                    @pl.when(has_next)
                    def _():
                        ids_copy(i + 1, other).wait()

                @pl.when(has_next)
                def _():
                    issue(other, r + ahead - group, nb)
            compute(tok, r, b)
        y_ref[pl.ds(pl.multiple_of(g * group, group), group), :] = yacc_ref[...]

    def body(g, carry):
        run_group(g, last=False)
        return carry

    ngroups = tb // group
    lax.fori_loop(0, ngroups - 1, body, 0)
    run_group(ngroups - 1, last=True)
    x3 = x2_ref[...] + y_ref[...]
    out_ref[...] = _rms(x3, og_ref[...]) if final_norm else x3


def _peer_gather(ids_flat, table, x2, gates_t, ffn_gain, final_gain, final_norm, first_block, out_rows,
                 out_block0, prev):
    t, d = x2.shape
    tb = GATHER_TOKENS
    nsteps = t // tb - first_block
    full = lambda shape: pl.BlockSpec(shape, lambda i: (0, 0))
    has_prev = prev is not None
    return pl.pallas_call(
        functools.partial(_peer_gather_kernel, final_norm=final_norm, first_block=first_block, has_prev=has_prev),
        grid=(nsteps,),
        in_specs=[
            pl.BlockSpec(memory_space=pl.ANY),
            pl.BlockSpec(memory_space=pl.ANY),
            pl.BlockSpec((tb, d), lambda i: (i + first_block, 0)),
            pl.BlockSpec((PEER_SLOTS, tb), lambda i: (0, i + first_block)),
            full((1, d)),
            full((1, d)),
        ] + ([pl.BlockSpec(memory_space=pl.ANY)] if has_prev else []),
        out_specs=pl.BlockSpec((tb, d), lambda i: (i + first_block + out_block0, 0)),
        out_shape=jax.ShapeDtypeStruct((out_rows, d), F32),
        input_output_aliases={6: 0} if has_prev else {},
        scratch_shapes=[
            pltpu.SMEM((2 * tb * PEER_SLOTS,), jnp.int32),
            pltpu.SemaphoreType.DMA((2,)),
            *[pltpu.VMEM((PEER_SLOTS, d), jnp.uint32) for _ in range(GATHER_BUFS)],
            pltpu.SemaphoreType.DMA((GATHER_BUFS,)),
            pltpu.VMEM((tb, d), F32),
            pltpu.VMEM((tb, d), F32),
            pltpu.VMEM((GATHER_GROUP, d), F32),
        ],
        compiler_params=_params("arbitrary"),
        name="peer_gather",
    )(ids_flat, table, x2, gates_t, ffn_gain.reshape(1, d), final_gain.reshape(1, d), *([prev] if has_prev else []))


SC_CORES, SC_SUBCORES, SC_LANES = 2, 16, 16
SC_WORKERS = SC_CORES * SC_SUBCORES
SC_ROWS = SC_LANES
SC_COLS = 8
SC_DOT_ROWS = 8
SC_GROUPS, SC_GROUP_DENOM = (1, 1, 2, 4), 8
SC_SHARE = (62, 128)


def _sc_erf(x):
    ax = jnp.abs(x)
    t = 1.0 / (1.0 + 0.3275911 * ax)
    poly = t * (0.254829592 + t * (-0.284496736 + t * (1.421413741 + t * (-1.453152027 + t * 1.061405429))))
    y = 1.0 - poly * jnp.exp(-ax * ax)
    return jnp.where(x < 0, -y, y)


def _peer_sc(ids_flat, gates_tok, xn, table, ts):
    d = D_MODEL
    lanes, rows = SC_LANES, SC_ROWS
    nchunk = PEER_SLOTS // rows
    block = lanes * SC_COLS
    per_worker = ts // SC_WORKERS
    assert ts % (2 * SC_WORKERS) == 0 and d % block == 0 and nchunk % 2 == 0
    mesh = plsc.VectorSubcoreMesh(core_axis_name="c", subcore_axis_name="s")

    @functools.partial(
        pl.kernel, mesh=mesh,
        out_type=jax.ShapeDtypeStruct((ts, d), F32),
        scratch_types=[
            pltpu.VMEM((2, PEER_SLOTS), jnp.int32),
            pltpu.VMEM((2, PEER_SLOTS), F32),
            pltpu.VMEM((2, d), F32),
            pltpu.VMEM((d,), F32),
            pltpu.VMEM((2, rows, 1, d), jnp.uint32),
            pltpu.VMEM((rows, lanes), F32),
            pltpu.VMEM((lanes,), F32),
            pltpu.VMEM((rows, lanes), jnp.int32),
            pltpu.SemaphoreType.DMA((2,)),
            pltpu.SemaphoreType.DMA((2,)),
        ],
        compiler_params=pltpu.CompilerParams(needs_layout_passes=False),
        name="peer_sparsecore",
    )
    def run(ids_hbm, gates_hbm, xn_hbm, tbl_hbm, y_hbm, ids_v, gates_v, x_v, y_v, rows_v, acc_v, coeff_v, ridx_v,
            row_sem, in_sem):
        worker = lax.axis_index("s") * SC_CORES + lax.axis_index("c")
        first = worker * per_worker
        lane = lax.iota(jnp.int32, lanes)
        zero = jnp.zeros((lanes,), F32)
        hi_mask = jnp.full((lanes,), 0xFFFF0000, jnp.uint32)
        for r in range(rows):
            ridx_v[r, :] = jnp.full((lanes,), r, jnp.int32)

        def gather(par, c, slot):
            idx = ids_v[par, pl.ds(pl.multiple_of(c * rows, rows), rows)]
            return pltpu.make_async_copy(tbl_hbm.at[idx], rows_v.at[slot], row_sem.at[slot])

        def token_inputs(tok, par):
            return (pltpu.make_async_copy(ids_hbm.at[pl.ds(tok * PEER_SLOTS, PEER_SLOTS)], ids_v.at[par], in_sem.at[par]),
                    pltpu.make_async_copy(gates_hbm.at[pl.ds(tok * PEER_SLOTS, PEER_SLOTS)], gates_v.at[par],
                                          in_sem.at[par]),
                    pltpu.make_async_copy(xn_hbm.at[tok], x_v.at[par], in_sem.at[par]))

        def chunk(par, c, slot):
            for r0 in range(0, rows, SC_DOT_ROWS):
                def dot_body(j, accs, r0=r0):
                    xj = x_v[par, pl.ds(j * lanes, lanes)]
                    return tuple(
                        accs[q] + plsc.bitcast(rows_v[slot, r0 + q, 0, pl.ds(j * lanes, lanes)] << 16, F32) * xj
                        for q in range(SC_DOT_ROWS))
                accs = lax.fori_loop(0, d // lanes, dot_body, (zero,) * SC_DOT_ROWS, unroll=4)
                for q in range(SC_DOT_ROWS):
                    acc_v[r0 + q, :] = accs[q]
            a = zero
            for col in range(lanes):
                a = a + plsc.load_gather(acc_v, [lane, ridx_v[col, :]])
            gate = gates_v[par, pl.ds(pl.multiple_of(c * rows, rows), rows)]
            coeff_v[...] = 0.5 * a * (1.0 + _sc_erf(a * (2.0 ** -0.5))) * gate
            coeffs = [plsc.load_gather(coeff_v, [ridx_v[r, :]]) for r in range(rows)]

            @pl.loop(0, d // block)
            def _(jb):
                base = jb * block
                accs = [zero] * SC_COLS
                for r in range(rows):
                    for q in range(SC_COLS):
                        w = rows_v[slot, r, 0, pl.ds(base + q * lanes, lanes)]
                        accs[q] = accs[q] + plsc.bitcast(w & hi_mask, F32) * coeffs[r]
                for q in range(SC_COLS):
                    plsc.addupdate(y_v.at[pl.ds(base + q * lanes, lanes)], accs[q])

        for cp in token_inputs(first, 0):
            cp.start()

        @pl.loop(0, per_worker // 2)
        def _(pair):
            for par in range(2):
                i = pair * 2 + par
                tok = first + i
                for cp in token_inputs(tok, par):
                    cp.wait()

                @pl.when(i + 1 < per_worker)
                def _():
                    for cp in token_inputs(tok + 1, 1 - par):
                        cp.start()

                @pl.loop(0, d // lanes)
                def _(j):
                    y_v[pl.ds(j * lanes, lanes)] = zero

                gather(par, 0, 0).start()

                @pl.loop(0, nchunk // 2)
                def _(cpair):
                    for slot in range(2):
                        c = cpair * 2 + slot
                        if slot == 0:
                            gather(par, c + 1, 1).start()
                        else:
                            @pl.when(c + 1 < nchunk)
                            def _():
                                gather(par, c + 1, 0).start()
                        gather(par, c, slot).wait()
                        chunk(par, c, slot)

                pltpu.sync_copy(y_v, y_hbm.at[tok])

    return run(ids_flat, gates_tok, xn, table)


def _rms_rows_kernel(x_ref, g_ref, o_ref):
    o_ref[...] = _rms(x_ref[...], g_ref[...])


def _rms_rows(x, gain, rows, tm):
    d = x.shape[1]
    return pl.pallas_call(
        _rms_rows_kernel,
        grid=(rows // tm,),
        in_specs=[pl.BlockSpec((tm, d), lambda i: (i, 0)), pl.BlockSpec((1, d), lambda i: (0, 0))],
        out_specs=pl.BlockSpec((tm, d), lambda i: (i, 0)),
        out_shape=jax.ShapeDtypeStruct((rows, d), F32),
        compiler_params=_params("parallel"),
        name="rms_rows",
    )(x, gain.reshape(1, d))


def _peer_finish_kernel(x2_ref, y_ref, og_ref, prev_ref, out_ref, *, final_norm):
    del prev_ref
    x3 = x2_ref[...] + y_ref[...]
    out_ref[...] = _rms(x3, og_ref[...]) if final_norm else x3


def _peer_finish(x2, y_head, out_rest, final_gain, final_norm, out_block0):
    rows, d = y_head.shape
    tm = GATHER_TOKENS
    blk = pl.BlockSpec((tm, d), lambda i: (i, 0))
    return pl.pallas_call(
        functools.partial(_peer_finish_kernel, final_norm=final_norm),
        grid=(rows // tm,),
        in_specs=[blk, blk, pl.BlockSpec((1, d), lambda i: (0, 0)), pl.BlockSpec(memory_space=pl.ANY)],
        out_specs=pl.BlockSpec((tm, d), lambda i: (i + out_block0, 0)),
        out_shape=jax.ShapeDtypeStruct(out_rest.shape, F32),
        input_output_aliases={3: 0},
        compiler_params=_params("parallel"),
        name="peer_finish",
    )(x2, y_head, final_gain.reshape(1, d), out_rest)


def _rope_angles(positions, dh):
    inv_freq = ROPE_THETA ** (-jnp.arange(0, dh, 2, dtype=F32) / dh)
    ang = positions.astype(F32).reshape(-1, 1) * inv_freq
    return jnp.cos(ang), jnp.sin(ang)


def _rope_tables_att(positions):
    cos, sin = _rope_angles(positions, HEAD_DIM)
    copies = LANES // HEAD_DIM
    return (jnp.tile(jnp.concatenate([cos, cos], axis=1), (1, copies)),
            jnp.tile(jnp.concatenate([-sin, sin], axis=1), (1, copies)))


def _pack_table_kernel(u_ref, v_ref, o_ref):
    ub = lax.bitcast_convert_type(u_ref[...].astype(BF16).astype(F32), jnp.uint32)
    vb = lax.bitcast_convert_type(v_ref[...].astype(BF16).astype(F32), jnp.uint32)
    o_ref[:, 0, :] = (ub >> 16) | vb


def _pack_expert_table(u, v, rows=PACK_ROWS):
    n, d = u.shape
    blk = pl.BlockSpec((rows, d), lambda i: (i, 0))
    return pl.pallas_call(
        _pack_table_kernel,
        grid=(n // rows,),
        in_specs=[blk, blk],
        out_specs=pl.BlockSpec((rows, 1, d), lambda i: (i, 0, 0)),
        out_shape=jax.ShapeDtypeStruct((n, 1, d), jnp.uint32),
        compiler_params=_params("parallel"),
        name="pack_table",
    )(u, v)


def kernel(x, mem, positions, mix_norm_gain, w_in, att_sinks, att_out_gain, ret_out_gain, w_out,
           cross_norm_gain, mem_norm_gain, w_xq, w_xk, w_xv, w_xo, ffn_norm_gain,
           w_peer_q, peer_sub_keys, peer_u, peer_v, final_norm_gain):
    batch, seq, d = x.shape
    mem_len = mem.shape[1]
    t = batch * seq
    depth = w_in.shape[0]
    assert d == D_MODEL and seq % BLOCK == 0 and t % GATHER_TOKENS == 0

    cos_a, sin_a = _rope_tables_att(positions)
    cos_r, sin_r = _rope_angles(positions, RET_HEAD_DIM)
    xf = x.reshape(t, d)
    memf = mem.reshape(batch * mem_len, d)
    nblocks = t // GATHER_TOKENS
    split = batch % SC_GROUP_DENOM == 0 and nblocks % SC_SHARE[1] == 0
    groups = [batch * g // SC_GROUP_DENOM for g in SC_GROUPS] if split else [batch]
    sc_total = nblocks * SC_SHARE[0] // SC_SHARE[1] if split else 0
    tm_mid = min(ROW_TM, seq)
    for l in range(depth):
        sc_left = sc_total
        w_in_b = _reorder_w_in(w_in[l]).astype(BF16)
        w_out_b, w_xq_b, w_xo_b, w_pq_b = (w.astype(BF16) for w in (w_out[l], w_xq[l], w_xo[l], w_peer_q[l]))
        keys_b = peer_sub_keys[l].astype(BF16)
        w_kv = jnp.concatenate([w_xk[l], w_xv[l]], axis=1).astype(BF16)
        kv_mem = _norm_matmul(memf, mem_norm_gain[l], w_kv, mem_len, 2 * XATT_WIDTH, BF16)
        table = _pack_expert_table(peer_u[l], peer_v[l])
        last = l == depth - 1
        out = None
        pending = []
        b0 = 0
        for bp in groups:
            r0, tp = b0 * seq, bp * seq
            block0, blocks_p = r0 // GATHER_TOKENS, tp // GATHER_TOKENS
            rows = slice(r0, r0 + tp)
            h = _norm_matmul(xf, mix_norm_gain[l], w_in_b, min(IN_PROJ_TM, tp), IN_PROJ_TN, BF16, row0=r0, rows=tp)
            oa = _swa(h, cos_a[rows], sin_a[rows], att_sinks[l], att_out_gain[l], bp, seq)
            orr = _retention(h, cos_r[rows], sin_r[rows], ret_out_gain[l], bp, seq)
            x1, qx = _outproj(xf, oa, orr, w_out_b, cross_norm_gain[l], w_xq_b, tm_mid, row0=r0)
            x2, pq = _xattn(qx, kv_mem, x1, w_xo_b, ffn_norm_gain[l], w_pq_b, bp, seq, mem_len, tm_mid, batch0=b0)
            ids_t, gates_t = _peer_topk(pq, keys_b, LANES, PEER_HEADS)
            ids = ids_t.reshape(PEER_SLOTS, tp).T.reshape(tp * PEER_SLOTS)
            gates_t = gates_t.reshape(PEER_SLOTS, tp)
            sc_blocks = min(sc_left, blocks_p)
            sc_left -= sc_blocks
            if sc_blocks:
                ts = sc_blocks * GATHER_TOKENS
                xn_head = _rms_rows(x2, ffn_norm_gain[l], ts, GATHER_TOKENS)
                y_head = _peer_sc(ids, gates_t[:, :ts].T.reshape(ts * PEER_SLOTS), xn_head, table, ts)
                pending.append((x2, y_head, block0))
            if sc_blocks < blocks_p:
                out = _peer_gather(ids, table, x2, gates_t, ffn_norm_gain[l], final_norm_gain, last, sc_blocks,
                                   t, block0, out)
            b0 += bp
        for x2, y_head, block0 in pending:
            out = _peer_finish(x2, y_head, out, final_norm_gain, last, block0)
        xf = out
    return xf.reshape(batch, seq, d)
```

```python
import functools

import jax
import jax.numpy as jnp
from jax import lax
from jax.experimental import pallas as pl
from jax.experimental.pallas import tpu as pltpu
from jax.experimental.pallas import tpu_sc as plsc

F32 = jnp.float32
BF16 = jnp.bfloat16

EPS = 1e-6
GN_EPS = 1e-5
ROPE_THETA = 10000.0
D_MODEL = 2048
HEAD_DIM = 64
ATT_Q_HEADS = 16
ATT_KV_HEADS = 2
ATT_WIDTH = ATT_Q_HEADS * HEAD_DIM
ATT_KV_WIDTH = ATT_KV_HEADS * HEAD_DIM
BLOCK = 128
RET_HEADS = 4
RET_HEAD_DIM = 256
RET_WIDTH = RET_HEADS * RET_HEAD_DIM
RET_CHUNK = 128
IN_COLS = ATT_WIDTH + 2 * ATT_KV_WIDTH + 4 * RET_WIDTH
XATT_HEADS = 4
XATT_HEAD_DIM = 128
XATT_WIDTH = XATT_HEADS * XATT_HEAD_DIM
PEER_HEADS = 8
PEER_N_KEYS = 128
PEER_HALF_DIM = 128
PEER_TOPK = 16
PEER_SLOTS = PEER_HEADS * PEER_TOPK

LANES = 128
SUBLANES = 8
VMEM_LIMIT = 56 * 1024 * 1024
IN_PROJ_TM, IN_PROJ_TN = 1024, 1792
ROW_TM = 256
PACK_ROWS = 256

_REF_QR0 = ATT_WIDTH + 2 * ATT_KV_WIDTH
_QR0 = 0
_KR0 = _QR0 + RET_WIDTH
_VR0 = _KR0 + RET_WIDTH
_GR0 = _VR0 + RET_WIDTH
_QA0 = _GR0 + RET_WIDTH
_KA0 = _QA0 + ATT_WIDTH
_VA0 = _KA0 + ATT_KV_WIDTH


def _reorder_w_in(w):
    return jnp.concatenate([w[:, _REF_QR0:], w[:, :_REF_QR0]], axis=1)


def _params(*sem):
    return pltpu.CompilerParams(dimension_semantics=sem, vmem_limit_bytes=VMEM_LIMIT)


def _rms(xf, gain):
    ms = jnp.mean(xf * xf, axis=-1, keepdims=True)
    return xf * lax.rsqrt(ms + EPS) * gain


def _dot(a, b):
    return jnp.dot(a, b, preferred_element_type=F32)


def _dot_nt(a, b):
    return lax.dot_general(a, b, (((1,), (1,)), ((), ())), preferred_element_type=F32)


def _norm_matmul_kernel(x_ref, g_ref, w_ref, o_ref, xn_ref):
    @pl.when(pl.program_id(1) == 0)
    def _():
        xn_ref[...] = _rms(x_ref[...], g_ref[...]).astype(BF16)

    o_ref[...] = _dot(xn_ref[...], w_ref[...]).astype(o_ref.dtype)


def _norm_matmul(x, gain, w_bf16, tm, tn, out_dtype, row0=0, rows=None):
    d = x.shape[1]
    t = x.shape[0] if rows is None else rows
    n = w_bf16.shape[1]
    blk0 = row0 // tm
    return pl.pallas_call(
        _norm_matmul_kernel,
        grid=(t // tm, n // tn),
        in_specs=[
            pl.BlockSpec((tm, d), lambda i, j: (i + blk0, 0)),
            pl.BlockSpec((1, d), lambda i, j: (0, 0)),
            pl.BlockSpec((d, tn), lambda i, j: (0, j)),
        ],
        out_specs=pl.BlockSpec((tm, tn), lambda i, j: (i, j)),
        out_shape=jax.ShapeDtypeStruct((t, n), out_dtype),
        scratch_shapes=[pltpu.VMEM((tm, d), BF16)],
        compiler_params=_params("parallel", "arbitrary"),
        name="norm_matmul",
    )(x, gain.reshape(1, d), w_bf16)


def _swa_kernel(sink_ref, q_ref, kc_ref, kp_ref, vc_ref, vp_ref, cc_ref, sc_ref, cp_ref, sp_ref,
                gain_ref, o_ref):
    n = pl.program_id(1)
    lane = lax.broadcasted_iota(jnp.int32, (1, LANES), 1)
    first_half = (lane % HEAD_DIM) < (HEAD_DIM // 2)
    lo = lane < HEAD_DIM

    def rope(x, c, s):
        partner = jnp.where(first_half, pltpu.roll(x, LANES - HEAD_DIM // 2, 1),
                            pltpu.roll(x, HEAD_DIM // 2, 1))
        return x * c + partner * s

    cc, sc = cc_ref[...], sc_ref[...]
    k = jnp.concatenate([rope(kp_ref[...].astype(F32), cp_ref[...], sp_ref[...]),
                         rope(kc_ref[...].astype(F32), cc, sc)], axis=0)
    v = jnp.concatenate([vp_ref[...].astype(F32), vc_ref[...].astype(F32)], axis=0)
    k_sw = pltpu.roll(k, HEAD_DIM, 1)
    v_sw = pltpu.roll(v, HEAD_DIM, 1)

    def place(a, a_sw, c, half):
        src = a if c == half else a_sw
        keep = lo if half == 0 else jnp.logical_not(lo)
        return jnp.where(keep, src, 0.0).astype(BF16)

    kvar = {(c, h): place(k, k_sw, c, h) for c in range(ATT_KV_HEADS) for h in range(2)}
    vvar = {(c, h): place(v, v_sw, c, h) for c in range(ATT_KV_HEADS) for h in range(2)}

    qi = lax.broadcasted_iota(jnp.int32, (BLOCK, 2 * BLOCK), 0) + BLOCK
    ki = lax.broadcasted_iota(jnp.int32, (BLOCK, 2 * BLOCK), 1)
    dist = qi - ki
    kmin = jnp.where(n > 0, 0, BLOCK)
    valid = (dist >= 0) & (dist < BLOCK) & (ki >= kmin)

    group = ATT_Q_HEADS // ATT_KV_HEADS
    for j in range(ATT_WIDTH // LANES):
        c = (2 * j) // group
        cols = slice(j * LANES, (j + 1) * LANES)
        qg = (rope(q_ref[:, cols].astype(F32), cc, sc) * (HEAD_DIM ** -0.5)).astype(BF16)
        out = jnp.zeros((BLOCK, LANES), F32)
        for half in range(2):
            s = jnp.where(valid, _dot_nt(qg, kvar[(c, half)]), -jnp.inf)
            sink = sink_ref[2 * j + half]
            m = jnp.maximum(jnp.max(s, axis=-1, keepdims=True), sink)
            p = jnp.exp(s - m)
            denom = jnp.sum(p, axis=-1, keepdims=True) + jnp.exp(sink - m)
            out = out + _dot((p / denom).astype(BF16), vvar[(c, half)])
        sq = out * out
        ss_lo = jnp.sum(jnp.where(lo, sq, 0.0), axis=-1, keepdims=True)
        ss_hi = jnp.sum(jnp.where(lo, 0.0, sq), axis=-1, keepdims=True)
        ms = jnp.where(lo, ss_lo, ss_hi) * (1.0 / HEAD_DIM)
        o_ref[:, cols] = (out * lax.rsqrt(ms + EPS) * gain_ref[:, cols]).astype(o_ref.dtype)


def _swa(h, cos_a, sin_a, sinks, gain, batch, seq):
    nb = seq // BLOCK
    t = batch * seq
    kcol, vcol = _KA0 // LANES, _VA0 // LANES
    cur = lambda b, n: b * nb + n
    prev = lambda b, n: b * nb + jnp.maximum(n - 1, 0)
    row = lambda col, f: pl.BlockSpec((BLOCK, LANES), lambda b, n: (f(b, n), col))
    return pl.pallas_call(
        _swa_kernel,
        grid=(batch, nb),
        in_specs=[
            pl.BlockSpec(memory_space=pltpu.SMEM),
            pl.BlockSpec((BLOCK, ATT_WIDTH), lambda b, n: (cur(b, n), _QA0 // ATT_WIDTH)),
            row(kcol, cur), row(kcol, prev), row(vcol, cur), row(vcol, prev),
            row(0, cur), row(0, cur), row(0, prev), row(0, prev),
            pl.BlockSpec((1, ATT_WIDTH), lambda b, n: (0, 0)),
        ],
        out_specs=pl.BlockSpec((BLOCK, ATT_WIDTH), lambda b, n: (cur(b, n), 0)),
        out_shape=jax.ShapeDtypeStruct((t, ATT_WIDTH), BF16),
        compiler_params=_params("parallel", "parallel"),
        name="swa_attention",
    )(sinks, h, h, h, h, h, cos_a, sin_a, cos_a, sin_a, gain.reshape(1, ATT_WIDTH))


def _ret_kernel(lg_ref, cd_ref, q_ref, k_ref, v_ref, g_ref, c_ref, s_ref, gain_ref, o_ref, state_ref):
    n = pl.program_id(1)

    @pl.when(n == 0)
    def _():
        state_ref[...] = jnp.zeros_like(state_ref)

    c, s = c_ref[...], s_ref[...]
    hd = RET_HEAD_DIM
    half = hd // 2

    def rope(x):
        x1, x2 = x[:, :half], x[:, half:]
        return jnp.concatenate([x1 * c - x2 * s, x2 * c + x1 * s], axis=1)

    ri = lax.broadcasted_iota(jnp.int32, (RET_CHUNK, RET_CHUNK), 0).astype(F32)
    ci = lax.broadcasted_iota(jnp.int32, (RET_CHUNK, RET_CHUNK), 1).astype(F32)
    diff = ri - ci
    for hh in range(RET_HEADS):
        cols = slice(hh * hd, (hh + 1) * hd)
        lg = lg_ref[hh]
        q = rope(q_ref[:, cols].astype(F32))
        k = rope(k_ref[:, cols].astype(F32)) * (hd ** -0.5)
        v = v_ref[:, cols].astype(BF16)

        decay = jnp.where(diff >= 0, jnp.exp(jnp.maximum(diff, 0.0) * lg), 0.0)
        zeta = jnp.exp((RET_CHUNK - 1 - ri) * lg)
        xi = jnp.exp((ri + 1.0) * lg)
        zeta2 = jnp.concatenate([zeta, zeta], axis=1)
        xi2 = jnp.concatenate([xi, xi], axis=1)

        inner = _dot_nt(q.astype(BF16), k.astype(BF16)) * decay
        state = state_ref[hh]
        out = _dot(inner.astype(BF16), v) + _dot((q * xi2).astype(BF16), state.astype(BF16))
        kv = lax.dot_general((k * zeta2).astype(BF16), v, (((0,), (0,)), ((), ())), preferred_element_type=F32)
        state_ref[hh] = state * cd_ref[hh] + kv

        mu = jnp.mean(out, axis=-1, keepdims=True)
        cen = out - mu
        var = jnp.mean(cen * cen, axis=-1, keepdims=True)
        g = g_ref[:, cols].astype(F32)
        o = cen * lax.rsqrt(var + GN_EPS) * gain_ref[:, cols] * (g * jax.nn.sigmoid(g))
        o_ref[:, cols] = o.astype(o_ref.dtype)


def _retention(h, cos_r, sin_r, gain, batch, seq):
    nc = seq // RET_CHUNK
    t = batch * seq
    hd = RET_HEAD_DIM
    log_gamma = jnp.log1p(-jnp.exp2(-5.0 - jnp.arange(RET_HEADS, dtype=F32)))
    chunk_decay = jnp.exp(RET_CHUNK * log_gamma)
    col = lambda c0: pl.BlockSpec((RET_CHUNK, RET_WIDTH), lambda b, n: (b * nc + n, c0 // RET_WIDTH))
    tab = pl.BlockSpec((RET_CHUNK, hd // 2), lambda b, n: (b * nc + n, 0))
    return pl.pallas_call(
        _ret_kernel,
        grid=(batch, nc),
        in_specs=[
            pl.BlockSpec(memory_space=pltpu.SMEM),
            pl.BlockSpec(memory_space=pltpu.SMEM),
            col(_QR0), col(_KR0), col(_VR0), col(_GR0), tab, tab,
            pl.BlockSpec((1, RET_WIDTH), lambda b, n: (0, 0)),
        ],
        out_specs=pl.BlockSpec((RET_CHUNK, RET_WIDTH), lambda b, n: (b * nc + n, 0)),
        out_shape=jax.ShapeDtypeStruct((t, RET_WIDTH), BF16),
        scratch_shapes=[pltpu.VMEM((RET_HEADS, hd, hd), F32)],
        compiler_params=_params("parallel", "arbitrary"),
        name="retention",
    )(log_gamma, chunk_decay, h, h, h, h, cos_r, sin_r, gain.reshape(1, RET_WIDTH))


def _outproj_kernel(x_ref, oa_ref, or_ref, wa_ref, wr_ref, g_ref, wq_ref, x1_ref, qx_ref):
    x1 = x_ref[...] + _dot(oa_ref[...], wa_ref[...]) + _dot(or_ref[...], wr_ref[...])
    x1_ref[...] = x1
    qx_ref[...] = _dot(_rms(x1, g_ref[...]).astype(BF16), wq_ref[...]).astype(qx_ref.dtype)


def _outproj(x, oa, orr, w_out_bf16, gain, w_xq_bf16, tm, row0=0):
    t, d = oa.shape[0], x.shape[1]
    blk0 = row0 // tm
    full = lambda shape: pl.BlockSpec(shape, lambda i: (0, 0))
    return pl.pallas_call(
        _outproj_kernel,
        grid=(t // tm,),
        in_specs=[
            pl.BlockSpec((tm, d), lambda i: (i + blk0, 0)),
            pl.BlockSpec((tm, ATT_WIDTH), lambda i: (i, 0)),
            pl.BlockSpec((tm, RET_WIDTH), lambda i: (i, 0)),
            pl.BlockSpec((ATT_WIDTH, d), lambda i: (0, 0)),
            pl.BlockSpec((RET_WIDTH, d), lambda i: (1, 0)),
            full((1, d)),
            full((d, XATT_WIDTH)),
        ],
        out_specs=[pl.BlockSpec((tm, d), lambda i: (i, 0)), pl.BlockSpec((tm, XATT_WIDTH), lambda i: (i, 0))],
        out_shape=[jax.ShapeDtypeStruct((t, d), F32), jax.ShapeDtypeStruct((t, XATT_WIDTH), BF16)],
        compiler_params=_params("parallel"),
        name="out_proj",
    )(x, oa, orr, w_out_bf16, w_out_bf16, gain.reshape(1, d), w_xq_bf16)


def _xattn_kernel(qx_ref, k_ref, v_ref, x1_ref, wo_ref, g_ref, wpq_ref, x2_ref, pq_ref):
    heads = []
    for hh in range(XATT_HEADS):
        cols = slice(hh * XATT_HEAD_DIM, (hh + 1) * XATT_HEAD_DIM)
        s = _dot_nt(qx_ref[:, cols], k_ref[:, cols]) * (XATT_HEAD_DIM ** -0.5)
        p = jnp.exp(s - jnp.max(s, axis=-1, keepdims=True))
        p = p / jnp.sum(p, axis=-1, keepdims=True)
        heads.append(_dot(p.astype(BF16), v_ref[:, cols]))
    o = jnp.concatenate(heads, axis=1).astype(BF16)
    x2 = x1_ref[...] + _dot(o, wo_ref[...])
    x2_ref[...] = x2
    pq_ref[...] = _dot(_rms(x2, g_ref[...]).astype(BF16), wpq_ref[...])


def _xattn(qx, kv_mem, x1, w_xo_bf16, gain, w_pq_bf16, batch, seq, mem_len, tm, batch0=0):
    t, d = x1.shape
    npq = w_pq_bf16.shape[1]
    nt = seq // tm
    rows = lambda width: pl.BlockSpec((tm, width), lambda b, i: (b * nt + i, 0))
    full = lambda shape: pl.BlockSpec(shape, lambda b, i: (0, 0))
    return pl.pallas_call(
        _xattn_kernel,
        grid=(batch, nt),
        in_specs=[
            rows(XATT_WIDTH),
            pl.BlockSpec((mem_len, XATT_WIDTH), lambda b, i: (b + batch0, 0)),
            pl.BlockSpec((mem_len, XATT_WIDTH), lambda b, i: (b + batch0, 1)),
            rows(d),
            full((XATT_WIDTH, d)),
            full((1, d)),
            full((d, npq)),
        ],
        out_specs=[rows(d), rows(npq)],
        out_shape=[jax.ShapeDtypeStruct((t, d), F32), jax.ShapeDtypeStruct((t, npq), F32)],
        compiler_params=_params("parallel", "parallel"),
        name="cross_attention",
    )(qx, kv_mem, kv_mem, x1, w_xo_bf16, gain.reshape(1, d), w_pq_bf16)


def _topk_rows(s, k, payload=None):
    nrows = s.shape[0]
    rows = lax.broadcasted_iota(jnp.int32, s.shape, 0).astype(F32)
    vals, sel = [], []
    for _ in range(k):
        m = jnp.max(s, axis=0, keepdims=True)
        am = jnp.min(jnp.where(s == m, rows, float(nrows)), axis=0, keepdims=True)
        hit = rows == am
        vals.append(m)
        sel.append(am if payload is None else jnp.max(jnp.where(hit, payload, -1.0), axis=0, keepdims=True))
        s = jnp.where(hit, -jnp.inf, s)
    return jnp.concatenate(vals, axis=0), jnp.concatenate(sel, axis=0)


def _staircase(t1, t2):
    kk, half = PEER_TOPK, SUBLANES
    blocks = [(t1[0:1, :], t2, kk), (t1[1:2, :], t2[0:half, :], half)]
    blocks += [(t1[a:a + 1, :], t2[0:half, :], kk // (a + 1)) for a in range(2, half)]
    blocks.append((t1[half:kk, :], t2[0:1, :], half))
    return blocks


def _peer_topk_kernel(q_ref, keys_ref, ids_ref, gates_ref, *, heads):
    width = 2 * PEER_HALF_DIM
    row8 = lax.broadcasted_iota(jnp.int32, (SUBLANES, q_ref.shape[0]), 0)
    for hh in range(heads):
        tops = []
        for p in range(2):
            c0 = hh * width + p * PEER_HALF_DIM
            qp = q_ref[:, c0:c0 + PEER_HALF_DIM].astype(BF16)
            tops.append(_topk_rows(_dot_nt(keys_ref[hh, p], qp), PEER_TOPK))
        (s1, i1), (s2, i2) = tops
        cand_s, cand_i = [], []
        for (a_s, b_s, live), (a_i, b_i, _) in zip(_staircase(s1, s2), _staircase(i1 * float(PEER_N_KEYS), i2)):
            blk = a_s + b_s
            if live < blk.shape[0]:
                blk = jnp.where(row8 < live, blk, -jnp.inf)
            cand_s.append(blk)
            cand_i.append(a_i + b_i)
        top_s, top_e = _topk_rows(jnp.concatenate(cand_s, axis=0), PEER_TOPK,
                                  payload=jnp.concatenate(cand_i, axis=0))
        e = jnp.exp(top_s - top_s[0:1, :])
        ids_ref[hh] = top_e.astype(jnp.int32)
        gates_ref[hh] = e / jnp.sum(e, axis=0, keepdims=True)


def _peer_topk(pq, keys_bf16, tt, heads):
    t = pq.shape[0]
    width = 2 * PEER_HALF_DIM
    out = pl.BlockSpec((heads, PEER_TOPK, tt), lambda i, hh: (hh, 0, i))
    return pl.pallas_call(
        functools.partial(_peer_topk_kernel, heads=heads),
        grid=(t // tt, PEER_HEADS // heads),
        in_specs=[
            pl.BlockSpec((tt, heads * width), lambda i, hh: (i, hh)),
            pl.BlockSpec((heads, 2, PEER_N_KEYS, PEER_HALF_DIM), lambda i, hh: (hh, 0, 0, 0)),
        ],
        out_specs=[out, out],
        out_shape=[jax.ShapeDtypeStruct((PEER_HEADS, PEER_TOPK, t), jnp.int32),
                   jax.ShapeDtypeStruct((PEER_HEADS, PEER_TOPK, t), F32)],
        compiler_params=_params("parallel", "parallel"),
        name="peer_topk",
    )(pq, keys_bf16)


GATHER_TOKENS = 128
GATHER_BUFS = 8
GATHER_GROUP = 8
assert GATHER_GROUP % GATHER_BUFS == 0 and GATHER_GROUP % SUBLANES == 0 and GATHER_TOKENS % GATHER_GROUP == 0


def _peer_gather_kernel(ids_hbm, tbl_hbm, x2_ref, gates_ref, fg_ref, og_ref, *rest, final_norm, first_block,
                        has_prev):
    out_ref, ids_smem, ids_sem, *scratch = rest[1:] if has_prev else rest
    gbufs = scratch[:GATHER_BUFS]
    gsem, xn_ref, y_ref, yacc_ref = scratch[GATHER_BUFS:]
    i = pl.program_id(0)
    has_next = i + 1 < pl.num_programs(0)
    slot = i % 2
    other = 1 - slot
    tb = GATHER_TOKENS
    nchunk = D_MODEL // LANES
    nids = tb * PEER_SLOTS
    ahead = GATHER_BUFS - 1
    group = GATHER_GROUP

    def ids_copy(step, sl):
        return pltpu.make_async_copy(ids_hbm.at[pl.ds(pl.multiple_of((first_block + step) * nids, nids), nids)],
                                     ids_smem.at[pl.ds(pl.multiple_of(sl * nids, nids), nids)], ids_sem.at[sl])

    def row_copy(e, b, k):
        return pltpu.make_async_copy(tbl_hbm.at[e], gbufs[b].at[pl.ds(k, 1)], gsem.at[b])

    def issue(sl, tok, b):
        base = sl * nids + tok * PEER_SLOTS
        for k in range(PEER_SLOTS):
            row_copy(ids_smem[base + k], b, k).start(priority=k % 2)

    def wait_rows(b):
        for k in range(PEER_SLOTS):
            row_copy(0, b, k).wait()

    @pl.when(i == 0)
    def _():
        ids_copy(0, 0).start()
        ids_copy(0, 0).wait()
        for tok in range(ahead):
            issue(0, tok, tok)

    @pl.when(has_next)
    def _():
        ids_copy(i + 1, other).start()

    xn_ref[...] = _rms(x2_ref[...], fg_ref[...])

    lane = lax.broadcasted_iota(jnp.int32, (PEER_SLOTS, LANES), 1)
    hi_mask = jnp.uint32(0xFFFF0000)

    def compute(tok, r, b):
        gbuf = gbufs[b]
        xrow = xn_ref[pl.ds(tok, 1), :]
        acc = jnp.zeros((PEER_SLOTS, LANES), F32)
        for c in range(nchunk):
            w = gbuf[:, c * LANES:(c + 1) * LANES]
            u = lax.bitcast_convert_type(w << 16, F32)
            acc = acc + u * xrow[:, c * LANES:(c + 1) * LANES]
        a = jnp.sum(acc, axis=-1, keepdims=True)
        gate = jnp.sum(jnp.where(lane == tok, gates_ref[...], 0.0), axis=-1, keepdims=True)
        coeff = 0.5 * a * (1.0 + lax.erf(a * (2.0 ** -0.5))) * gate
        for c in range(nchunk):
            w = gbuf[:, c * LANES:(c + 1) * LANES]
            vv = lax.bitcast_convert_type(w & hi_mask, F32)
            yacc_ref[r:r + 1, c * LANES:(c + 1) * LANES] = jnp.sum(vv * coeff, axis=0, keepdims=True)

    def run_group(g, last):
        for r in range(group):
            b = r % GATHER_BUFS
            nb = (r + ahead) % GATHER_BUFS
            tok = g * group + r
            wait_rows(b)
            if not last or r + ahead < group:
                issue(slot, tok + ahead, nb)
            else:
                if r + ahead == group:
                    @pl.when(has_next)
                    def _():
                        ids_copy(i + 1, other).wait()

                @pl.when(has_next)
                def _():
                    issue(other, r + ahead - group, nb)
            compute(tok, r, b)
        y_ref[pl.ds(pl.multiple_of(g * group, group), group), :] = yacc_ref[...]

    def body(g, carry):
        run_group(g, last=False)
        return carry

    ngroups = tb // group
    lax.fori_loop(0, ngroups - 1, body, 0)
    run_group(ngroups - 1, last=True)
    x3 = x2_ref[...] + y_ref[...]
    out_ref[...] = _rms(x3, og_ref[...]) if final_norm else x3


def _peer_gather(ids_flat, table, x2, gates_t, ffn_gain, final_gain, final_norm, first_block, out_rows,
                 out_block0, prev):
    t, d = x2.shape
    tb = GATHER_TOKENS
    nsteps = t // tb - first_block
    full = lambda shape: pl.BlockSpec(shape, lambda i: (0, 0))
    has_prev = prev is not None
    return pl.pallas_call(
        functools.partial(_peer_gather_kernel, final_norm=final_norm, first_block=first_block, has_prev=has_prev),
        grid=(nsteps,),
        in_specs=[
            pl.BlockSpec(memory_space=pl.ANY),
            pl.BlockSpec(memory_space=pl.ANY),
            pl.BlockSpec((tb, d), lambda i: (i + first_block, 0)),
            pl.BlockSpec((PEER_SLOTS, tb), lambda i: (0, i + first_block)),
            full((1, d)),
            full((1, d)),
        ] + ([pl.BlockSpec(memory_space=pl.ANY)] if has_prev else []),
        out_specs=pl.BlockSpec((tb, d), lambda i: (i + first_block + out_block0, 0)),
        out_shape=jax.ShapeDtypeStruct((out_rows, d), F32),
        input_output_aliases={6: 0} if has_prev else {},
        scratch_shapes=[
            pltpu.SMEM((2 * tb * PEER_SLOTS,), jnp.int32),
            pltpu.SemaphoreType.DMA((2,)),
            *[pltpu.VMEM((PEER_SLOTS, d), jnp.uint32) for _ in range(GATHER_BUFS)],
            pltpu.SemaphoreType.DMA((GATHER_BUFS,)),
            pltpu.VMEM((tb, d), F32),
            pltpu.VMEM((tb, d), F32),
            pltpu.VMEM((GATHER_GROUP, d), F32),
        ],
        compiler_params=_params("arbitrary"),
        name="peer_gather",
    )(ids_flat, table, x2, gates_t, ffn_gain.reshape(1, d), final_gain.reshape(1, d), *([prev] if has_prev else []))


SC_CORES, SC_SUBCORES, SC_LANES = 2, 16, 16
SC_WORKERS = SC_CORES * SC_SUBCORES
SC_ROWS = SC_LANES
SC_COLS = 8
SC_DOT_ROWS = 8
SC_GROUPS, SC_GROUP_DENOM = (1, 1, 2, 4), 8
SC_SHARE = (63, 128)


def _sc_erf(x):
    ax = jnp.abs(x)
    t = 1.0 / (1.0 + 0.3275911 * ax)
    poly = t * (0.254829592 + t * (-0.284496736 + t * (1.421413741 + t * (-1.453152027 + t * 1.061405429))))
    y = 1.0 - poly * jnp.exp(-ax * ax)
    return jnp.where(x < 0, -y, y)


def _peer_sc(ids_flat, gates_tok, xn, table, ts):
    d = D_MODEL
    lanes, rows = SC_LANES, SC_ROWS
    nchunk = PEER_SLOTS // rows
    block = lanes * SC_COLS
    per_worker = ts // SC_WORKERS
    assert ts % (2 * SC_WORKERS) == 0 and d % block == 0 and nchunk % 2 == 0
    mesh = plsc.VectorSubcoreMesh(core_axis_name="c", subcore_axis_name="s")

    @functools.partial(
        pl.kernel, mesh=mesh,
        out_type=jax.ShapeDtypeStruct((ts, d), F32),
        scratch_types=[
            pltpu.VMEM((2, PEER_SLOTS), jnp.int32),
            pltpu.VMEM((2, PEER_SLOTS), F32),
            pltpu.VMEM((2, d), F32),
            pltpu.VMEM((d,), F32),
            pltpu.VMEM((2, rows, 1, d), jnp.uint32),
            pltpu.VMEM((rows, lanes), F32),
            pltpu.VMEM((lanes,), F32),
            pltpu.VMEM((rows, lanes), jnp.int32),
            pltpu.SemaphoreType.DMA((2,)),
            pltpu.SemaphoreType.DMA((2,)),
        ],
        compiler_params=pltpu.CompilerParams(needs_layout_passes=False),
        name="peer_sparsecore",
    )
    def run(ids_hbm, gates_hbm, xn_hbm, tbl_hbm, y_hbm, ids_v, gates_v, x_v, y_v, rows_v, acc_v, coeff_v, ridx_v,
            row_sem, in_sem):
        worker = lax.axis_index("s") * SC_CORES + lax.axis_index("c")
        first = worker * per_worker
        lane = lax.iota(jnp.int32, lanes)
        zero = jnp.zeros((lanes,), F32)
        hi_mask = jnp.full((lanes,), 0xFFFF0000, jnp.uint32)
        for r in range(rows):
            ridx_v[r, :] = jnp.full((lanes,), r, jnp.int32)

        def gather(par, c, slot):
            idx = ids_v[par, pl.ds(pl.multiple_of(c * rows, rows), rows)]
            return pltpu.make_async_copy(tbl_hbm.at[idx], rows_v.at[slot], row_sem.at[slot])

        def token_inputs(tok, par):
            return (pltpu.make_async_copy(ids_hbm.at[pl.ds(tok * PEER_SLOTS, PEER_SLOTS)], ids_v.at[par], in_sem.at[par]),
                    pltpu.make_async_copy(gates_hbm.at[pl.ds(tok * PEER_SLOTS, PEER_SLOTS)], gates_v.at[par],
                                          in_sem.at[par]),
                    pltpu.make_async_copy(xn_hbm.at[tok], x_v.at[par], in_sem.at[par]))

        def chunk(par, c, slot):
            for r0 in range(0, rows, SC_DOT_ROWS):
                def dot_body(j, accs, r0=r0):
                    xj = x_v[par, pl.ds(j * lanes, lanes)]
                    return tuple(
                        accs[q] + plsc.bitcast(rows_v[slot, r0 + q, 0, pl.ds(j * lanes, lanes)] << 16, F32) * xj
                        for q in range(SC_DOT_ROWS))
                accs = lax.fori_loop(0, d // lanes, dot_body, (zero,) * SC_DOT_ROWS, unroll=4)
                for q in range(SC_DOT_ROWS):
                    acc_v[r0 + q, :] = accs[q]
            a = zero
            for col in range(lanes):
                a = a + plsc.load_gather(acc_v, [lane, ridx_v[col, :]])
            gate = gates_v[par, pl.ds(pl.multiple_of(c * rows, rows), rows)]
            coeff_v[...] = 0.5 * a * (1.0 + _sc_erf(a * (2.0 ** -0.5))) * gate
            coeffs = [plsc.load_gather(coeff_v, [ridx_v[r, :]]) for r in range(rows)]

            @pl.loop(0, d // block)
            def _(jb):
                base = jb * block
                accs = [zero] * SC_COLS
                for r in range(rows):
                    for q in range(SC_COLS):
                        w = rows_v[slot, r, 0, pl.ds(base + q * lanes, lanes)]
                        accs[q] = accs[q] + plsc.bitcast(w & hi_mask, F32) * coeffs[r]
                for q in range(SC_COLS):
                    plsc.addupdate(y_v.at[pl.ds(base + q * lanes, lanes)], accs[q])

        for cp in token_inputs(first, 0):
            cp.start()

        @pl.loop(0, per_worker // 2)
        def _(pair):
            for par in range(2):
                i = pair * 2 + par
                tok = first + i
                for cp in token_inputs(tok, par):
                    cp.wait()

                @pl.when(i + 1 < per_worker)
                def _():
                    for cp in token_inputs(tok + 1, 1 - par):
                        cp.start()

                @pl.loop(0, d // lanes)
                def _(j):
                    y_v[pl.ds(j * lanes, lanes)] = zero

                gather(par, 0, 0).start()

                @pl.loop(0, nchunk // 2)
                def _(cpair):
                    for slot in range(2):
                        c = cpair * 2 + slot
                        if slot == 0:
                            gather(par, c + 1, 1).start()
                        else:
                            @pl.when(c + 1 < nchunk)
                            def _():
                                gather(par, c + 1, 0).start()
                        gather(par, c, slot).wait()
                        chunk(par, c, slot)

                pltpu.sync_copy(y_v, y_hbm.at[tok])

    return run(ids_flat, gates_tok, xn, table)


def _rms_rows_kernel(x_ref, g_ref, o_ref):
    o_ref[...] = _rms(x_ref[...], g_ref[...])


def _rms_rows(x, gain, rows, tm):
    d = x.shape[1]
    return pl.pallas_call(
        _rms_rows_kernel,
        grid=(rows // tm,),
        in_specs=[pl.BlockSpec((tm, d), lambda i: (i, 0)), pl.BlockSpec((1, d), lambda i: (0, 0))],
        out_specs=pl.BlockSpec((tm, d), lambda i: (i, 0)),
        out_shape=jax.ShapeDtypeStruct((rows, d), F32),
        compiler_params=_params("parallel"),
        name="rms_rows",
    )(x, gain.reshape(1, d))


def _peer_finish_kernel(x2_ref, y_ref, og_ref, prev_ref, out_ref, *, final_norm):
    del prev_ref
    x3 = x2_ref[...] + y_ref[...]
    out_ref[...] = _rms(x3, og_ref[...]) if final_norm else x3


def _peer_finish(x2, y_head, out_rest, final_gain, final_norm, out_block0):
    rows, d = y_head.shape
    tm = GATHER_TOKENS
    blk = pl.BlockSpec((tm, d), lambda i: (i, 0))
    return pl.pallas_call(
        functools.partial(_peer_finish_kernel, final_norm=final_norm),
        grid=(rows // tm,),
        in_specs=[blk, blk, pl.BlockSpec((1, d), lambda i: (0, 0)), pl.BlockSpec(memory_space=pl.ANY)],
        out_specs=pl.BlockSpec((tm, d), lambda i: (i + out_block0, 0)),
        out_shape=jax.ShapeDtypeStruct(out_rest.shape, F32),
        input_output_aliases={3: 0},
        compiler_params=_params("parallel"),
        name="peer_finish",
    )(x2, y_head, final_gain.reshape(1, d), out_rest)


def _rope_angles(positions, dh):
    inv_freq = ROPE_THETA ** (-jnp.arange(0, dh, 2, dtype=F32) / dh)
    ang = positions.astype(F32).reshape(-1, 1) * inv_freq
    return jnp.cos(ang), jnp.sin(ang)


def _rope_tables_att(positions):
    cos, sin = _rope_angles(positions, HEAD_DIM)
    copies = LANES // HEAD_DIM
    return (jnp.tile(jnp.concatenate([cos, cos], axis=1), (1, copies)),
            jnp.tile(jnp.concatenate([-sin, sin], axis=1), (1, copies)))


def _pack_table_kernel(u_ref, v_ref, o_ref):
    ub = lax.bitcast_convert_type(u_ref[...].astype(BF16).astype(F32), jnp.uint32)
    vb = lax.bitcast_convert_type(v_ref[...].astype(BF16).astype(F32), jnp.uint32)
    o_ref[:, 0, :] = (ub >> 16) | vb


def _pack_expert_table(u, v, rows=PACK_ROWS):
    n, d = u.shape
    blk = pl.BlockSpec((rows, d), lambda i: (i, 0))
    return pl.pallas_call(
        _pack_table_kernel,
        grid=(n // rows,),
        in_specs=[blk, blk],
        out_specs=pl.BlockSpec((rows, 1, d), lambda i: (i, 0, 0)),
        out_shape=jax.ShapeDtypeStruct((n, 1, d), jnp.uint32),
        compiler_params=_params("parallel"),
        name="pack_table",
    )(u, v)


def kernel(x, mem, positions, mix_norm_gain, w_in, att_sinks, att_out_gain, ret_out_gain, w_out,
           cross_norm_gain, mem_norm_gain, w_xq, w_xk, w_xv, w_xo, ffn_norm_gain,
           w_peer_q, peer_sub_keys, peer_u, peer_v, final_norm_gain):
    batch, seq, d = x.shape
    mem_len = mem.shape[1]
    t = batch * seq
    depth = w_in.shape[0]
    assert d == D_MODEL and seq % BLOCK == 0 and t % GATHER_TOKENS == 0

    cos_a, sin_a = _rope_tables_att(positions)
    cos_r, sin_r = _rope_angles(positions, RET_HEAD_DIM)
    xf = x.reshape(t, d)
    memf = mem.reshape(batch * mem_len, d)
    nblocks = t // GATHER_TOKENS
    split = batch % SC_GROUP_DENOM == 0 and nblocks % SC_SHARE[1] == 0
    groups = [batch * g // SC_GROUP_DENOM for g in SC_GROUPS] if split else [batch]
    sc_total = nblocks * SC_SHARE[0] // SC_SHARE[1] if split else 0
    tm_mid = min(ROW_TM, seq)
    for l in range(depth):
        sc_left = sc_total
        w_in_b = _reorder_w_in(w_in[l]).astype(BF16)
        w_out_b, w_xq_b, w_xo_b, w_pq_b = (w.astype(BF16) for w in (w_out[l], w_xq[l], w_xo[l], w_peer_q[l]))
        keys_b = peer_sub_keys[l].astype(BF16)
        w_kv = jnp.concatenate([w_xk[l], w_xv[l]], axis=1).astype(BF16)
        kv_mem = _norm_matmul(memf, mem_norm_gain[l], w_kv, mem_len, 2 * XATT_WIDTH, BF16)
        table = _pack_expert_table(peer_u[l], peer_v[l])
        last = l == depth - 1
        out = None
        pending = []
        b0 = 0
        for bp in groups:
            r0, tp = b0 * seq, bp * seq
            block0, blocks_p = r0 // GATHER_TOKENS, tp // GATHER_TOKENS
            rows = slice(r0, r0 + tp)
            h = _norm_matmul(xf, mix_norm_gain[l], w_in_b, min(IN_PROJ_TM, tp), IN_PROJ_TN, BF16, row0=r0, rows=tp)
            oa = _swa(h, cos_a[rows], sin_a[rows], att_sinks[l], att_out_gain[l], bp, seq)
            orr = _retention(h, cos_r[rows], sin_r[rows], ret_out_gain[l], bp, seq)
            x1, qx = _outproj(xf, oa, orr, w_out_b, cross_norm_gain[l], w_xq_b, tm_mid, row0=r0)
            x2, pq = _xattn(qx, kv_mem, x1, w_xo_b, ffn_norm_gain[l], w_pq_b, bp, seq, mem_len, tm_mid, batch0=b0)
            ids_t, gates_t = _peer_topk(pq, keys_b, LANES, PEER_HEADS)
            ids = ids_t.reshape(PEER_SLOTS, tp).T.reshape(tp * PEER_SLOTS)
            gates_t = gates_t.reshape(PEER_SLOTS, tp)
            sc_blocks = min(sc_left, blocks_p)
            sc_left -= sc_blocks
            if sc_blocks:
                ts = sc_blocks * GATHER_TOKENS
                xn_head = _rms_rows(x2, ffn_norm_gain[l], ts, GATHER_TOKENS)
                y_head = _peer_sc(ids, gates_t[:, :ts].T.reshape(ts * PEER_SLOTS), xn_head, table, ts)
                pending.append((x2, y_head, block0))
            if sc_blocks < blocks_p:
                out = _peer_gather(ids, table, x2, gates_t, ffn_norm_gain[l], final_norm_gain, last, sc_blocks,
                                   t, block0, out)
            b0 += bp
        for x2, y_head, block0 in pending:
            out = _peer_finish(x2, y_head, out, final_norm_gain, last, block0)
        xf = out
    return xf.reshape(batch, seq, d)
```

```python
import functools

import jax
import jax.numpy as jnp
from jax import lax
from jax.experimental import pallas as pl
from jax.experimental.pallas import tpu as pltpu
from jax.experimental.pallas import tpu_sc as plsc

F32 = jnp.float32
BF16 = jnp.bfloat16

EPS = 1e-6
GN_EPS = 1e-5
ROPE_THETA = 10000.0
D_MODEL = 2048
HEAD_DIM = 64
ATT_Q_HEADS = 16
ATT_KV_HEADS = 2
ATT_WIDTH = ATT_Q_HEADS * HEAD_DIM
ATT_KV_WIDTH = ATT_KV_HEADS * HEAD_DIM
BLOCK = 128
RET_HEADS = 4
RET_HEAD_DIM = 256
RET_WIDTH = RET_HEADS * RET_HEAD_DIM
RET_CHUNK = 128
IN_COLS = ATT_WIDTH + 2 * ATT_KV_WIDTH + 4 * RET_WIDTH
XATT_HEADS = 4
XATT_HEAD_DIM = 128
XATT_WIDTH = XATT_HEADS * XATT_HEAD_DIM
PEER_HEADS = 8
PEER_N_KEYS = 128
PEER_HALF_DIM = 128
PEER_TOPK = 16
PEER_SLOTS = PEER_HEADS * PEER_TOPK

LANES = 128
SUBLANES = 8
VMEM_LIMIT = 56 * 1024 * 1024
IN_PROJ_TM, IN_PROJ_TN = 1024, 1792
ROW_TM = 256
PACK_ROWS = 256

_REF_QR0 = ATT_WIDTH + 2 * ATT_KV_WIDTH
_QR0 = 0
_KR0 = _QR0 + RET_WIDTH
_VR0 = _KR0 + RET_WIDTH
_GR0 = _VR0 + RET_WIDTH
_QA0 = _GR0 + RET_WIDTH
_KA0 = _QA0 + ATT_WIDTH
_VA0 = _KA0 + ATT_KV_WIDTH


def _reorder_w_in(w):
    return jnp.concatenate([w[:, _REF_QR0:], w[:, :_REF_QR0]], axis=1)


def _params(*sem):
    return pltpu.CompilerParams(dimension_semantics=sem, vmem_limit_bytes=VMEM_LIMIT)


def _rms(xf, gain):
    ms = jnp.mean(xf * xf, axis=-1, keepdims=True)
    return xf * lax.rsqrt(ms + EPS) * gain


def _dot(a, b):
    return jnp.dot(a, b, preferred_element_type=F32)


def _dot_nt(a, b):
    return lax.dot_general(a, b, (((1,), (1,)), ((), ())), preferred_element_type=F32)


def _norm_matmul_kernel(x_ref, g_ref, w_ref, o_ref, xn_ref):
    @pl.when(pl.program_id(1) == 0)
    def _():
        xn_ref[...] = _rms(x_ref[...], g_ref[...]).astype(BF16)

    o_ref[...] = _dot(xn_ref[...], w_ref[...]).astype(o_ref.dtype)


def _norm_matmul(x, gain, w_bf16, tm, tn, out_dtype, row0=0, rows=None):
    d = x.shape[1]
    t = x.shape[0] if rows is None else rows
    n = w_bf16.shape[1]
    blk0 = row0 // tm
    return pl.pallas_call(
        _norm_matmul_kernel,
        grid=(t // tm, n // tn),
        in_specs=[
            pl.BlockSpec((tm, d), lambda i, j: (i + blk0, 0)),
            pl.BlockSpec((1, d), lambda i, j: (0, 0)),
            pl.BlockSpec((d, tn), lambda i, j: (0, j)),
        ],
        out_specs=pl.BlockSpec((tm, tn), lambda i, j: (i, j)),
        out_shape=jax.ShapeDtypeStruct((t, n), out_dtype),
        scratch_shapes=[pltpu.VMEM((tm, d), BF16)],
        compiler_params=_params("parallel", "arbitrary"),
        name="norm_matmul",
    )(x, gain.reshape(1, d), w_bf16)


def _swa_kernel(sink_ref, q_ref, kc_ref, kp_ref, vc_ref, vp_ref, cc_ref, sc_ref, cp_ref, sp_ref,
                gain_ref, o_ref):
    n = pl.program_id(1)
    lane = lax.broadcasted_iota(jnp.int32, (1, LANES), 1)
    first_half = (lane % HEAD_DIM) < (HEAD_DIM // 2)
    lo = lane < HEAD_DIM

    def rope(x, c, s):
        partner = jnp.where(first_half, pltpu.roll(x, LANES - HEAD_DIM // 2, 1),
                            pltpu.roll(x, HEAD_DIM // 2, 1))
        return x * c + partner * s

    cc, sc = cc_ref[...], sc_ref[...]
    k = jnp.concatenate([rope(kp_ref[...].astype(F32), cp_ref[...], sp_ref[...]),
                         rope(kc_ref[...].astype(F32), cc, sc)], axis=0)
    v = jnp.concatenate([vp_ref[...].astype(F32), vc_ref[...].astype(F32)], axis=0)
    k_sw = pltpu.roll(k, HEAD_DIM, 1)
    v_sw = pltpu.roll(v, HEAD_DIM, 1)

    def place(a, a_sw, c, half):
        src = a if c == half else a_sw
        keep = lo if half == 0 else jnp.logical_not(lo)
        return jnp.where(keep, src, 0.0).astype(BF16)

    kvar = {(c, h): place(k, k_sw, c, h) for c in range(ATT_KV_HEADS) for h in range(2)}
    vvar = {(c, h): place(v, v_sw, c, h) for c in range(ATT_KV_HEADS) for h in range(2)}

    qi = lax.broadcasted_iota(jnp.int32, (BLOCK, 2 * BLOCK), 0) + BLOCK
    ki = lax.broadcasted_iota(jnp.int32, (BLOCK, 2 * BLOCK), 1)
    dist = qi - ki
    kmin = jnp.where(n > 0, 0, BLOCK)
    valid = (dist >= 0) & (dist < BLOCK) & (ki >= kmin)

    group = ATT_Q_HEADS // ATT_KV_HEADS
    for j in range(ATT_WIDTH // LANES):
        c = (2 * j) // group
        cols = slice(j * LANES, (j + 1) * LANES)
        qg = (rope(q_ref[:, cols].astype(F32), cc, sc) * (HEAD_DIM ** -0.5)).astype(BF16)
        out = jnp.zeros((BLOCK, LANES), F32)
        for half in range(2):
            s = jnp.where(valid, _dot_nt(qg, kvar[(c, half)]), -jnp.inf)
            sink = sink_ref[2 * j + half]
            m = jnp.maximum(jnp.max(s, axis=-1, keepdims=True), sink)
            p = jnp.exp(s - m)
            denom = jnp.sum(p, axis=-1, keepdims=True) + jnp.exp(sink - m)
            out = out + _dot((p / denom).astype(BF16), vvar[(c, half)])
        sq = out * out
        ss_lo = jnp.sum(jnp.where(lo, sq, 0.0), axis=-1, keepdims=True)
        ss_hi = jnp.sum(jnp.where(lo, 0.0, sq), axis=-1, keepdims=True)
        ms = jnp.where(lo, ss_lo, ss_hi) * (1.0 / HEAD_DIM)
        o_ref[:, cols] = (out * lax.rsqrt(ms + EPS) * gain_ref[:, cols]).astype(o_ref.dtype)


def _swa(h, cos_a, sin_a, sinks, gain, batch, seq):
    nb = seq // BLOCK
    t = batch * seq
    kcol, vcol = _KA0 // LANES, _VA0 // LANES
    cur = lambda b, n: b * nb + n
    prev = lambda b, n: b * nb + jnp.maximum(n - 1, 0)
    row = lambda col, f: pl.BlockSpec((BLOCK, LANES), lambda b, n: (f(b, n), col))
    return pl.pallas_call(
        _swa_kernel,
        grid=(batch, nb),
        in_specs=[
            pl.BlockSpec(memory_space=pltpu.SMEM),
            pl.BlockSpec((BLOCK, ATT_WIDTH), lambda b, n: (cur(b, n), _QA0 // ATT_WIDTH)),
            row(kcol, cur), row(kcol, prev), row(vcol, cur), row(vcol, prev),
            row(0, cur), row(0, cur), row(0, prev), row(0, prev),
            pl.BlockSpec((1, ATT_WIDTH), lambda b, n: (0, 0)),
        ],
        out_specs=pl.BlockSpec((BLOCK, ATT_WIDTH), lambda b, n: (cur(b, n), 0)),
        out_shape=jax.ShapeDtypeStruct((t, ATT_WIDTH), BF16),
        compiler_params=_params("parallel", "parallel"),
        name="swa_attention",
    )(sinks, h, h, h, h, h, cos_a, sin_a, cos_a, sin_a, gain.reshape(1, ATT_WIDTH))


def _ret_kernel(lg_ref, cd_ref, q_ref, k_ref, v_ref, g_ref, c_ref, s_ref, gain_ref, o_ref, state_ref):
    n = pl.program_id(1)

    @pl.when(n == 0)
    def _():
        state_ref[...] = jnp.zeros_like(state_ref)

    c, s = c_ref[...], s_ref[...]
    hd = RET_HEAD_DIM
    half = hd // 2

    def rope(x):
        x1, x2 = x[:, :half], x[:, half:]
        return jnp.concatenate([x1 * c - x2 * s, x2 * c + x1 * s], axis=1)

    ri = lax.broadcasted_iota(jnp.int32, (RET_CHUNK, RET_CHUNK), 0).astype(F32)
    ci = lax.broadcasted_iota(jnp.int32, (RET_CHUNK, RET_CHUNK), 1).astype(F32)
    diff = ri - ci
    for hh in range(RET_HEADS):
        cols = slice(hh * hd, (hh + 1) * hd)
        lg = lg_ref[hh]
        q = rope(q_ref[:, cols].astype(F32))
        k = rope(k_ref[:, cols].astype(F32)) * (hd ** -0.5)
        v = v_ref[:, cols].astype(BF16)

        decay = jnp.where(diff >= 0, jnp.exp(jnp.maximum(diff, 0.0) * lg), 0.0)
        zeta = jnp.exp((RET_CHUNK - 1 - ri) * lg)
        xi = jnp.exp((ri + 1.0) * lg)
        zeta2 = jnp.concatenate([zeta, zeta], axis=1)
        xi2 = jnp.concatenate([xi, xi], axis=1)

        inner = _dot_nt(q.astype(BF16), k.astype(BF16)) * decay
        state = state_ref[hh]
        out = _dot(inner.astype(BF16), v) + _dot((q * xi2).astype(BF16), state.astype(BF16))
        kv = lax.dot_general((k * zeta2).astype(BF16), v, (((0,), (0,)), ((), ())), preferred_element_type=F32)
        state_ref[hh] = state * cd_ref[hh] + kv

        mu = jnp.mean(out, axis=-1, keepdims=True)
        cen = out - mu
        var = jnp.mean(cen * cen, axis=-1, keepdims=True)
        g = g_ref[:, cols].astype(F32)
        o = cen * lax.rsqrt(var + GN_EPS) * gain_ref[:, cols] * (g * jax.nn.sigmoid(g))
        o_ref[:, cols] = o.astype(o_ref.dtype)


def _retention(h, cos_r, sin_r, gain, batch, seq):
    nc = seq // RET_CHUNK
    t = batch * seq
    hd = RET_HEAD_DIM
    log_gamma = jnp.log1p(-jnp.exp2(-5.0 - jnp.arange(RET_HEADS, dtype=F32)))
    chunk_decay = jnp.exp(RET_CHUNK * log_gamma)
    col = lambda c0: pl.BlockSpec((RET_CHUNK, RET_WIDTH), lambda b, n: (b * nc + n, c0 // RET_WIDTH))
    tab = pl.BlockSpec((RET_CHUNK, hd // 2), lambda b, n: (b * nc + n, 0))
    return pl.pallas_call(
        _ret_kernel,
        grid=(batch, nc),
        in_specs=[
            pl.BlockSpec(memory_space=pltpu.SMEM),
            pl.BlockSpec(memory_space=pltpu.SMEM),
            col(_QR0), col(_KR0), col(_VR0), col(_GR0), tab, tab,
            pl.BlockSpec((1, RET_WIDTH), lambda b, n: (0, 0)),
        ],
        out_specs=pl.BlockSpec((RET_CHUNK, RET_WIDTH), lambda b, n: (b * nc + n, 0)),
        out_shape=jax.ShapeDtypeStruct((t, RET_WIDTH), BF16),
        scratch_shapes=[pltpu.VMEM((RET_HEADS, hd, hd), F32)],
        compiler_params=_params("parallel", "arbitrary"),
        name="retention",
    )(log_gamma, chunk_decay, h, h, h, h, cos_r, sin_r, gain.reshape(1, RET_WIDTH))


def _outproj_kernel(x_ref, oa_ref, or_ref, wa_ref, wr_ref, g_ref, wq_ref, x1_ref, qx_ref):
    x1 = x_ref[...] + _dot(oa_ref[...], wa_ref[...]) + _dot(or_ref[...], wr_ref[...])
    x1_ref[...] = x1
    qx_ref[...] = _dot(_rms(x1, g_ref[...]).astype(BF16), wq_ref[...]).astype(qx_ref.dtype)


def _outproj(x, oa, orr, w_out_bf16, gain, w_xq_bf16, tm, row0=0):
    t, d = oa.shape[0], x.shape[1]
    blk0 = row0 // tm
    full = lambda shape: pl.BlockSpec(shape, lambda i: (0, 0))
    return pl.pallas_call(
        _outproj_kernel,
        grid=(t // tm,),
        in_specs=[
            pl.BlockSpec((tm, d), lambda i: (i + blk0, 0)),
            pl.BlockSpec((tm, ATT_WIDTH), lambda i: (i, 0)),
            pl.BlockSpec((tm, RET_WIDTH), lambda i: (i, 0)),
            pl.BlockSpec((ATT_WIDTH, d), lambda i: (0, 0)),
            pl.BlockSpec((RET_WIDTH, d), lambda i: (1, 0)),
            full((1, d)),
            full((d, XATT_WIDTH)),
        ],
        out_specs=[pl.BlockSpec((tm, d), lambda i: (i, 0)), pl.BlockSpec((tm, XATT_WIDTH), lambda i: (i, 0))],
        out_shape=[jax.ShapeDtypeStruct((t, d), F32), jax.ShapeDtypeStruct((t, XATT_WIDTH), BF16)],
        compiler_params=_params("parallel"),
        name="out_proj",
    )(x, oa, orr, w_out_bf16, w_out_bf16, gain.reshape(1, d), w_xq_bf16)


def _xattn_kernel(qx_ref, k_ref, v_ref, x1_ref, wo_ref, g_ref, wpq_ref, x2_ref, pq_ref):
    heads = []
    for hh in range(XATT_HEADS):
        cols = slice(hh * XATT_HEAD_DIM, (hh + 1) * XATT_HEAD_DIM)
        s = _dot_nt(qx_ref[:, cols], k_ref[:, cols]) * (XATT_HEAD_DIM ** -0.5)
        p = jnp.exp(s - jnp.max(s, axis=-1, keepdims=True))
        p = p / jnp.sum(p, axis=-1, keepdims=True)
        heads.append(_dot(p.astype(BF16), v_ref[:, cols]))
    o = jnp.concatenate(heads, axis=1).astype(BF16)
    x2 = x1_ref[...] + _dot(o, wo_ref[...])
    x2_ref[...] = x2
    pq_ref[...] = _dot(_rms(x2, g_ref[...]).astype(BF16), wpq_ref[...])


def _xattn(qx, kv_mem, x1, w_xo_bf16, gain, w_pq_bf16, batch, seq, mem_len, tm, batch0=0):
    t, d = x1.shape
    npq = w_pq_bf16.shape[1]
    nt = seq // tm
    rows = lambda width: pl.BlockSpec((tm, width), lambda b, i: (b * nt + i, 0))
    full = lambda shape: pl.BlockSpec(shape, lambda b, i: (0, 0))
    return pl.pallas_call(
        _xattn_kernel,
        grid=(batch, nt),
        in_specs=[
            rows(XATT_WIDTH),
            pl.BlockSpec((mem_len, XATT_WIDTH), lambda b, i: (b + batch0, 0)),
            pl.BlockSpec((mem_len, XATT_WIDTH), lambda b, i: (b + batch0, 1)),
            rows(d),
            full((XATT_WIDTH, d)),
            full((1, d)),
            full((d, npq)),
        ],
        out_specs=[rows(d), rows(npq)],
        out_shape=[jax.ShapeDtypeStruct((t, d), F32), jax.ShapeDtypeStruct((t, npq), F32)],
        compiler_params=_params("parallel", "parallel"),
        name="cross_attention",
    )(qx, kv_mem, kv_mem, x1, w_xo_bf16, gain.reshape(1, d), w_pq_bf16)


def _topk_rows(s, k, payload=None):
    nrows = s.shape[0]
    rows = lax.broadcasted_iota(jnp.int32, s.shape, 0).astype(F32)
    vals, sel = [], []
    for _ in range(k):
        m = jnp.max(s, axis=0, keepdims=True)
        am = jnp.min(jnp.where(s == m, rows, float(nrows)), axis=0, keepdims=True)
        hit = rows == am
        vals.append(m)
        sel.append(am if payload is None else jnp.max(jnp.where(hit, payload, -1.0), axis=0, keepdims=True))
        s = jnp.where(hit, -jnp.inf, s)
    return jnp.concatenate(vals, axis=0), jnp.concatenate(sel, axis=0)


def _staircase(t1, t2):
    kk, half = PEER_TOPK, SUBLANES
    blocks = [(t1[0:1, :], t2, kk), (t1[1:2, :], t2[0:half, :], half)]
    blocks += [(t1[a:a + 1, :], t2[0:half, :], kk // (a + 1)) for a in range(2, half)]
    blocks.append((t1[half:kk, :], t2[0:1, :], half))
    return blocks


def _peer_topk_kernel(q_ref, keys_ref, ids_ref, gates_ref, *, heads):
    width = 2 * PEER_HALF_DIM
    row8 = lax.broadcasted_iota(jnp.int32, (SUBLANES, q_ref.shape[0]), 0)
    for hh in range(heads):
        tops = []
        for p in range(2):
            c0 = hh * width + p * PEER_HALF_DIM
            qp = q_ref[:, c0:c0 + PEER_HALF_DIM].astype(BF16)
            tops.append(_topk_rows(_dot_nt(keys_ref[hh, p], qp), PEER_TOPK))
        (s1, i1), (s2, i2) = tops
        cand_s, cand_i = [], []
        for (a_s, b_s, live), (a_i, b_i, _) in zip(_staircase(s1, s2), _staircase(i1 * float(PEER_N_KEYS), i2)):
            blk = a_s + b_s
            if live < blk.shape[0]:
                blk = jnp.where(row8 < live, blk, -jnp.inf)
            cand_s.append(blk)
            cand_i.append(a_i + b_i)
        top_s, top_e = _topk_rows(jnp.concatenate(cand_s, axis=0), PEER_TOPK,
                                  payload=jnp.concatenate(cand_i, axis=0))
        e = jnp.exp(top_s - top_s[0:1, :])
        ids_ref[hh] = top_e.astype(jnp.int32)
        gates_ref[hh] = e / jnp.sum(e, axis=0, keepdims=True)


def _peer_topk(pq, keys_bf16, tt, heads):
    t = pq.shape[0]
    width = 2 * PEER_HALF_DIM
    out = pl.BlockSpec((heads, PEER_TOPK, tt), lambda i, hh: (hh, 0, i))
    return pl.pallas_call(
        functools.partial(_peer_topk_kernel, heads=heads),
        grid=(t // tt, PEER_HEADS // heads),
        in_specs=[
            pl.BlockSpec((tt, heads * width), lambda i, hh: (i, hh)),
            pl.BlockSpec((heads, 2, PEER_N_KEYS, PEER_HALF_DIM), lambda i, hh: (hh, 0, 0, 0)),
        ],
        out_specs=[out, out],
        out_shape=[jax.ShapeDtypeStruct((PEER_HEADS, PEER_TOPK, t), jnp.int32),
                   jax.ShapeDtypeStruct((PEER_HEADS, PEER_TOPK, t), F32)],
        compiler_params=_params("parallel", "parallel"),
        name="peer_topk",
    )(pq, keys_bf16)


GATHER_TOKENS = 128
GATHER_BUFS = 8
GATHER_GROUP = 8
assert GATHER_GROUP % GATHER_BUFS == 0 and GATHER_GROUP % SUBLANES == 0 and GATHER_TOKENS % GATHER_GROUP == 0


def _peer_gather_kernel(ids_hbm, tbl_hbm, x2_ref, gates_ref, fg_ref, og_ref, *rest, final_norm, first_block,
                        has_prev):
    out_ref, ids_smem, ids_sem, *scratch = rest[1:] if has_prev else rest
    gbufs = scratch[:GATHER_BUFS]
    gsem, xn_ref, y_ref, yacc_ref = scratch[GATHER_BUFS:]
    i = pl.program_id(0)
    has_next = i + 1 < pl.num_programs(0)
    slot = i % 2
    other = 1 - slot
    tb = GATHER_TOKENS
    nchunk = D_MODEL // LANES
    nids = tb * PEER_SLOTS
    ahead = GATHER_BUFS - 1
    group = GATHER_GROUP

    def ids_copy(step, sl):
        return pltpu.make_async_copy(ids_hbm.at[pl.ds(pl.multiple_of((first_block + step) * nids, nids), nids)],
                                     ids_smem.at[pl.ds(pl.multiple_of(sl * nids, nids), nids)], ids_sem.at[sl])

    def row_copy(e, b, k):
        return pltpu.make_async_copy(tbl_hbm.at[e], gbufs[b].at[pl.ds(k, 1)], gsem.at[b])

    def issue(sl, tok, b):
        base = sl * nids + tok * PEER_SLOTS
        for k in range(PEER_SLOTS):
            row_copy(ids_smem[base + k], b, k).start(priority=k % 2)

    def wait_rows(b):
        for k in range(PEER_SLOTS):
            row_copy(0, b, k).wait()

    @pl.when(i == 0)
    def _():
        ids_copy(0, 0).start()
        ids_copy(0, 0).wait()
        for tok in range(ahead):
            issue(0, tok, tok)

    @pl.when(has_next)
    def _():
        ids_copy(i + 1, other).start()

    xn_ref[...] = _rms(x2_ref[...], fg_ref[...])

    lane = lax.broadcasted_iota(jnp.int32, (PEER_SLOTS, LANES), 1)
    hi_mask = jnp.uint32(0xFFFF0000)

    def compute(tok, r, b):
        gbuf = gbufs[b]
        xrow = xn_ref[pl.ds(tok, 1), :]
        acc = jnp.zeros((PEER_SLOTS, LANES), F32)
        for c in range(nchunk):
            w = gbuf[:, c * LANES:(c + 1) * LANES]
            u = lax.bitcast_convert_type(w << 16, F32)
            acc = acc + u * xrow[:, c * LANES:(c + 1) * LANES]
        a = jnp.sum(acc, axis=-1, keepdims=True)
        gate = jnp.sum(jnp.where(lane == tok, gates_ref[...], 0.0), axis=-1, keepdims=True)
        coeff = 0.5 * a * (1.0 + lax.erf(a * (2.0 ** -0.5))) * gate
        for c in range(nchunk):
            w = gbuf[:, c * LANES:(c + 1) * LANES]
            vv = lax.bitcast_convert_type(w & hi_mask, F32)
            yacc_ref[r:r + 1, c * LANES:(c + 1) * LANES] = jnp.sum(vv * coeff, axis=0, keepdims=True)

    def run_group(g, last):
        for r in range(group):
            b = r % GATHER_BUFS
            nb = (r + ahead) % GATHER_BUFS
            tok = g * group + r
            wait_rows(b)
            if not last or r + ahead < group:
                issue(slot, tok + ahead, nb)
            else:
                if r + ahead == group:
                    @pl.when(has_next)
                    def _():
                        ids_copy(i + 1, other).wait()

                @pl.when(has_next)
                def _():
                    issue(other, r + ahead - group, nb)
            compute(tok, r, b)
        y_ref[pl.ds(pl.multiple_of(g * group, group), group), :] = yacc_ref[...]

    def body(g, carry):
        run_group(g, last=False)
        return carry

    ngroups = tb // group
    lax.fori_loop(0, ngroups - 1, body, 0)
    run_group(ngroups - 1, last=True)
    x3 = x2_ref[...] + y_ref[...]
    out_ref[...] = _rms(x3, og_ref[...]) if final_norm else x3


def _peer_gather(ids_flat, table, x2, gates_t, ffn_gain, final_gain, final_norm, first_block, out_rows,
                 out_block0, prev):
    t, d = x2.shape
    tb = GATHER_TOKENS
    nsteps = t // tb - first_block
    full = lambda shape: pl.BlockSpec(shape, lambda i: (0, 0))
    has_prev = prev is not None
    return pl.pallas_call(
        functools.partial(_peer_gather_kernel, final_norm=final_norm, first_block=first_block, has_prev=has_prev),
        grid=(nsteps,),
        in_specs=[
            pl.BlockSpec(memory_space=pl.ANY),
            pl.BlockSpec(memory_space=pl.ANY),
            pl.BlockSpec((tb, d), lambda i: (i + first_block, 0)),
            pl.BlockSpec((PEER_SLOTS, tb), lambda i: (0, i + first_block)),
            full((1, d)),
            full((1, d)),
        ] + ([pl.BlockSpec(memory_space=pl.ANY)] if has_prev else []),
        out_specs=pl.BlockSpec((tb, d), lambda i: (i + first_block + out_block0, 0)),
        out_shape=jax.ShapeDtypeStruct((out_rows, d), F32),
        input_output_aliases={6: 0} if has_prev else {},
        scratch_shapes=[
            pltpu.SMEM((2 * tb * PEER_SLOTS,), jnp.int32),
            pltpu.SemaphoreType.DMA((2,)),
            *[pltpu.VMEM((PEER_SLOTS, d), jnp.uint32) for _ in range(GATHER_BUFS)],
            pltpu.SemaphoreType.DMA((GATHER_BUFS,)),
            pltpu.VMEM((tb, d), F32),
            pltpu.VMEM((tb, d), F32),
            pltpu.VMEM((GATHER_GROUP, d), F32),
        ],
        compiler_params=_params("arbitrary"),
        name="peer_gather",
    )(ids_flat, table, x2, gates_t, ffn_gain.reshape(1, d), final_gain.reshape(1, d), *([prev] if has_prev else []))


SC_CORES, SC_SUBCORES, SC_LANES = 2, 16, 16
SC_WORKERS = SC_CORES * SC_SUBCORES
SC_ROWS = SC_LANES
SC_COLS = 8
SC_DOT_ROWS = 8
SC_GROUPS, SC_GROUP_DENOM = (1, 1, 2, 4), 8
SC_SHARE = (63, 128)


def _sc_erf(x):
    ax = jnp.abs(x)
    t = 1.0 / (1.0 + 0.3275911 * ax)
    poly = t * (0.254829592 + t * (-0.284496736 + t * (1.421413741 + t * (-1.453152027 + t * 1.061405429))))
    y = 1.0 - poly * jnp.exp(-ax * ax)
    return jnp.where(x < 0, -y, y)


def _peer_sc(ids_flat, gates_tok, xn, table, ts):
    d = D_MODEL
    lanes, rows = SC_LANES, SC_ROWS
    nchunk = PEER_SLOTS // rows
    block = lanes * SC_COLS
    per_worker = ts // SC_WORKERS
    assert ts % (2 * SC_WORKERS) == 0 and d % block == 0 and nchunk % 2 == 0
    mesh = plsc.VectorSubcoreMesh(core_axis_name="c", subcore_axis_name="s")

    @functools.partial(
        pl.kernel, mesh=mesh,
        out_type=jax.ShapeDtypeStruct((ts, d), F32),
        scratch_types=[
            pltpu.VMEM((2, PEER_SLOTS), jnp.int32),
            pltpu.VMEM((2, PEER_SLOTS), F32),
            pltpu.VMEM((2, d), F32),
            pltpu.VMEM((d,), F32),
            pltpu.VMEM((2, rows, 1, d), jnp.uint32),
            pltpu.VMEM((rows, lanes), F32),
            pltpu.VMEM((lanes,), F32),
            pltpu.VMEM((rows, lanes), jnp.int32),
            pltpu.SemaphoreType.DMA((2,)),
            pltpu.SemaphoreType.DMA((2,)),
        ],
        compiler_params=pltpu.CompilerParams(needs_layout_passes=False),
        name="peer_sparsecore",
    )
    def run(ids_hbm, gates_hbm, xn_hbm, tbl_hbm, y_hbm, ids_v, gates_v, x_v, y_v, rows_v, acc_v, coeff_v, ridx_v,
            row_sem, in_sem):
        worker = lax.axis_index("s") * SC_CORES + lax.axis_index("c")
        first = worker * per_worker
        lane = lax.iota(jnp.int32, lanes)
        zero = jnp.zeros((lanes,), F32)
        hi_mask = jnp.full((lanes,), 0xFFFF0000, jnp.uint32)
        for r in range(rows):
            ridx_v[r, :] = jnp.full((lanes,), r, jnp.int32)

        def gather(par, c, slot):
            idx = ids_v[par, pl.ds(pl.multiple_of(c * rows, rows), rows)]
            return pltpu.make_async_copy(tbl_hbm.at[idx], rows_v.at[slot], row_sem.at[slot])

        def token_inputs(tok, par):
            return (pltpu.make_async_copy(ids_hbm.at[pl.ds(tok * PEER_SLOTS, PEER_SLOTS)], ids_v.at[par], in_sem.at[par]),
                    pltpu.make_async_copy(gates_hbm.at[pl.ds(tok * PEER_SLOTS, PEER_SLOTS)], gates_v.at[par],
                                          in_sem.at[par]),
                    pltpu.make_async_copy(xn_hbm.at[tok], x_v.at[par], in_sem.at[par]))

        def chunk(par, c, slot):
            for r0 in range(0, rows, SC_DOT_ROWS):
                def dot_body(j, accs, r0=r0):
                    xj = x_v[par, pl.ds(j * lanes, lanes)]
                    return tuple(
                        accs[q] + plsc.bitcast(rows_v[slot, r0 + q, 0, pl.ds(j * lanes, lanes)] << 16, F32) * xj
                        for q in range(SC_DOT_ROWS))
                accs = lax.fori_loop(0, d // lanes, dot_body, (zero,) * SC_DOT_ROWS, unroll=4)
                for q in range(SC_DOT_ROWS):
                    acc_v[r0 + q, :] = accs[q]
            a = zero
            for col in range(lanes):
                a = a + plsc.load_gather(acc_v, [lane, ridx_v[col, :]])
            gate = gates_v[par, pl.ds(pl.multiple_of(c * rows, rows), rows)]
            coeff_v[...] = 0.5 * a * (1.0 + _sc_erf(a * (2.0 ** -0.5))) * gate
            coeffs = [plsc.load_gather(coeff_v, [ridx_v[r, :]]) for r in range(rows)]

            @pl.loop(0, d // block)
            def _(jb):
                base = jb * block
                accs = [zero] * SC_COLS
                for r in range(rows):
                    for q in range(SC_COLS):
                        w = rows_v[slot, r, 0, pl.ds(base + q * lanes, lanes)]
                        accs[q] = accs[q] + plsc.bitcast(w & hi_mask, F32) * coeffs[r]
                for q in range(SC_COLS):
                    plsc.addupdate(y_v.at[pl.ds(base + q * lanes, lanes)], accs[q])

        for cp in token_inputs(first, 0):
            cp.start()

        @pl.loop(0, per_worker // 2)
        def _(pair):
            for par in range(2):
                i = pair * 2 + par
                tok = first + i
                for cp in token_inputs(tok, par):
                    cp.wait()

                @pl.when(i + 1 < per_worker)
                def _():
                    for cp in token_inputs(tok + 1, 1 - par):
                        cp.start()

                @pl.loop(0, d // lanes)
                def _(j):
                    y_v[pl.ds(j * lanes, lanes)] = zero

                gather(par, 0, 0).start()

                @pl.loop(0, nchunk // 2)
                def _(cpair):
                    for slot in range(2):
                        c = cpair * 2 + slot
                        if slot == 0:
                            gather(par, c + 1, 1).start()
                        else:
                            @pl.when(c + 1 < nchunk)
                            def _():
                                gather(par, c + 1, 0).start()
                        gather(par, c, slot).wait()
                        chunk(par, c, slot)

                pltpu.sync_copy(y_v, y_hbm.at[tok])

    return run(ids_flat, gates_tok, xn, table)


def _rms_rows_kernel(x_ref, g_ref, o_ref):
    o_ref[...] = _rms(x_ref[...], g_ref[...])


def _rms_rows(x, gain, rows, tm):
    d = x.shape[1]
    return pl.pallas_call(
        _rms_rows_kernel,
        grid=(rows // tm,),
        in_specs=[pl.BlockSpec((tm, d), lambda i: (i, 0)), pl.BlockSpec((1, d), lambda i: (0, 0))],
        out_specs=pl.BlockSpec((tm, d), lambda i: (i, 0)),
        out_shape=jax.ShapeDtypeStruct((rows, d), F32),
        compiler_params=_params("parallel"),
        name="rms_rows",
    )(x, gain.reshape(1, d))


def _peer_finish_kernel(x2_ref, y_ref, og_ref, prev_ref, out_ref, *, final_norm):
    del prev_ref
    x3 = x2_ref[...] + y_ref[...]
    out_ref[...] = _rms(x3, og_ref[...]) if final_norm else x3


def _peer_finish(x2, y_head, out_rest, final_gain, final_norm, out_block0):
    rows, d = y_head.shape
    tm = GATHER_TOKENS
    blk = pl.BlockSpec((tm, d), lambda i: (i, 0))
    return pl.pallas_call(
        functools.partial(_peer_finish_kernel, final_norm=final_norm),
        grid=(rows // tm,),
        in_specs=[blk, blk, pl.BlockSpec((1, d), lambda i: (0, 0)), pl.BlockSpec(memory_space=pl.ANY)],
        out_specs=pl.BlockSpec((tm, d), lambda i: (i + out_block0, 0)),
        out_shape=jax.ShapeDtypeStruct(out_rest.shape, F32),
        input_output_aliases={3: 0},
        compiler_params=_params("parallel"),
        name="peer_finish",
    )(x2, y_head, final_gain.reshape(1, d), out_rest)


def _rope_angles(positions, dh):
    inv_freq = ROPE_THETA ** (-jnp.arange(0, dh, 2, dtype=F32) / dh)
    ang = positions.astype(F32).reshape(-1, 1) * inv_freq
    return jnp.cos(ang), jnp.sin(ang)


def _rope_tables_att(positions):
    cos, sin = _rope_angles(positions, HEAD_DIM)
    copies = LANES // HEAD_DIM
    return (jnp.tile(jnp.concatenate([cos, cos], axis=1), (1, copies)),
            jnp.tile(jnp.concatenate([-sin, sin], axis=1), (1, copies)))


def _pack_table_kernel(u_ref, v_ref, o_ref):
    ub = lax.bitcast_convert_type(u_ref[...].astype(BF16).astype(F32), jnp.uint32)
    vb = lax.bitcast_convert_type(v_ref[...].astype(BF16).astype(F32), jnp.uint32)
    o_ref[:, 0, :] = (ub >> 16) | vb


def _pack_expert_table(u, v, rows=PACK_ROWS):
    n, d = u.shape
    blk = pl.BlockSpec((rows, d), lambda i: (i, 0))
    return pl.pallas_call(
        _pack_table_kernel,
        grid=(n // rows,),
        in_specs=[blk, blk],
        out_specs=pl.BlockSpec((rows, 1, d), lambda i: (i, 0, 0)),
        out_shape=jax.ShapeDtypeStruct((n, 1, d), jnp.uint32),
        compiler_params=_params("parallel"),
        name="pack_table",
    )(u, v)


def kernel(x, mem, positions, mix_norm_gain, w_in, att_sinks, att_out_gain, ret_out_gain, w_out,
           cross_norm_gain, mem_norm_gain, w_xq, w_xk, w_xv, w_xo, ffn_norm_gain,
           w_peer_q, peer_sub_keys, peer_u, peer_v, final_norm_gain):
    batch, seq, d = x.shape
    mem_len = mem.shape[1]
    t = batch * seq
    depth = w_in.shape[0]
    assert d == D_MODEL and seq % BLOCK == 0 and t % GATHER_TOKENS == 0

    cos_a, sin_a = _rope_tables_att(positions)
    cos_r, sin_r = _rope_angles(positions, RET_HEAD_DIM)
    xf = x.reshape(t, d)
    memf = mem.reshape(batch * mem_len, d)
    nblocks = t // GATHER_TOKENS
    split = batch % SC_GROUP_DENOM == 0 and nblocks % SC_SHARE[1] == 0
    groups = [batch * g // SC_GROUP_DENOM for g in SC_GROUPS] if split else [batch]
    sc_total = nblocks * SC_SHARE[0] // SC_SHARE[1] if split else 0
    tm_mid = min(ROW_TM, seq)
    for l in range(depth):
        sc_left = sc_total
        w_in_b = _reorder_w_in(w_in[l]).astype(BF16)
        w_out_b, w_xq_b, w_xo_b, w_pq_b = (w.astype(BF16) for w in (w_out[l], w_xq[l], w_xo[l], w_peer_q[l]))
        keys_b = peer_sub_keys[l].astype(BF16)
        w_kv = jnp.concatenate([w_xk[l], w_xv[l]], axis=1).astype(BF16)
        kv_mem = _norm_matmul(memf, mem_norm_gain[l], w_kv, mem_len, 2 * XATT_WIDTH, BF16)
        table = _pack_expert_table(peer_u[l], peer_v[l])
        last = l == depth - 1
        out = None
        pending = []
        b0 = 0
        for gi, bp in enumerate(groups):
            r0, tp = b0 * seq, bp * seq
            block0, blocks_p = r0 // GATHER_TOKENS, tp // GATHER_TOKENS
            rows = slice(r0, r0 + tp)
            h = _norm_matmul(xf, mix_norm_gain[l], w_in_b, min(IN_PROJ_TM, tp), IN_PROJ_TN, BF16, row0=r0, rows=tp)
            oa = _swa(h, cos_a[rows], sin_a[rows], att_sinks[l], att_out_gain[l], bp, seq)
            orr = _retention(h, cos_r[rows], sin_r[rows], ret_out_gain[l], bp, seq)
            x1, qx = _outproj(xf, oa, orr, w_out_b, cross_norm_gain[l], w_xq_b, tm_mid, row0=r0)
            x2, pq = _xattn(qx, kv_mem, x1, w_xo_b, ffn_norm_gain[l], w_pq_b, bp, seq, mem_len, tm_mid, batch0=b0)
            ids_t, gates_t = _peer_topk(pq, keys_b, LANES, PEER_HEADS)
            ids = ids_t.reshape(PEER_SLOTS, tp).T.reshape(tp * PEER_SLOTS)
            gates_t = gates_t.reshape(PEER_SLOTS, tp)
            sc_blocks = min(sc_left, blocks_p)
            sc_left -= sc_blocks
            if sc_blocks:
                ts = sc_blocks * GATHER_TOKENS
                xn_head = _rms_rows(x2, ffn_norm_gain[l], ts, GATHER_TOKENS)
                y_head = _peer_sc(ids, gates_t[:, :ts].T.reshape(ts * PEER_SLOTS), xn_head, table, ts)
                pending.append((gi, x2, y_head, block0))
            if sc_blocks < blocks_p:
                out = _peer_gather(ids, table, x2, gates_t, ffn_norm_gain[l], final_norm_gain, last, sc_blocks,
                                   t, block0, out)
                while pending and pending[0][0] <= gi - 2:
                    _, x2_sc, y_sc, blk = pending.pop(0)
                    out = _peer_finish(x2_sc, y_sc, out, final_norm_gain, last, blk)
            b0 += bp
        for _, x2_sc, y_sc, blk in pending:
            out = _peer_finish(x2_sc, y_sc, out, final_norm_gain, last, blk)
        xf = out
    return xf.reshape(batch, seq, d)
```

```python
import functools

import jax
import jax.numpy as jnp
from jax import lax
from jax.experimental import pallas as pl
from jax.experimental.pallas import tpu as pltpu
from jax.experimental.pallas import tpu_sc as plsc

F32 = jnp.float32
BF16 = jnp.bfloat16

EPS = 1e-6
GN_EPS = 1e-5
ROPE_THETA = 10000.0
D_MODEL = 2048
HEAD_DIM = 64
ATT_Q_HEADS = 16
ATT_KV_HEADS = 2
ATT_WIDTH = ATT_Q_HEADS * HEAD_DIM
ATT_KV_WIDTH = ATT_KV_HEADS * HEAD_DIM
BLOCK = 128
RET_HEADS = 4
RET_HEAD_DIM = 256
RET_WIDTH = RET_HEADS * RET_HEAD_DIM
RET_CHUNK = 128
IN_COLS = ATT_WIDTH + 2 * ATT_KV_WIDTH + 4 * RET_WIDTH
XATT_HEADS = 4
XATT_HEAD_DIM = 128
XATT_WIDTH = XATT_HEADS * XATT_HEAD_DIM
PEER_HEADS = 8
PEER_N_KEYS = 128
PEER_HALF_DIM = 128
PEER_TOPK = 16
PEER_SLOTS = PEER_HEADS * PEER_TOPK

LANES = 128
SUBLANES = 8
VMEM_LIMIT = 56 * 1024 * 1024
IN_PROJ_TM, IN_PROJ_TN = 1024, 1792
ROW_TM = 512
PACK_ROWS = 256

_REF_QR0 = ATT_WIDTH + 2 * ATT_KV_WIDTH
_QR0 = 0
_KR0 = _QR0 + RET_WIDTH
_VR0 = _KR0 + RET_WIDTH
_GR0 = _VR0 + RET_WIDTH
_QA0 = _GR0 + RET_WIDTH
_KA0 = _QA0 + ATT_WIDTH
_VA0 = _KA0 + ATT_KV_WIDTH


def _reorder_w_in(w):
    return jnp.concatenate([w[:, _REF_QR0:], w[:, :_REF_QR0]], axis=1)


def _params(*sem):
    return pltpu.CompilerParams(dimension_semantics=sem, vmem_limit_bytes=VMEM_LIMIT)


def _rms(xf, gain):
    ms = jnp.mean(xf * xf, axis=-1, keepdims=True)
    return xf * lax.rsqrt(ms + EPS) * gain


def _dot(a, b):
    return jnp.dot(a, b, preferred_element_type=F32)


def _dot_nt(a, b):
    return lax.dot_general(a, b, (((1,), (1,)), ((), ())), preferred_element_type=F32)


def _norm_matmul_kernel(x_ref, g_ref, w_ref, o_ref, xn_ref):
    @pl.when(pl.program_id(1) == 0)
    def _():
        xn_ref[...] = _rms(x_ref[...], g_ref[...]).astype(BF16)

    o_ref[...] = _dot(xn_ref[...], w_ref[...]).astype(o_ref.dtype)


def _norm_matmul(x, gain, w_bf16, tm, tn, out_dtype, row0=0, rows=None):
    d = x.shape[1]
    t = x.shape[0] if rows is None else rows
    n = w_bf16.shape[1]
    blk0 = row0 // tm
    return pl.pallas_call(
        _norm_matmul_kernel,
        grid=(t // tm, n // tn),
        in_specs=[
            pl.BlockSpec((tm, d), lambda i, j: (i + blk0, 0)),
            pl.BlockSpec((1, d), lambda i, j: (0, 0)),
            pl.BlockSpec((d, tn), lambda i, j: (0, j)),
        ],
        out_specs=pl.BlockSpec((tm, tn), lambda i, j: (i, j)),
        out_shape=jax.ShapeDtypeStruct((t, n), out_dtype),
        scratch_shapes=[pltpu.VMEM((tm, d), BF16)],
        compiler_params=_params("parallel", "arbitrary"),
        name="norm_matmul",
    )(x, gain.reshape(1, d), w_bf16)


def _swa_kernel(sink_ref, q_ref, kc_ref, kp_ref, vc_ref, vp_ref, cc_ref, sc_ref, cp_ref, sp_ref,
                gain_ref, o_ref):
    n = pl.program_id(1)
    lane = lax.broadcasted_iota(jnp.int32, (1, LANES), 1)
    first_half = (lane % HEAD_DIM) < (HEAD_DIM // 2)
    lo = lane < HEAD_DIM

    def rope(x, c, s):
        partner = jnp.where(first_half, pltpu.roll(x, LANES - HEAD_DIM // 2, 1),
                            pltpu.roll(x, HEAD_DIM // 2, 1))
        return x * c + partner * s

    cc, sc = cc_ref[...], sc_ref[...]
    k = jnp.concatenate([rope(kp_ref[...].astype(F32), cp_ref[...], sp_ref[...]),
                         rope(kc_ref[...].astype(F32), cc, sc)], axis=0)
    v = jnp.concatenate([vp_ref[...].astype(F32), vc_ref[...].astype(F32)], axis=0)
    k_sw = pltpu.roll(k, HEAD_DIM, 1)
    v_sw = pltpu.roll(v, HEAD_DIM, 1)

    def place(a, a_sw, c, half):
        src = a if c == half else a_sw
        keep = lo if half == 0 else jnp.logical_not(lo)
        return jnp.where(keep, src, 0.0).astype(BF16)

    kvar = {(c, h): place(k, k_sw, c, h) for c in range(ATT_KV_HEADS) for h in range(2)}
    vvar = {(c, h): place(v, v_sw, c, h) for c in range(ATT_KV_HEADS) for h in range(2)}

    qi = lax.broadcasted_iota(jnp.int32, (BLOCK, 2 * BLOCK), 0) + BLOCK
    ki = lax.broadcasted_iota(jnp.int32, (BLOCK, 2 * BLOCK), 1)
    dist = qi - ki
    kmin = jnp.where(n > 0, 0, BLOCK)
    valid = (dist >= 0) & (dist < BLOCK) & (ki >= kmin)

    group = ATT_Q_HEADS // ATT_KV_HEADS
    for j in range(ATT_WIDTH // LANES):
        c = (2 * j) // group
        cols = slice(j * LANES, (j + 1) * LANES)
        qg = (rope(q_ref[:, cols].astype(F32), cc, sc) * (HEAD_DIM ** -0.5)).astype(BF16)
        out = jnp.zeros((BLOCK, LANES), F32)
        for half in range(2):
            s = jnp.where(valid, _dot_nt(qg, kvar[(c, half)]), -jnp.inf)
            sink = sink_ref[2 * j + half]
            m = jnp.maximum(jnp.max(s, axis=-1, keepdims=True), sink)
            p = jnp.exp(s - m)
            denom = jnp.sum(p, axis=-1, keepdims=True) + jnp.exp(sink - m)
            out = out + _dot((p / denom).astype(BF16), vvar[(c, half)])
        sq = out * out
        ss_lo = jnp.sum(jnp.where(lo, sq, 0.0), axis=-1, keepdims=True)
        ss_hi = jnp.sum(jnp.where(lo, 0.0, sq), axis=-1, keepdims=True)
        ms = jnp.where(lo, ss_lo, ss_hi) * (1.0 / HEAD_DIM)
        o_ref[:, cols] = (out * lax.rsqrt(ms + EPS) * gain_ref[:, cols]).astype(o_ref.dtype)


def _swa(h, cos_a, sin_a, sinks, gain, batch, seq):
    nb = seq // BLOCK
    t = batch * seq
    kcol, vcol = _KA0 // LANES, _VA0 // LANES
    cur = lambda b, n: b * nb + n
    prev = lambda b, n: b * nb + jnp.maximum(n - 1, 0)
    row = lambda col, f: pl.BlockSpec((BLOCK, LANES), lambda b, n: (f(b, n), col))
    return pl.pallas_call(
        _swa_kernel,
        grid=(batch, nb),
        in_specs=[
            pl.BlockSpec(memory_space=pltpu.SMEM),
            pl.BlockSpec((BLOCK, ATT_WIDTH), lambda b, n: (cur(b, n), _QA0 // ATT_WIDTH)),
            row(kcol, cur), row(kcol, prev), row(vcol, cur), row(vcol, prev),
            row(0, cur), row(0, cur), row(0, prev), row(0, prev),
            pl.BlockSpec((1, ATT_WIDTH), lambda b, n: (0, 0)),
        ],
        out_specs=pl.BlockSpec((BLOCK, ATT_WIDTH), lambda b, n: (cur(b, n), 0)),
        out_shape=jax.ShapeDtypeStruct((t, ATT_WIDTH), BF16),
        compiler_params=_params("parallel", "parallel"),
        name="swa_attention",
    )(sinks, h, h, h, h, h, cos_a, sin_a, cos_a, sin_a, gain.reshape(1, ATT_WIDTH))


def _ret_kernel(lg_ref, cd_ref, q_ref, k_ref, v_ref, g_ref, c_ref, s_ref, gain_ref, o_ref, state_ref):
    n = pl.program_id(1)

    @pl.when(n == 0)
    def _():
        state_ref[...] = jnp.zeros_like(state_ref)

    c, s = c_ref[...], s_ref[...]
    hd = RET_HEAD_DIM
    half = hd // 2

    def rope(x):
        x1, x2 = x[:, :half], x[:, half:]
        return jnp.concatenate([x1 * c - x2 * s, x2 * c + x1 * s], axis=1)

    ri = lax.broadcasted_iota(jnp.int32, (RET_CHUNK, RET_CHUNK), 0).astype(F32)
    ci = lax.broadcasted_iota(jnp.int32, (RET_CHUNK, RET_CHUNK), 1).astype(F32)
    diff = ri - ci
    for hh in range(RET_HEADS):
        cols = slice(hh * hd, (hh + 1) * hd)
        lg = lg_ref[hh]
        q = rope(q_ref[:, cols].astype(F32))
        k = rope(k_ref[:, cols].astype(F32)) * (hd ** -0.5)
        v = v_ref[:, cols].astype(BF16)

        decay = jnp.where(diff >= 0, jnp.exp(jnp.maximum(diff, 0.0) * lg), 0.0)
        zeta = jnp.exp((RET_CHUNK - 1 - ri) * lg)
        xi = jnp.exp((ri + 1.0) * lg)
        zeta2 = jnp.concatenate([zeta, zeta], axis=1)
        xi2 = jnp.concatenate([xi, xi], axis=1)

        inner = _dot_nt(q.astype(BF16), k.astype(BF16)) * decay
        state = state_ref[hh]
        out = _dot(inner.astype(BF16), v) + _dot((q * xi2).astype(BF16), state.astype(BF16))
        kv = lax.dot_general((k * zeta2).astype(BF16), v, (((0,), (0,)), ((), ())), preferred_element_type=F32)
        state_ref[hh] = state * cd_ref[hh] + kv

        mu = jnp.mean(out, axis=-1, keepdims=True)
        cen = out - mu
        var = jnp.mean(cen * cen, axis=-1, keepdims=True)
        g = g_ref[:, cols].astype(F32)
        o = cen * lax.rsqrt(var + GN_EPS) * gain_ref[:, cols] * (g * jax.nn.sigmoid(g))
        o_ref[:, cols] = o.astype(o_ref.dtype)


def _retention(h, cos_r, sin_r, gain, batch, seq):
    nc = seq // RET_CHUNK
    t = batch * seq
    hd = RET_HEAD_DIM
    log_gamma = jnp.log1p(-jnp.exp2(-5.0 - jnp.arange(RET_HEADS, dtype=F32)))
    chunk_decay = jnp.exp(RET_CHUNK * log_gamma)
    col = lambda c0: pl.BlockSpec((RET_CHUNK, RET_WIDTH), lambda b, n: (b * nc + n, c0 // RET_WIDTH))
    tab = pl.BlockSpec((RET_CHUNK, hd // 2), lambda b, n: (b * nc + n, 0))
    return pl.pallas_call(
        _ret_kernel,
        grid=(batch, nc),
        in_specs=[
            pl.BlockSpec(memory_space=pltpu.SMEM),
            pl.BlockSpec(memory_space=pltpu.SMEM),
            col(_QR0), col(_KR0), col(_VR0), col(_GR0), tab, tab,
            pl.BlockSpec((1, RET_WIDTH), lambda b, n: (0, 0)),
        ],
        out_specs=pl.BlockSpec((RET_CHUNK, RET_WIDTH), lambda b, n: (b * nc + n, 0)),
        out_shape=jax.ShapeDtypeStruct((t, RET_WIDTH), BF16),
        scratch_shapes=[pltpu.VMEM((RET_HEADS, hd, hd), F32)],
        compiler_params=_params("parallel", "arbitrary"),
        name="retention",
    )(log_gamma, chunk_decay, h, h, h, h, cos_r, sin_r, gain.reshape(1, RET_WIDTH))


def _outproj_kernel(x_ref, oa_ref, or_ref, wa_ref, wr_ref, g_ref, wq_ref, x1_ref, qx_ref):
    x1 = x_ref[...] + _dot(oa_ref[...], wa_ref[...]) + _dot(or_ref[...], wr_ref[...])
    x1_ref[...] = x1
    qx_ref[...] = _dot(_rms(x1, g_ref[...]).astype(BF16), wq_ref[...]).astype(qx_ref.dtype)


def _outproj(x, oa, orr, w_out_bf16, gain, w_xq_bf16, tm, row0=0):
    t, d = oa.shape[0], x.shape[1]
    blk0 = row0 // tm
    full = lambda shape: pl.BlockSpec(shape, lambda i: (0, 0))
    return pl.pallas_call(
        _outproj_kernel,
        grid=(t // tm,),
        in_specs=[
            pl.BlockSpec((tm, d), lambda i: (i + blk0, 0)),
            pl.BlockSpec((tm, ATT_WIDTH), lambda i: (i, 0)),
            pl.BlockSpec((tm, RET_WIDTH), lambda i: (i, 0)),
            pl.BlockSpec((ATT_WIDTH, d), lambda i: (0, 0)),
            pl.BlockSpec((RET_WIDTH, d), lambda i: (1, 0)),
            full((1, d)),
            full((d, XATT_WIDTH)),
        ],
        out_specs=[pl.BlockSpec((tm, d), lambda i: (i, 0)), pl.BlockSpec((tm, XATT_WIDTH), lambda i: (i, 0))],
        out_shape=[jax.ShapeDtypeStruct((t, d), F32), jax.ShapeDtypeStruct((t, XATT_WIDTH), BF16)],
        compiler_params=_params("parallel"),
        name="out_proj",
    )(x, oa, orr, w_out_bf16, w_out_bf16, gain.reshape(1, d), w_xq_bf16)


def _xattn_kernel(qx_ref, k_ref, v_ref, x1_ref, wo_ref, g_ref, wpq_ref, x2_ref, pq_ref):
    heads = []
    for hh in range(XATT_HEADS):
        cols = slice(hh * XATT_HEAD_DIM, (hh + 1) * XATT_HEAD_DIM)
        s = _dot_nt(qx_ref[:, cols], k_ref[:, cols]) * (XATT_HEAD_DIM ** -0.5)
        p = jnp.exp(s - jnp.max(s, axis=-1, keepdims=True))
        p = p / jnp.sum(p, axis=-1, keepdims=True)
        heads.append(_dot(p.astype(BF16), v_ref[:, cols]))
    o = jnp.concatenate(heads, axis=1).astype(BF16)
    x2 = x1_ref[...] + _dot(o, wo_ref[...])
    x2_ref[...] = x2
    pq_ref[...] = _dot(_rms(x2, g_ref[...]).astype(BF16), wpq_ref[...])


def _xattn(qx, kv_mem, x1, w_xo_bf16, gain, w_pq_bf16, batch, seq, mem_len, tm, batch0=0):
    t, d = x1.shape
    npq = w_pq_bf16.shape[1]
    nt = seq // tm
    rows = lambda width: pl.BlockSpec((tm, width), lambda b, i: (b * nt + i, 0))
    full = lambda shape: pl.BlockSpec(shape, lambda b, i: (0, 0))
    return pl.pallas_call(
        _xattn_kernel,
        grid=(batch, nt),
        in_specs=[
            rows(XATT_WIDTH),
            pl.BlockSpec((mem_len, XATT_WIDTH), lambda b, i: (b + batch0, 0)),
            pl.BlockSpec((mem_len, XATT_WIDTH), lambda b, i: (b + batch0, 1)),
            rows(d),
            full((XATT_WIDTH, d)),
            full((1, d)),
            full((d, npq)),
        ],
        out_specs=[rows(d), rows(npq)],
        out_shape=[jax.ShapeDtypeStruct((t, d), F32), jax.ShapeDtypeStruct((t, npq), F32)],
        compiler_params=_params("parallel", "parallel"),
        name="cross_attention",
    )(qx, kv_mem, kv_mem, x1, w_xo_bf16, gain.reshape(1, d), w_pq_bf16)


def _topk_rows(s, k, payload=None):
    nrows = s.shape[0]
    rows = lax.broadcasted_iota(jnp.int32, s.shape, 0).astype(F32)
    vals, sel = [], []
    for _ in range(k):
        m = jnp.max(s, axis=0, keepdims=True)
        am = jnp.min(jnp.where(s == m, rows, float(nrows)), axis=0, keepdims=True)
        hit = rows == am
        vals.append(m)
        sel.append(am if payload is None else jnp.max(jnp.where(hit, payload, -1.0), axis=0, keepdims=True))
        s = jnp.where(hit, -jnp.inf, s)
    return jnp.concatenate(vals, axis=0), jnp.concatenate(sel, axis=0)


def _staircase(t1, t2):
    kk, half = PEER_TOPK, SUBLANES
    blocks = [(t1[0:1, :], t2, kk), (t1[1:2, :], t2[0:half, :], half)]
    blocks += [(t1[a:a + 1, :], t2[0:half, :], kk // (a + 1)) for a in range(2, half)]
    blocks.append((t1[half:kk, :], t2[0:1, :], half))
    return blocks


def _peer_topk_kernel(q_ref, keys_ref, ids_ref, gates_ref, *, heads):
    width = 2 * PEER_HALF_DIM
    row8 = lax.broadcasted_iota(jnp.int32, (SUBLANES, q_ref.shape[0]), 0)
    for hh in range(heads):
        tops = []
        for p in range(2):
            c0 = hh * width + p * PEER_HALF_DIM
            qp = q_ref[:, c0:c0 + PEER_HALF_DIM].astype(BF16)
            tops.append(_topk_rows(_dot_nt(keys_ref[hh, p], qp), PEER_TOPK))
        (s1, i1), (s2, i2) = tops
        cand_s, cand_i = [], []
        for (a_s, b_s, live), (a_i, b_i, _) in zip(_staircase(s1, s2), _staircase(i1 * float(PEER_N_KEYS), i2)):
            blk = a_s + b_s
            if live < blk.shape[0]:
                blk = jnp.where(row8 < live, blk, -jnp.inf)
            cand_s.append(blk)
            cand_i.append(a_i + b_i)
        top_s, top_e = _topk_rows(jnp.concatenate(cand_s, axis=0), PEER_TOPK,
                                  payload=jnp.concatenate(cand_i, axis=0))
        e = jnp.exp(top_s - top_s[0:1, :])
        ids_ref[hh] = top_e.astype(jnp.int32)
        gates_ref[hh] = e / jnp.sum(e, axis=0, keepdims=True)


def _peer_topk(pq, keys_bf16, tt, heads):
    t = pq.shape[0]
    width = 2 * PEER_HALF_DIM
    out = pl.BlockSpec((heads, PEER_TOPK, tt), lambda i, hh: (hh, 0, i))
    return pl.pallas_call(
        functools.partial(_peer_topk_kernel, heads=heads),
        grid=(t // tt, PEER_HEADS // heads),
        in_specs=[
            pl.BlockSpec((tt, heads * width), lambda i, hh: (i, hh)),
            pl.BlockSpec((heads, 2, PEER_N_KEYS, PEER_HALF_DIM), lambda i, hh: (hh, 0, 0, 0)),
        ],
        out_specs=[out, out],
        out_shape=[jax.ShapeDtypeStruct((PEER_HEADS, PEER_TOPK, t), jnp.int32),
                   jax.ShapeDtypeStruct((PEER_HEADS, PEER_TOPK, t), F32)],
        compiler_params=_params("parallel", "parallel"),
        name="peer_topk",
    )(pq, keys_bf16)


GATHER_TOKENS = 128
GATHER_BUFS = 8
GATHER_GROUP = 8
assert GATHER_GROUP % GATHER_BUFS == 0 and GATHER_GROUP % SUBLANES == 0 and GATHER_TOKENS % GATHER_GROUP == 0


def _peer_gather_kernel(ids_hbm, tbl_hbm, x2_ref, gates_ref, fg_ref, og_ref, *rest, final_norm, first_block,
                        has_prev):
    out_ref, ids_smem, ids_sem, *scratch = rest[1:] if has_prev else rest
    gbufs = scratch[:GATHER_BUFS]
    gsem, xn_ref, y_ref, yacc_ref = scratch[GATHER_BUFS:]
    i = pl.program_id(0)
    has_next = i + 1 < pl.num_programs(0)
    slot = i % 2
    other = 1 - slot
    tb = GATHER_TOKENS
    nchunk = D_MODEL // LANES
    nids = tb * PEER_SLOTS
    ahead = GATHER_BUFS - 1
    group = GATHER_GROUP

    def ids_copy(step, sl):
        return pltpu.make_async_copy(ids_hbm.at[pl.ds(pl.multiple_of((first_block + step) * nids, nids), nids)],
                                     ids_smem.at[pl.ds(pl.multiple_of(sl * nids, nids), nids)], ids_sem.at[sl])

    def row_copy(e, b, k):
        return pltpu.make_async_copy(tbl_hbm.at[e], gbufs[b].at[pl.ds(k, 1)], gsem.at[b])

    def issue(sl, tok, b):
        base = sl * nids + tok * PEER_SLOTS
        for k in range(PEER_SLOTS):
            row_copy(ids_smem[base + k], b, k).start(priority=k % 2)

    def wait_rows(b):
        for k in range(PEER_SLOTS):
            row_copy(0, b, k).wait()

    @pl.when(i == 0)
    def _():
        ids_copy(0, 0).start()
        ids_copy(0, 0).wait()
        for tok in range(ahead):
            issue(0, tok, tok)

    @pl.when(has_next)
    def _():
        ids_copy(i + 1, other).start()

    xn_ref[...] = _rms(x2_ref[...], fg_ref[...])

    lane = lax.broadcasted_iota(jnp.int32, (PEER_SLOTS, LANES), 1)
    hi_mask = jnp.uint32(0xFFFF0000)

    def compute(tok, r, b):
        gbuf = gbufs[b]
        xrow = xn_ref[pl.ds(tok, 1), :]
        acc = jnp.zeros((PEER_SLOTS, LANES), F32)
        for c in range(nchunk):
            w = gbuf[:, c * LANES:(c + 1) * LANES]
            u = lax.bitcast_convert_type(w << 16, F32)
            acc = acc + u * xrow[:, c * LANES:(c + 1) * LANES]
        a = jnp.sum(acc, axis=-1, keepdims=True)
        gate = jnp.sum(jnp.where(lane == tok, gates_ref[...], 0.0), axis=-1, keepdims=True)
        coeff = 0.5 * a * (1.0 + lax.erf(a * (2.0 ** -0.5))) * gate
        for c in range(nchunk):
            w = gbuf[:, c * LANES:(c + 1) * LANES]
            vv = lax.bitcast_convert_type(w & hi_mask, F32)
            yacc_ref[r:r + 1, c * LANES:(c + 1) * LANES] = jnp.sum(vv * coeff, axis=0, keepdims=True)

    def run_group(g, last):
        for r in range(group):
            b = r % GATHER_BUFS
            nb = (r + ahead) % GATHER_BUFS
            tok = g * group + r
            wait_rows(b)
            if not last or r + ahead < group:
                issue(slot, tok + ahead, nb)
            else:
                if r + ahead == group:
                    @pl.when(has_next)
                    def _():
                        ids_copy(i + 1, other).wait()

                @pl.when(has_next)
                def _():
                    issue(other, r + ahead - group, nb)
            compute(tok, r, b)
        y_ref[pl.ds(pl.multiple_of(g * group, group), group), :] = yacc_ref[...]

    def body(g, carry):
        run_group(g, last=False)
        return carry

    ngroups = tb // group
    lax.fori_loop(0, ngroups - 1, body, 0)
    run_group(ngroups - 1, last=True)
    x3 = x2_ref[...] + y_ref[...]
    out_ref[...] = _rms(x3, og_ref[...]) if final_norm else x3


def _peer_gather(ids_flat, table, x2, gates_t, ffn_gain, final_gain, final_norm, first_block, out_rows,
                 out_block0, prev):
    t, d = x2.shape
    tb = GATHER_TOKENS
    nsteps = t // tb - first_block
    full = lambda shape: pl.BlockSpec(shape, lambda i: (0, 0))
    has_prev = prev is not None
    return pl.pallas_call(
        functools.partial(_peer_gather_kernel, final_norm=final_norm, first_block=first_block, has_prev=has_prev),
        grid=(nsteps,),
        in_specs=[
            pl.BlockSpec(memory_space=pl.ANY),
            pl.BlockSpec(memory_space=pl.ANY),
            pl.BlockSpec((tb, d), lambda i: (i + first_block, 0)),
            pl.BlockSpec((PEER_SLOTS, tb), lambda i: (0, i + first_block)),
            full((1, d)),
            full((1, d)),
        ] + ([pl.BlockSpec(memory_space=pl.ANY)] if has_prev else []),
        out_specs=pl.BlockSpec((tb, d), lambda i: (i + first_block + out_block0, 0)),
        out_shape=jax.ShapeDtypeStruct((out_rows, d), F32),
        input_output_aliases={6: 0} if has_prev else {},
        scratch_shapes=[
            pltpu.SMEM((2 * tb * PEER_SLOTS,), jnp.int32),
            pltpu.SemaphoreType.DMA((2,)),
            *[pltpu.VMEM((PEER_SLOTS, d), jnp.uint32) for _ in range(GATHER_BUFS)],
            pltpu.SemaphoreType.DMA((GATHER_BUFS,)),
            pltpu.VMEM((tb, d), F32),
            pltpu.VMEM((tb, d), F32),
            pltpu.VMEM((GATHER_GROUP, d), F32),
        ],
        compiler_params=_params("arbitrary"),
        name="peer_gather",
    )(ids_flat, table, x2, gates_t, ffn_gain.reshape(1, d), final_gain.reshape(1, d), *([prev] if has_prev else []))


SC_CORES, SC_SUBCORES, SC_LANES = 2, 16, 16
SC_WORKERS = SC_CORES * SC_SUBCORES
SC_ROWS = SC_LANES
SC_COLS = 8
SC_DOT_ROWS = 8
SC_GROUPS, SC_GROUP_DENOM = (1, 1, 2, 4), 8
SC_SHARE = (63, 128)


def _sc_erf(x):
    ax = jnp.abs(x)
    t = 1.0 / (1.0 + 0.3275911 * ax)
    poly = t * (0.254829592 + t * (-0.284496736 + t * (1.421413741 + t * (-1.453152027 + t * 1.061405429))))
    y = 1.0 - poly * jnp.exp(-ax * ax)
    return jnp.where(x < 0, -y, y)


def _peer_sc(ids_flat, gates_tok, xn, table, ts):
    d = D_MODEL
    lanes, rows = SC_LANES, SC_ROWS
    nchunk = PEER_SLOTS // rows
    block = lanes * SC_COLS
    per_worker = ts // SC_WORKERS
    assert ts % (2 * SC_WORKERS) == 0 and d % block == 0 and nchunk % 2 == 0
    mesh = plsc.VectorSubcoreMesh(core_axis_name="c", subcore_axis_name="s")

    @functools.partial(
        pl.kernel, mesh=mesh,
        out_type=jax.ShapeDtypeStruct((ts, d), F32),
        scratch_types=[
            pltpu.VMEM((2, PEER_SLOTS), jnp.int32),
            pltpu.VMEM((2, PEER_SLOTS), F32),
            pltpu.VMEM((2, d), F32),
            pltpu.VMEM((d,), F32),
            pltpu.VMEM((2, rows, 1, d), jnp.uint32),
            pltpu.VMEM((rows, lanes), F32),
            pltpu.VMEM((lanes,), F32),
            pltpu.VMEM((rows, lanes), jnp.int32),
            pltpu.SemaphoreType.DMA((2,)),
            pltpu.SemaphoreType.DMA((2,)),
        ],
        compiler_params=pltpu.CompilerParams(needs_layout_passes=False),
        name="peer_sparsecore",
    )
    def run(ids_hbm, gates_hbm, xn_hbm, tbl_hbm, y_hbm, ids_v, gates_v, x_v, y_v, rows_v, acc_v, coeff_v, ridx_v,
            row_sem, in_sem):
        worker = lax.axis_index("s") * SC_CORES + lax.axis_index("c")
        first = worker * per_worker
        lane = lax.iota(jnp.int32, lanes)
        zero = jnp.zeros((lanes,), F32)
        hi_mask = jnp.full((lanes,), 0xFFFF0000, jnp.uint32)
        for r in range(rows):
            ridx_v[r, :] = jnp.full((lanes,), r, jnp.int32)

        def gather(par, c, slot):
            idx = ids_v[par, pl.ds(pl.multiple_of(c * rows, rows), rows)]
            return pltpu.make_async_copy(tbl_hbm.at[idx], rows_v.at[slot], row_sem.at[slot])

        def token_inputs(tok, par):
            return (pltpu.make_async_copy(ids_hbm.at[pl.ds(tok * PEER_SLOTS, PEER_SLOTS)], ids_v.at[par], in_sem.at[par]),
                    pltpu.make_async_copy(gates_hbm.at[pl.ds(tok * PEER_SLOTS, PEER_SLOTS)], gates_v.at[par],
                                          in_sem.at[par]),
                    pltpu.make_async_copy(xn_hbm.at[tok], x_v.at[par], in_sem.at[par]))

        def chunk(par, c, slot):
            for r0 in range(0, rows, SC_DOT_ROWS):
                def dot_body(j, accs, r0=r0):
                    xj = x_v[par, pl.ds(j * lanes, lanes)]
                    return tuple(
                        accs[q] + plsc.bitcast(rows_v[slot, r0 + q, 0, pl.ds(j * lanes, lanes)] << 16, F32) * xj
                        for q in range(SC_DOT_ROWS))
                accs = lax.fori_loop(0, d // lanes, dot_body, (zero,) * SC_DOT_ROWS, unroll=4)
                for q in range(SC_DOT_ROWS):
                    acc_v[r0 + q, :] = accs[q]
            a = zero
            for col in range(lanes):
                a = a + plsc.load_gather(acc_v, [lane, ridx_v[col, :]])
            gate = gates_v[par, pl.ds(pl.multiple_of(c * rows, rows), rows)]
            coeff_v[...] = 0.5 * a * (1.0 + _sc_erf(a * (2.0 ** -0.5))) * gate
            coeffs = [plsc.load_gather(coeff_v, [ridx_v[r, :]]) for r in range(rows)]

            @pl.loop(0, d // block)
            def _(jb):
                base = jb * block
                accs = [zero] * SC_COLS
                for r in range(rows):
                    for q in range(SC_COLS):
                        w = rows_v[slot, r, 0, pl.ds(base + q * lanes, lanes)]
                        accs[q] = accs[q] + plsc.bitcast(w & hi_mask, F32) * coeffs[r]
                for q in range(SC_COLS):
                    plsc.addupdate(y_v.at[pl.ds(base + q * lanes, lanes)], accs[q])

        for cp in token_inputs(first, 0):
            cp.start()

        @pl.loop(0, per_worker // 2)
        def _(pair):
            for par in range(2):
                i = pair * 2 + par
                tok = first + i
                for cp in token_inputs(tok, par):
                    cp.wait()

                @pl.when(i + 1 < per_worker)
                def _():
                    for cp in token_inputs(tok + 1, 1 - par):
                        cp.start()

                @pl.loop(0, d // lanes)
                def _(j):
                    y_v[pl.ds(j * lanes, lanes)] = zero

                gather(par, 0, 0).start()

                @pl.loop(0, nchunk // 2)
                def _(cpair):
                    for slot in range(2):
                        c = cpair * 2 + slot
                        if slot == 0:
                            gather(par, c + 1, 1).start()
                        else:
                            @pl.when(c + 1 < nchunk)
                            def _():
                                gather(par, c + 1, 0).start()
                        gather(par, c, slot).wait()
                        chunk(par, c, slot)

                pltpu.sync_copy(y_v, y_hbm.at[tok])

    return run(ids_flat, gates_tok, xn, table)


def _rms_rows_kernel(x_ref, g_ref, o_ref):
    o_ref[...] = _rms(x_ref[...], g_ref[...])


def _rms_rows(x, gain, rows, tm):
    d = x.shape[1]
    return pl.pallas_call(
        _rms_rows_kernel,
        grid=(rows // tm,),
        in_specs=[pl.BlockSpec((tm, d), lambda i: (i, 0)), pl.BlockSpec((1, d), lambda i: (0, 0))],
        out_specs=pl.BlockSpec((tm, d), lambda i: (i, 0)),
        out_shape=jax.ShapeDtypeStruct((rows, d), F32),
        compiler_params=_params("parallel"),
        name="rms_rows",
    )(x, gain.reshape(1, d))


def _peer_finish_kernel(x2_ref, y_ref, og_ref, prev_ref, out_ref, *, final_norm):
    del prev_ref
    x3 = x2_ref[...] + y_ref[...]
    out_ref[...] = _rms(x3, og_ref[...]) if final_norm else x3


def _peer_finish(x2, y_head, out_rest, final_gain, final_norm, out_block0):
    rows, d = y_head.shape
    tm = GATHER_TOKENS
    blk = pl.BlockSpec((tm, d), lambda i: (i, 0))
    return pl.pallas_call(
        functools.partial(_peer_finish_kernel, final_norm=final_norm),
        grid=(rows // tm,),
        in_specs=[blk, blk, pl.BlockSpec((1, d), lambda i: (0, 0)), pl.BlockSpec(memory_space=pl.ANY)],
        out_specs=pl.BlockSpec((tm, d), lambda i: (i + out_block0, 0)),
        out_shape=jax.ShapeDtypeStruct(out_rest.shape, F32),
        input_output_aliases={3: 0},
        compiler_params=_params("parallel"),
        name="peer_finish",
    )(x2, y_head, final_gain.reshape(1, d), out_rest)


def _rope_angles(positions, dh):
    inv_freq = ROPE_THETA ** (-jnp.arange(0, dh, 2, dtype=F32) / dh)
    ang = positions.astype(F32).reshape(-1, 1) * inv_freq
    return jnp.cos(ang), jnp.sin(ang)


def _rope_tables_att(positions):
    cos, sin = _rope_angles(positions, HEAD_DIM)
    copies = LANES // HEAD_DIM
    return (jnp.tile(jnp.concatenate([cos, cos], axis=1), (1, copies)),
            jnp.tile(jnp.concatenate([-sin, sin], axis=1), (1, copies)))


def _pack_table_kernel(u_ref, v_ref, o_ref):
    ub = lax.bitcast_convert_type(u_ref[...].astype(BF16).astype(F32), jnp.uint32)
    vb = lax.bitcast_convert_type(v_ref[...].astype(BF16).astype(F32), jnp.uint32)
    o_ref[:, 0, :] = (ub >> 16) | vb


def _pack_expert_table(u, v, rows=PACK_ROWS):
    n, d = u.shape
    blk = pl.BlockSpec((rows, d), lambda i: (i, 0))
    return pl.pallas_call(
        _pack_table_kernel,
        grid=(n // rows,),
        in_specs=[blk, blk],
        out_specs=pl.BlockSpec((rows, 1, d), lambda i: (i, 0, 0)),
        out_shape=jax.ShapeDtypeStruct((n, 1, d), jnp.uint32),
        compiler_params=_params("parallel"),
        name="pack_table",
    )(u, v)


def kernel(x, mem, positions, mix_norm_gain, w_in, att_sinks, att_out_gain, ret_out_gain, w_out,
           cross_norm_gain, mem_norm_gain, w_xq, w_xk, w_xv, w_xo, ffn_norm_gain,
           w_peer_q, peer_sub_keys, peer_u, peer_v, final_norm_gain):
    batch, seq, d = x.shape
    mem_len = mem.shape[1]
    t = batch * seq
    depth = w_in.shape[0]
    assert d == D_MODEL and seq % BLOCK == 0 and t % GATHER_TOKENS == 0

    cos_a, sin_a = _rope_tables_att(positions)
    cos_r, sin_r = _rope_angles(positions, RET_HEAD_DIM)
    xf = x.reshape(t, d)
    memf = mem.reshape(batch * mem_len, d)
    nblocks = t // GATHER_TOKENS
    split = batch % SC_GROUP_DENOM == 0 and nblocks % SC_SHARE[1] == 0
    groups = [batch * g // SC_GROUP_DENOM for g in SC_GROUPS] if split else [batch]
    sc_total = nblocks * SC_SHARE[0] // SC_SHARE[1] if split else 0
    tm_mid = min(ROW_TM, seq)
    for l in range(depth):
        sc_left = sc_total
        w_in_b = _reorder_w_in(w_in[l]).astype(BF16)
        w_out_b, w_xq_b, w_xo_b, w_pq_b = (w.astype(BF16) for w in (w_out[l], w_xq[l], w_xo[l], w_peer_q[l]))
        keys_b = peer_sub_keys[l].astype(BF16)
        w_kv = jnp.concatenate([w_xk[l], w_xv[l]], axis=1).astype(BF16)
        kv_mem = _norm_matmul(memf, mem_norm_gain[l], w_kv, mem_len, 2 * XATT_WIDTH, BF16)
        table = _pack_expert_table(peer_u[l], peer_v[l])
        last = l == depth - 1
        out = None
        pending = []
        b0 = 0
        for bp in groups:
            r0, tp = b0 * seq, bp * seq
            block0, blocks_p = r0 // GATHER_TOKENS, tp // GATHER_TOKENS
            rows = slice(r0, r0 + tp)
            h = _norm_matmul(xf, mix_norm_gain[l], w_in_b, min(IN_PROJ_TM, tp), IN_PROJ_TN, BF16, row0=r0, rows=tp)
            oa = _swa(h, cos_a[rows], sin_a[rows], att_sinks[l], att_out_gain[l], bp, seq)
            orr = _retention(h, cos_r[rows], sin_r[rows], ret_out_gain[l], bp, seq)
            x1, qx = _outproj(xf, oa, orr, w_out_b, cross_norm_gain[l], w_xq_b, tm_mid, row0=r0)
            x2, pq = _xattn(qx, kv_mem, x1, w_xo_b, ffn_norm_gain[l], w_pq_b, bp, seq, mem_len, tm_mid, batch0=b0)
            ids_t, gates_t = _peer_topk(pq, keys_b, LANES, PEER_HEADS)
            ids = ids_t.reshape(PEER_SLOTS, tp).T.reshape(tp * PEER_SLOTS)
            gates_t = gates_t.reshape(PEER_SLOTS, tp)
            sc_blocks = min(sc_left, blocks_p)
            sc_left -= sc_blocks
            if sc_blocks:
                ts = sc_blocks * GATHER_TOKENS
                xn_head = _rms_rows(x2, ffn_norm_gain[l], ts, GATHER_TOKENS)
                y_head = _peer_sc(ids, gates_t[:, :ts].T.reshape(ts * PEER_SLOTS), xn_head, table, ts)
                pending.append((x2, y_head, block0))
            if sc_blocks < blocks_p:
                out = _peer_gather(ids, table, x2, gates_t, ffn_norm_gain[l], final_norm_gain, last, sc_blocks,
                                   t, block0, out)
            b0 += bp
        for x2, y_head, block0 in pending:
            out = _peer_finish(x2, y_head, out, final_norm_gain, last, block0)
        xf = out
    return xf.reshape(batch, seq, d)
```

```python
import functools

import jax
import jax.numpy as jnp
from jax import lax
from jax.experimental import pallas as pl
from jax.experimental.pallas import tpu as pltpu
from jax.experimental.pallas import tpu_sc as plsc

F32 = jnp.float32
BF16 = jnp.bfloat16

EPS = 1e-6
GN_EPS = 1e-5
ROPE_THETA = 10000.0
D_MODEL = 2048
HEAD_DIM = 64
ATT_Q_HEADS = 16
ATT_KV_HEADS = 2
ATT_WIDTH = ATT_Q_HEADS * HEAD_DIM
ATT_KV_WIDTH = ATT_KV_HEADS * HEAD_DIM
BLOCK = 128
RET_HEADS = 4
RET_HEAD_DIM = 256
RET_WIDTH = RET_HEADS * RET_HEAD_DIM
RET_CHUNK = 128
IN_COLS = ATT_WIDTH + 2 * ATT_KV_WIDTH + 4 * RET_WIDTH
XATT_HEADS = 4
XATT_HEAD_DIM = 128
XATT_WIDTH = XATT_HEADS * XATT_HEAD_DIM
PEER_HEADS = 8
PEER_N_KEYS = 128
PEER_HALF_DIM = 128
PEER_TOPK = 16
PEER_SLOTS = PEER_HEADS * PEER_TOPK

LANES = 128
SUBLANES = 8
VMEM_LIMIT = 56 * 1024 * 1024
IN_PROJ_TM, IN_PROJ_TN = 1024, 1792
ROW_TM = 512
PACK_ROWS = 256

_REF_QR0 = ATT_WIDTH + 2 * ATT_KV_WIDTH
_QR0 = 0
_KR0 = _QR0 + RET_WIDTH
_VR0 = _KR0 + RET_WIDTH
_GR0 = _VR0 + RET_WIDTH
_QA0 = _GR0 + RET_WIDTH
_KA0 = _QA0 + ATT_WIDTH
_VA0 = _KA0 + ATT_KV_WIDTH


def _reorder_w_in(w):
    return jnp.concatenate([w[:, _REF_QR0:], w[:, :_REF_QR0]], axis=1)


def _params(*sem):
    return pltpu.CompilerParams(dimension_semantics=sem, vmem_limit_bytes=VMEM_LIMIT)


def _rms(xf, gain):
    ms = jnp.mean(xf * xf, axis=-1, keepdims=True)
    return xf * lax.rsqrt(ms + EPS) * gain


def _dot(a, b):
    return jnp.dot(a, b, preferred_element_type=F32)


def _dot_nt(a, b):
    return lax.dot_general(a, b, (((1,), (1,)), ((), ())), preferred_element_type=F32)


def _norm_matmul_kernel(x_ref, g_ref, w_ref, o_ref, xn_ref):
    @pl.when(pl.program_id(1) == 0)
    def _():
        xn_ref[...] = _rms(x_ref[...], g_ref[...]).astype(BF16)

    o_ref[...] = _dot(xn_ref[...], w_ref[...]).astype(o_ref.dtype)


def _norm_matmul(x, gain, w_bf16, tm, tn, out_dtype, row0=0, rows=None):
    d = x.shape[1]
    t = x.shape[0] if rows is None else rows
    n = w_bf16.shape[1]
    blk0 = row0 // tm
    return pl.pallas_call(
        _norm_matmul_kernel,
        grid=(t // tm, n // tn),
        in_specs=[
            pl.BlockSpec((tm, d), lambda i, j: (i + blk0, 0)),
            pl.BlockSpec((1, d), lambda i, j: (0, 0)),
            pl.BlockSpec((d, tn), lambda i, j: (0, j)),
        ],
        out_specs=pl.BlockSpec((tm, tn), lambda i, j: (i, j)),
        out_shape=jax.ShapeDtypeStruct((t, n), out_dtype),
        scratch_shapes=[pltpu.VMEM((tm, d), BF16)],
        compiler_params=_params("parallel", "arbitrary"),
        name="norm_matmul",
    )(x, gain.reshape(1, d), w_bf16)


def _swa_kernel(sink_ref, q_ref, kc_ref, kp_ref, vc_ref, vp_ref, cc_ref, sc_ref, cp_ref, sp_ref,
                gain_ref, o_ref):
    n = pl.program_id(1)
    lane = lax.broadcasted_iota(jnp.int32, (1, LANES), 1)
    first_half = (lane % HEAD_DIM) < (HEAD_DIM // 2)
    lo = lane < HEAD_DIM

    def rope(x, c, s):
        partner = jnp.where(first_half, pltpu.roll(x, LANES - HEAD_DIM // 2, 1),
                            pltpu.roll(x, HEAD_DIM // 2, 1))
        return x * c + partner * s

    cc, sc = cc_ref[...], sc_ref[...]
    k = jnp.concatenate([rope(kp_ref[...].astype(F32), cp_ref[...], sp_ref[...]),
                         rope(kc_ref[...].astype(F32), cc, sc)], axis=0)
    v = jnp.concatenate([vp_ref[...].astype(F32), vc_ref[...].astype(F32)], axis=0)
    k_sw = pltpu.roll(k, HEAD_DIM, 1)
    v_sw = pltpu.roll(v, HEAD_DIM, 1)

    def place(a, a_sw, c, half):
        src = a if c == half else a_sw
        keep = lo if half == 0 else jnp.logical_not(lo)
        return jnp.where(keep, src, 0.0).astype(BF16)

    kvar = {(c, h): place(k, k_sw, c, h) for c in range(ATT_KV_HEADS) for h in range(2)}
    vvar = {(c, h): place(v, v_sw, c, h) for c in range(ATT_KV_HEADS) for h in range(2)}

    qi = lax.broadcasted_iota(jnp.int32, (BLOCK, 2 * BLOCK), 0) + BLOCK
    ki = lax.broadcasted_iota(jnp.int32, (BLOCK, 2 * BLOCK), 1)
    dist = qi - ki
    kmin = jnp.where(n > 0, 0, BLOCK)
    valid = (dist >= 0) & (dist < BLOCK) & (ki >= kmin)

    group = ATT_Q_HEADS // ATT_KV_HEADS
    for j in range(ATT_WIDTH // LANES):
        c = (2 * j) // group
        cols = slice(j * LANES, (j + 1) * LANES)
        qg = (rope(q_ref[:, cols].astype(F32), cc, sc) * (HEAD_DIM ** -0.5)).astype(BF16)
        out = jnp.zeros((BLOCK, LANES), F32)
        for half in range(2):
            s = jnp.where(valid, _dot_nt(qg, kvar[(c, half)]), -jnp.inf)
            sink = sink_ref[2 * j + half]
            m = jnp.maximum(jnp.max(s, axis=-1, keepdims=True), sink)
            p = jnp.exp(s - m)
            denom = jnp.sum(p, axis=-1, keepdims=True) + jnp.exp(sink - m)
            out = out + _dot((p / denom).astype(BF16), vvar[(c, half)])
        sq = out * out
        ss_lo = jnp.sum(jnp.where(lo, sq, 0.0), axis=-1, keepdims=True)
        ss_hi = jnp.sum(jnp.where(lo, 0.0, sq), axis=-1, keepdims=True)
        ms = jnp.where(lo, ss_lo, ss_hi) * (1.0 / HEAD_DIM)
        o_ref[:, cols] = (out * lax.rsqrt(ms + EPS) * gain_ref[:, cols]).astype(o_ref.dtype)


def _swa(h, cos_a, sin_a, sinks, gain, batch, seq):
    nb = seq // BLOCK
    t = batch * seq
    kcol, vcol = _KA0 // LANES, _VA0 // LANES
    cur = lambda b, n: b * nb + n
    prev = lambda b, n: b * nb + jnp.maximum(n - 1, 0)
    row = lambda col, f: pl.BlockSpec((BLOCK, LANES), lambda b, n: (f(b, n), col))
    return pl.pallas_call(
        _swa_kernel,
        grid=(batch, nb),
        in_specs=[
            pl.BlockSpec(memory_space=pltpu.SMEM),
            pl.BlockSpec((BLOCK, ATT_WIDTH), lambda b, n: (cur(b, n), _QA0 // ATT_WIDTH)),
            row(kcol, cur), row(kcol, prev), row(vcol, cur), row(vcol, prev),
            row(0, cur), row(0, cur), row(0, prev), row(0, prev),
            pl.BlockSpec((1, ATT_WIDTH), lambda b, n: (0, 0)),
        ],
        out_specs=pl.BlockSpec((BLOCK, ATT_WIDTH), lambda b, n: (cur(b, n), 0)),
        out_shape=jax.ShapeDtypeStruct((t, ATT_WIDTH), BF16),
        compiler_params=_params("parallel", "parallel"),
        name="swa_attention",
    )(sinks, h, h, h, h, h, cos_a, sin_a, cos_a, sin_a, gain.reshape(1, ATT_WIDTH))


def _ret_kernel(lg_ref, cd_ref, q_ref, k_ref, v_ref, g_ref, c_ref, s_ref, gain_ref, o_ref, state_ref):
    n = pl.program_id(1)

    @pl.when(n == 0)
    def _():
        state_ref[...] = jnp.zeros_like(state_ref)

    c, s = c_ref[...], s_ref[...]
    hd = RET_HEAD_DIM
    half = hd // 2

    def rope(x):
        x1, x2 = x[:, :half], x[:, half:]
        return jnp.concatenate([x1 * c - x2 * s, x2 * c + x1 * s], axis=1)

    ri = lax.broadcasted_iota(jnp.int32, (RET_CHUNK, RET_CHUNK), 0).astype(F32)
    ci = lax.broadcasted_iota(jnp.int32, (RET_CHUNK, RET_CHUNK), 1).astype(F32)
    diff = ri - ci
    for hh in range(RET_HEADS):
        cols = slice(hh * hd, (hh + 1) * hd)
        lg = lg_ref[hh]
        q = rope(q_ref[:, cols].astype(F32))
        k = rope(k_ref[:, cols].astype(F32)) * (hd ** -0.5)
        v = v_ref[:, cols].astype(BF16)

        decay = jnp.where(diff >= 0, jnp.exp(jnp.maximum(diff, 0.0) * lg), 0.0)
        zeta = jnp.exp((RET_CHUNK - 1 - ri) * lg)
        xi = jnp.exp((ri + 1.0) * lg)
        zeta2 = jnp.concatenate([zeta, zeta], axis=1)
        xi2 = jnp.concatenate([xi, xi], axis=1)

        inner = _dot_nt(q.astype(BF16), k.astype(BF16)) * decay
        state = state_ref[hh]
        out = _dot(inner.astype(BF16), v) + _dot((q * xi2).astype(BF16), state.astype(BF16))
        kv = lax.dot_general((k * zeta2).astype(BF16), v, (((0,), (0,)), ((), ())), preferred_element_type=F32)
        state_ref[hh] = state * cd_ref[hh] + kv

        mu = jnp.mean(out, axis=-1, keepdims=True)
        cen = out - mu
        var = jnp.mean(cen * cen, axis=-1, keepdims=True)
        g = g_ref[:, cols].astype(F32)
        o = cen * lax.rsqrt(var + GN_EPS) * gain_ref[:, cols] * (g * jax.nn.sigmoid(g))
        o_ref[:, cols] = o.astype(o_ref.dtype)


def _retention(h, cos_r, sin_r, gain, batch, seq):
    nc = seq // RET_CHUNK
    t = batch * seq
    hd = RET_HEAD_DIM
    log_gamma = jnp.log1p(-jnp.exp2(-5.0 - jnp.arange(RET_HEADS, dtype=F32)))
    chunk_decay = jnp.exp(RET_CHUNK * log_gamma)
    col = lambda c0: pl.BlockSpec((RET_CHUNK, RET_WIDTH), lambda b, n: (b * nc + n, c0 // RET_WIDTH))
    tab = pl.BlockSpec((RET_CHUNK, hd // 2), lambda b, n: (b * nc + n, 0))
    return pl.pallas_call(
        _ret_kernel,
        grid=(batch, nc),
        in_specs=[
            pl.BlockSpec(memory_space=pltpu.SMEM),
            pl.BlockSpec(memory_space=pltpu.SMEM),
            col(_QR0), col(_KR0), col(_VR0), col(_GR0), tab, tab,
            pl.BlockSpec((1, RET_WIDTH), lambda b, n: (0, 0)),
        ],
        out_specs=pl.BlockSpec((RET_CHUNK, RET_WIDTH), lambda b, n: (b * nc + n, 0)),
        out_shape=jax.ShapeDtypeStruct((t, RET_WIDTH), BF16),
        scratch_shapes=[pltpu.VMEM((RET_HEADS, hd, hd), F32)],
        compiler_params=_params("parallel", "arbitrary"),
        name="retention",
    )(log_gamma, chunk_decay, h, h, h, h, cos_r, sin_r, gain.reshape(1, RET_WIDTH))


def _outproj_kernel(x_ref, oa_ref, or_ref, wa_ref, wr_ref, g_ref, wq_ref, x1_ref, qx_ref):
    x1 = x_ref[...] + _dot(oa_ref[...], wa_ref[...]) + _dot(or_ref[...], wr_ref[...])
    x1_ref[...] = x1
    qx_ref[...] = _dot(_rms(x1, g_ref[...]).astype(BF16), wq_ref[...]).astype(qx_ref.dtype)


def _outproj(x, oa, orr, w_out_bf16, gain, w_xq_bf16, tm, row0=0):
    t, d = oa.shape[0], x.shape[1]
    blk0 = row0 // tm
    full = lambda shape: pl.BlockSpec(shape, lambda i: (0, 0))
    return pl.pallas_call(
        _outproj_kernel,
        grid=(t // tm,),
        in_specs=[
            pl.BlockSpec((tm, d), lambda i: (i + blk0, 0)),
            pl.BlockSpec((tm, ATT_WIDTH), lambda i: (i, 0)),
            pl.BlockSpec((tm, RET_WIDTH), lambda i: (i, 0)),
            pl.BlockSpec((ATT_WIDTH, d), lambda i: (0, 0)),
            pl.BlockSpec((RET_WIDTH, d), lambda i: (1, 0)),
            full((1, d)),
            full((d, XATT_WIDTH)),
        ],
        out_specs=[pl.BlockSpec((tm, d), lambda i: (i, 0)), pl.BlockSpec((tm, XATT_WIDTH), lambda i: (i, 0))],
        out_shape=[jax.ShapeDtypeStruct((t, d), F32), jax.ShapeDtypeStruct((t, XATT_WIDTH), BF16)],
        compiler_params=_params("parallel"),
        name="out_proj",
    )(x, oa, orr, w_out_bf16, w_out_bf16, gain.reshape(1, d), w_xq_bf16)


def _xattn_kernel(qx_ref, k_ref, v_ref, x1_ref, wo_ref, g_ref, wpq_ref, x2_ref, pq_ref, *maybe_xn_ref):
    heads = []
    for hh in range(XATT_HEADS):
        cols = slice(hh * XATT_HEAD_DIM, (hh + 1) * XATT_HEAD_DIM)
        s = _dot_nt(qx_ref[:, cols], k_ref[:, cols]) * (XATT_HEAD_DIM ** -0.5)
        p = jnp.exp(s - jnp.max(s, axis=-1, keepdims=True))
        p = p / jnp.sum(p, axis=-1, keepdims=True)
        heads.append(_dot(p.astype(BF16), v_ref[:, cols]))
    o = jnp.concatenate(heads, axis=1).astype(BF16)
    x2 = x1_ref[...] + _dot(o, wo_ref[...])
    x2_ref[...] = x2
    xn = _rms(x2, g_ref[...])
    for xn_ref in maybe_xn_ref:
        xn_ref[...] = xn
    pq_ref[...] = _dot(xn.astype(BF16), wpq_ref[...])


def _xattn(qx, kv_mem, x1, w_xo_bf16, gain, w_pq_bf16, batch, seq, mem_len, tm, batch0=0, emit_xn=False):
    t, d = x1.shape
    npq = w_pq_bf16.shape[1]
    nt = seq // tm
    rows = lambda width: pl.BlockSpec((tm, width), lambda b, i: (b * nt + i, 0))
    full = lambda shape: pl.BlockSpec(shape, lambda b, i: (0, 0))
    return pl.pallas_call(
        _xattn_kernel,
        grid=(batch, nt),
        in_specs=[
            rows(XATT_WIDTH),
            pl.BlockSpec((mem_len, XATT_WIDTH), lambda b, i: (b + batch0, 0)),
            pl.BlockSpec((mem_len, XATT_WIDTH), lambda b, i: (b + batch0, 1)),
            rows(d),
            full((XATT_WIDTH, d)),
            full((1, d)),
            full((d, npq)),
        ],
        out_specs=[rows(d), rows(npq)] + ([rows(d)] if emit_xn else []),
        out_shape=[jax.ShapeDtypeStruct((t, d), F32), jax.ShapeDtypeStruct((t, npq), F32)]
        + ([jax.ShapeDtypeStruct((t, d), F32)] if emit_xn else []),
        compiler_params=_params("parallel", "parallel"),
        name="cross_attention",
    )(qx, kv_mem, kv_mem, x1, w_xo_bf16, gain.reshape(1, d), w_pq_bf16)


def _topk_rows(s, k, payload=None):
    nrows = s.shape[0]
    rows = lax.broadcasted_iota(jnp.int32, s.shape, 0).astype(F32)
    vals, sel = [], []
    for _ in range(k):
        m = jnp.max(s, axis=0, keepdims=True)
        am = jnp.min(jnp.where(s == m, rows, float(nrows)), axis=0, keepdims=True)
        hit = rows == am
        vals.append(m)
        sel.append(am if payload is None else jnp.max(jnp.where(hit, payload, -1.0), axis=0, keepdims=True))
        s = jnp.where(hit, -jnp.inf, s)
    return jnp.concatenate(vals, axis=0), jnp.concatenate(sel, axis=0)


def _staircase(t1, t2):
    kk, half = PEER_TOPK, SUBLANES
    blocks = [(t1[0:1, :], t2, kk), (t1[1:2, :], t2[0:half, :], half)]
    blocks += [(t1[a:a + 1, :], t2[0:half, :], kk // (a + 1)) for a in range(2, half)]
    blocks.append((t1[half:kk, :], t2[0:1, :], half))
    return blocks


def _peer_topk_kernel(q_ref, keys_ref, ids_ref, gates_ref, *, heads):
    width = 2 * PEER_HALF_DIM
    row8 = lax.broadcasted_iota(jnp.int32, (SUBLANES, q_ref.shape[0]), 0)
    for hh in range(heads):
        tops = []
        for p in range(2):
            c0 = hh * width + p * PEER_HALF_DIM
            qp = q_ref[:, c0:c0 + PEER_HALF_DIM].astype(BF16)
            tops.append(_topk_rows(_dot_nt(keys_ref[hh, p], qp), PEER_TOPK))
        (s1, i1), (s2, i2) = tops
        cand_s, cand_i = [], []
        for (a_s, b_s, live), (a_i, b_i, _) in zip(_staircase(s1, s2), _staircase(i1 * float(PEER_N_KEYS), i2)):
            blk = a_s + b_s
            if live < blk.shape[0]:
                blk = jnp.where(row8 < live, blk, -jnp.inf)
            cand_s.append(blk)
            cand_i.append(a_i + b_i)
        top_s, top_e = _topk_rows(jnp.concatenate(cand_s, axis=0), PEER_TOPK,
                                  payload=jnp.concatenate(cand_i, axis=0))
        e = jnp.exp(top_s - top_s[0:1, :])
        ids_ref[hh] = top_e.astype(jnp.int32)
        gates_ref[hh] = e / jnp.sum(e, axis=0, keepdims=True)


def _peer_topk(pq, keys_bf16, tt, heads):
    t = pq.shape[0]
    width = 2 * PEER_HALF_DIM
    out = pl.BlockSpec((heads, PEER_TOPK, tt), lambda i, hh: (hh, 0, i))
    return pl.pallas_call(
        functools.partial(_peer_topk_kernel, heads=heads),
        grid=(t // tt, PEER_HEADS // heads),
        in_specs=[
            pl.BlockSpec((tt, heads * width), lambda i, hh: (i, hh)),
            pl.BlockSpec((heads, 2, PEER_N_KEYS, PEER_HALF_DIM), lambda i, hh: (hh, 0, 0, 0)),
        ],
        out_specs=[out, out],
        out_shape=[jax.ShapeDtypeStruct((PEER_HEADS, PEER_TOPK, t), jnp.int32),
                   jax.ShapeDtypeStruct((PEER_HEADS, PEER_TOPK, t), F32)],
        compiler_params=_params("parallel", "parallel"),
        name="peer_topk",
    )(pq, keys_bf16)


GATHER_TOKENS = 128
GATHER_BUFS = 8
GATHER_GROUP = 8
assert GATHER_GROUP % GATHER_BUFS == 0 and GATHER_GROUP % SUBLANES == 0 and GATHER_TOKENS % GATHER_GROUP == 0


def _peer_gather_kernel(ids_hbm, tbl_hbm, x2_ref, gates_ref, fg_ref, og_ref, *rest, final_norm, first_block,
                        has_prev):
    out_ref, ids_smem, ids_sem, *scratch = rest[1:] if has_prev else rest
    gbufs = scratch[:GATHER_BUFS]
    gsem, xn_ref, y_ref, yacc_ref = scratch[GATHER_BUFS:]
    i = pl.program_id(0)
    has_next = i + 1 < pl.num_programs(0)
    slot = i % 2
    other = 1 - slot
    tb = GATHER_TOKENS
    nchunk = D_MODEL // LANES
    nids = tb * PEER_SLOTS
    ahead = GATHER_BUFS - 1
    group = GATHER_GROUP

    def ids_copy(step, sl):
        return pltpu.make_async_copy(ids_hbm.at[pl.ds(pl.multiple_of((first_block + step) * nids, nids), nids)],
                                     ids_smem.at[pl.ds(pl.multiple_of(sl * nids, nids), nids)], ids_sem.at[sl])

    def row_copy(e, b, k):
        return pltpu.make_async_copy(tbl_hbm.at[e], gbufs[b].at[pl.ds(k, 1)], gsem.at[b])

    def issue(sl, tok, b):
        base = sl * nids + tok * PEER_SLOTS
        for k in range(PEER_SLOTS):
            row_copy(ids_smem[base + k], b, k).start(priority=k % 2)

    def wait_rows(b):
        for k in range(PEER_SLOTS):
            row_copy(0, b, k).wait()

    @pl.when(i == 0)
    def _():
        ids_copy(0, 0).start()
        ids_copy(0, 0).wait()
        for tok in range(ahead):
            issue(0, tok, tok)

    @pl.when(has_next)
    def _():
        ids_copy(i + 1, other).start()

    xn_ref[...] = _rms(x2_ref[...], fg_ref[...])

    lane = lax.broadcasted_iota(jnp.int32, (PEER_SLOTS, LANES), 1)
    hi_mask = jnp.uint32(0xFFFF0000)

    def compute(tok, r, b):
        gbuf = gbufs[b]
        xrow = xn_ref[pl.ds(tok, 1), :]
        acc = jnp.zeros((PEER_SLOTS, LANES), F32)
        for c in range(nchunk):
            w = gbuf[:, c * LANES:(c + 1) * LANES]
            u = lax.bitcast_convert_type(w << 16, F32)
            acc = acc + u * xrow[:, c * LANES:(c + 1) * LANES]
        a = jnp.sum(acc, axis=-1, keepdims=True)
        gate = jnp.sum(jnp.where(lane == tok, gates_ref[...], 0.0), axis=-1, keepdims=True)
        coeff = 0.5 * a * (1.0 + lax.erf(a * (2.0 ** -0.5))) * gate
        for c in range(nchunk):
            w = gbuf[:, c * LANES:(c + 1) * LANES]
            vv = lax.bitcast_convert_type(w & hi_mask, F32)
            yacc_ref[r:r + 1, c * LANES:(c + 1) * LANES] = jnp.sum(vv * coeff, axis=0, keepdims=True)

    def run_group(g, last):
        for r in range(group):
            b = r % GATHER_BUFS
            nb = (r + ahead) % GATHER_BUFS
            tok = g * group + r
            wait_rows(b)
            if not last or r + ahead < group:
                issue(slot, tok + ahead, nb)
            else:
                if r + ahead == group:
                    @pl.when(has_next)
                    def _():
                        ids_copy(i + 1, other).wait()

                @pl.when(has_next)
                def _():
                    issue(other, r + ahead - group, nb)
            compute(tok, r, b)
        y_ref[pl.ds(pl.multiple_of(g * group, group), group), :] = yacc_ref[...]

    def body(g, carry):
        run_group(g, last=False)
        return carry

    ngroups = tb // group
    lax.fori_loop(0, ngroups - 1, body, 0)
    run_group(ngroups - 1, last=True)
    x3 = x2_ref[...] + y_ref[...]
    out_ref[...] = _rms(x3, og_ref[...]) if final_norm else x3


def _peer_gather(ids_flat, table, x2, gates_t, ffn_gain, final_gain, final_norm, first_block, out_rows,
                 out_block0, prev):
    t, d = x2.shape
    tb = GATHER_TOKENS
    nsteps = t // tb - first_block
    full = lambda shape: pl.BlockSpec(shape, lambda i: (0, 0))
    has_prev = prev is not None
    return pl.pallas_call(
        functools.partial(_peer_gather_kernel, final_norm=final_norm, first_block=first_block, has_prev=has_prev),
        grid=(nsteps,),
        in_specs=[
            pl.BlockSpec(memory_space=pl.ANY),
            pl.BlockSpec(memory_space=pl.ANY),
            pl.BlockSpec((tb, d), lambda i: (i + first_block, 0)),
            pl.BlockSpec((PEER_SLOTS, tb), lambda i: (0, i + first_block)),
            full((1, d)),
            full((1, d)),
        ] + ([pl.BlockSpec(memory_space=pl.ANY)] if has_prev else []),
        out_specs=pl.BlockSpec((tb, d), lambda i: (i + first_block + out_block0, 0)),
        out_shape=jax.ShapeDtypeStruct((out_rows, d), F32),
        input_output_aliases={6: 0} if has_prev else {},
        scratch_shapes=[
            pltpu.SMEM((2 * tb * PEER_SLOTS,), jnp.int32),
            pltpu.SemaphoreType.DMA((2,)),
            *[pltpu.VMEM((PEER_SLOTS, d), jnp.uint32) for _ in range(GATHER_BUFS)],
            pltpu.SemaphoreType.DMA((GATHER_BUFS,)),
            pltpu.VMEM((tb, d), F32),
            pltpu.VMEM((tb, d), F32),
            pltpu.VMEM((GATHER_GROUP, d), F32),
        ],
        compiler_params=_params("arbitrary"),
        name="peer_gather",
    )(ids_flat, table, x2, gates_t, ffn_gain.reshape(1, d), final_gain.reshape(1, d), *([prev] if has_prev else []))


SC_CORES, SC_SUBCORES, SC_LANES = 2, 16, 16
SC_WORKERS = SC_CORES * SC_SUBCORES
SC_ROWS = SC_LANES
SC_COLS = 8
SC_DOT_ROWS = 8
SC_GROUPS, SC_GROUP_DENOM = (1, 1, 2, 4), 8
SC_SHARE = (63, 128)


def _sc_erf(x):
    ax = jnp.abs(x)
    t = 1.0 / (1.0 + 0.3275911 * ax)
    poly = t * (0.254829592 + t * (-0.284496736 + t * (1.421413741 + t * (-1.453152027 + t * 1.061405429))))
    y = 1.0 - poly * jnp.exp(-ax * ax)
    return jnp.where(x < 0, -y, y)


def _peer_sc(ids_flat, gates_tok, xn, table, ts):
    d = D_MODEL
    lanes, rows = SC_LANES, SC_ROWS
    nchunk = PEER_SLOTS // rows
    block = lanes * SC_COLS
    per_worker = ts // SC_WORKERS
    assert ts % (2 * SC_WORKERS) == 0 and d % block == 0 and nchunk % 2 == 0
    mesh = plsc.VectorSubcoreMesh(core_axis_name="c", subcore_axis_name="s")

    @functools.partial(
        pl.kernel, mesh=mesh,
        out_type=jax.ShapeDtypeStruct((ts, d), F32),
        scratch_types=[
            pltpu.VMEM((2, PEER_SLOTS), jnp.int32),
            pltpu.VMEM((2, PEER_SLOTS), F32),
            pltpu.VMEM((2, d), F32),
            pltpu.VMEM((d,), F32),
            pltpu.VMEM((2, rows, 1, d), jnp.uint32),
            pltpu.VMEM((rows, lanes), F32),
            pltpu.VMEM((lanes,), F32),
            pltpu.VMEM((rows, lanes), jnp.int32),
            pltpu.SemaphoreType.DMA((2,)),
            pltpu.SemaphoreType.DMA((2,)),
        ],
        compiler_params=pltpu.CompilerParams(needs_layout_passes=False),
        name="peer_sparsecore",
    )
    def run(ids_hbm, gates_hbm, xn_hbm, tbl_hbm, y_hbm, ids_v, gates_v, x_v, y_v, rows_v, acc_v, coeff_v, ridx_v,
            row_sem, in_sem):
        worker = lax.axis_index("s") * SC_CORES + lax.axis_index("c")
        first = worker * per_worker
        lane = lax.iota(jnp.int32, lanes)
        zero = jnp.zeros((lanes,), F32)
        hi_mask = jnp.full((lanes,), 0xFFFF0000, jnp.uint32)
        for r in range(rows):
            ridx_v[r, :] = jnp.full((lanes,), r, jnp.int32)

        def gather(par, c, slot):
            idx = ids_v[par, pl.ds(pl.multiple_of(c * rows, rows), rows)]
            return pltpu.make_async_copy(tbl_hbm.at[idx], rows_v.at[slot], row_sem.at[slot])

        def token_inputs(tok, par):
            return (pltpu.make_async_copy(ids_hbm.at[pl.ds(tok * PEER_SLOTS, PEER_SLOTS)], ids_v.at[par], in_sem.at[par]),
                    pltpu.make_async_copy(gates_hbm.at[pl.ds(tok * PEER_SLOTS, PEER_SLOTS)], gates_v.at[par],
                                          in_sem.at[par]),
                    pltpu.make_async_copy(xn_hbm.at[tok], x_v.at[par], in_sem.at[par]))

        def chunk(par, c, slot):
            for r0 in range(0, rows, SC_DOT_ROWS):
                def dot_body(j, accs, r0=r0):
                    xj = x_v[par, pl.ds(j * lanes, lanes)]
                    return tuple(
                        accs[q] + plsc.bitcast(rows_v[slot, r0 + q, 0, pl.ds(j * lanes, lanes)] << 16, F32) * xj
                        for q in range(SC_DOT_ROWS))
                accs = lax.fori_loop(0, d // lanes, dot_body, (zero,) * SC_DOT_ROWS, unroll=4)
                for q in range(SC_DOT_ROWS):
                    acc_v[r0 + q, :] = accs[q]
            a = zero
            for col in range(lanes):
                a = a + plsc.load_gather(acc_v, [lane, ridx_v[col, :]])
            gate = gates_v[par, pl.ds(pl.multiple_of(c * rows, rows), rows)]
            coeff_v[...] = 0.5 * a * (1.0 + _sc_erf(a * (2.0 ** -0.5))) * gate
            coeffs = [plsc.load_gather(coeff_v, [ridx_v[r, :]]) for r in range(rows)]

            @pl.loop(0, d // block)
            def _(jb):
                base = jb * block
                accs = [zero] * SC_COLS
                for r in range(rows):
                    for q in range(SC_COLS):
                        w = rows_v[slot, r, 0, pl.ds(base + q * lanes, lanes)]
                        accs[q] = accs[q] + plsc.bitcast(w & hi_mask, F32) * coeffs[r]
                for q in range(SC_COLS):
                    plsc.addupdate(y_v.at[pl.ds(base + q * lanes, lanes)], accs[q])

        for cp in token_inputs(first, 0):
            cp.start()

        @pl.loop(0, per_worker // 2)
        def _(pair):
            for par in range(2):
                i = pair * 2 + par
                tok = first + i
                for cp in token_inputs(tok, par):
                    cp.wait()

                @pl.when(i + 1 < per_worker)
                def _():
                    for cp in token_inputs(tok + 1, 1 - par):
                        cp.start()

                @pl.loop(0, d // lanes)
                def _(j):
                    y_v[pl.ds(j * lanes, lanes)] = zero

                gather(par, 0, 0).start()

                @pl.loop(0, nchunk // 2)
                def _(cpair):
                    for slot in range(2):
                        c = cpair * 2 + slot
                        if slot == 0:
                            gather(par, c + 1, 1).start()
                        else:
                            @pl.when(c + 1 < nchunk)
                            def _():
                                gather(par, c + 1, 0).start()
                        gather(par, c, slot).wait()
                        chunk(par, c, slot)

                pltpu.sync_copy(y_v, y_hbm.at[tok])

    return run(ids_flat, gates_tok, xn, table)


def _peer_finish_kernel(x2_ref, y_ref, og_ref, prev_ref, out_ref, *, final_norm):
    del prev_ref
    x3 = x2_ref[...] + y_ref[...]
    out_ref[...] = _rms(x3, og_ref[...]) if final_norm else x3


def _peer_finish(x2, y_head, out_rest, final_gain, final_norm, out_block0):
    rows, d = y_head.shape
    tm = GATHER_TOKENS
    blk = pl.BlockSpec((tm, d), lambda i: (i, 0))
    return pl.pallas_call(
        functools.partial(_peer_finish_kernel, final_norm=final_norm),
        grid=(rows // tm,),
        in_specs=[blk, blk, pl.BlockSpec((1, d), lambda i: (0, 0)), pl.BlockSpec(memory_space=pl.ANY)],
        out_specs=pl.BlockSpec((tm, d), lambda i: (i + out_block0, 0)),
        out_shape=jax.ShapeDtypeStruct(out_rest.shape, F32),
        input_output_aliases={3: 0},
        compiler_params=_params("parallel"),
        name="peer_finish",
    )(x2, y_head, final_gain.reshape(1, d), out_rest)


def _rope_angles(positions, dh):
    inv_freq = ROPE_THETA ** (-jnp.arange(0, dh, 2, dtype=F32) / dh)
    ang = positions.astype(F32).reshape(-1, 1) * inv_freq
    return jnp.cos(ang), jnp.sin(ang)


def _rope_tables_att(positions):
    cos, sin = _rope_angles(positions, HEAD_DIM)
    copies = LANES // HEAD_DIM
    return (jnp.tile(jnp.concatenate([cos, cos], axis=1), (1, copies)),
            jnp.tile(jnp.concatenate([-sin, sin], axis=1), (1, copies)))


def _pack_table_kernel(u_ref, v_ref, o_ref):
    ub = lax.bitcast_convert_type(u_ref[...].astype(BF16).astype(F32), jnp.uint32)
    vb = lax.bitcast_convert_type(v_ref[...].astype(BF16).astype(F32), jnp.uint32)
    o_ref[:, 0, :] = (ub >> 16) | vb


def _pack_expert_table(u, v, rows=PACK_ROWS):
    n, d = u.shape
    blk = pl.BlockSpec((rows, d), lambda i: (i, 0))
    return pl.pallas_call(
        _pack_table_kernel,
        grid=(n // rows,),
        in_specs=[blk, blk],
        out_specs=pl.BlockSpec((rows, 1, d), lambda i: (i, 0, 0)),
        out_shape=jax.ShapeDtypeStruct((n, 1, d), jnp.uint32),
        compiler_params=_params("parallel"),
        name="pack_table",
    )(u, v)


def kernel(x, mem, positions, mix_norm_gain, w_in, att_sinks, att_out_gain, ret_out_gain, w_out,
           cross_norm_gain, mem_norm_gain, w_xq, w_xk, w_xv, w_xo, ffn_norm_gain,
           w_peer_q, peer_sub_keys, peer_u, peer_v, final_norm_gain):
    batch, seq, d = x.shape
    mem_len = mem.shape[1]
    t = batch * seq
    depth = w_in.shape[0]
    assert d == D_MODEL and seq % BLOCK == 0 and t % GATHER_TOKENS == 0

    cos_a, sin_a = _rope_tables_att(positions)
    cos_r, sin_r = _rope_angles(positions, RET_HEAD_DIM)
    xf = x.reshape(t, d)
    memf = mem.reshape(batch * mem_len, d)
    nblocks = t // GATHER_TOKENS
    split = batch % SC_GROUP_DENOM == 0 and nblocks % SC_SHARE[1] == 0
    groups = [batch * g // SC_GROUP_DENOM for g in SC_GROUPS] if split else [batch]
    sc_total = nblocks * SC_SHARE[0] // SC_SHARE[1] if split else 0
    tm_mid = min(ROW_TM, seq)
    for l in range(depth):
        sc_left = sc_total
        w_in_b = _reorder_w_in(w_in[l]).astype(BF16)
        w_out_b, w_xq_b, w_xo_b, w_pq_b = (w.astype(BF16) for w in (w_out[l], w_xq[l], w_xo[l], w_peer_q[l]))
        keys_b = peer_sub_keys[l].astype(BF16)
        w_kv = jnp.concatenate([w_xk[l], w_xv[l]], axis=1).astype(BF16)
        kv_mem = _norm_matmul(memf, mem_norm_gain[l], w_kv, mem_len, 2 * XATT_WIDTH, BF16)
        table = _pack_expert_table(peer_u[l], peer_v[l])
        last = l == depth - 1
        out = None
        pending = []
        b0 = 0
        for bp in groups:
            r0, tp = b0 * seq, bp * seq
            block0, blocks_p = r0 // GATHER_TOKENS, tp // GATHER_TOKENS
            rows = slice(r0, r0 + tp)
            h = _norm_matmul(xf, mix_norm_gain[l], w_in_b, min(IN_PROJ_TM, tp), IN_PROJ_TN, BF16, row0=r0, rows=tp)
            oa = _swa(h, cos_a[rows], sin_a[rows], att_sinks[l], att_out_gain[l], bp, seq)
            orr = _retention(h, cos_r[rows], sin_r[rows], ret_out_gain[l], bp, seq)
            x1, qx = _outproj(xf, oa, orr, w_out_b, cross_norm_gain[l], w_xq_b, tm_mid, row0=r0)
            sc_blocks = min(sc_left, blocks_p)
            sc_left -= sc_blocks
            x2, pq, *xn_sc = _xattn(qx, kv_mem, x1, w_xo_b, ffn_norm_gain[l], w_pq_b, bp, seq, mem_len, tm_mid,
                                    batch0=b0, emit_xn=sc_blocks > 0)
            ids_t, gates_t = _peer_topk(pq, keys_b, LANES, PEER_HEADS)
            ids = ids_t.reshape(PEER_SLOTS, tp).T.reshape(tp * PEER_SLOTS)
            gates_t = gates_t.reshape(PEER_SLOTS, tp)
            if sc_blocks:
                ts = sc_blocks * GATHER_TOKENS
                y_head = _peer_sc(ids, gates_t[:, :ts].T.reshape(ts * PEER_SLOTS), xn_sc[0], table, ts)
                pending.append((x2, y_head, block0))
            if sc_blocks < blocks_p:
                out = _peer_gather(ids, table, x2, gates_t, ffn_norm_gain[l], final_norm_gain, last, sc_blocks,
                                   t, block0, out)
            b0 += bp
        for x2, y_head, block0 in pending:
            out = _peer_finish(x2, y_head, out, final_norm_gain, last, block0)
        xf = out
    return xf.reshape(batch, seq, d)
```
